```python
import jax, jax.numpy as jnp
from jax import lax
import numpy as np

D_MODEL = 2048
BATCH = 4
SEQ = 2048
DEPTH = 4

D_MIX = D_MODEL
N_MIXERS = 4
W_BR = D_MIX // N_MIXERS
HEAD_DIM = 128
N_HEADS_BR = W_BR // HEAD_DIM
GLA_DK = W_BR // 2
GLA_HEAD_K = GLA_DK // N_HEADS_BR
GLA_LOWRANK = 16
GLA_TAU = 16.0
GLA_CHUNK = 64
LRU_C = 8.0
LRU_CONV = 4
NSA_DH = HEAD_DIM
CMP_LEN = 32
CMP_STRIDE = 16
CMP_HIDDEN = 128
SLC_LEN = 64
SLC_TOPN = 16
SLC_Q_BLOCK = 64
WIN = 512
Q_BLOCK = 128
WIN_BLOCKS = WIN // Q_BLOCK
FORCED_SCORE = 1e3
CONF_KERNEL = 31
EPS = 1e-6

PROJ_SIZES = (
    GLA_DK, GLA_DK, W_BR, GLA_LOWRANK, W_BR,
    W_BR, W_BR,
    W_BR, NSA_DH, NSA_DH, NSA_DH, NSA_DH, NSA_DH, NSA_DH,
    3 * N_HEADS_BR, W_BR,
    W_BR, W_BR, W_BR,
)
D_PROJ = sum(PROJ_SIZES)

kernel_name = 'hybrid_gla_rglru_nsa_conformer_trunk'

F32 = jnp.float32


def rmsnorm(x, g):
    x32 = x.astype(F32)
    y = x32 * lax.rsqrt(jnp.mean(x32 * x32, axis=-1, keepdims=True) + EPS)
    return (y * g.astype(F32)).astype(x.dtype)


def group_rmsnorm(y, g):
    shp = y.shape
    y32 = y.astype(F32).reshape(shp[:-1] + (shp[-1] // HEAD_DIM, HEAD_DIM))
    y32 = y32 * lax.rsqrt(jnp.mean(y32 * y32, axis=-1, keepdims=True) + EPS)
    return (y32.reshape(shp) * g.astype(F32)).astype(y.dtype)


def layernorm(x, g, b):
    x32 = x.astype(F32)
    mu = jnp.mean(x32, axis=-1, keepdims=True)
    xc = x32 - mu
    var = jnp.mean(xc * xc, axis=-1, keepdims=True)
    return (xc * lax.rsqrt(var + EPS) * g.astype(F32) + b.astype(F32)).astype(x.dtype)


def causal_dwconv(x, w, b):
    width, ch = w.shape
    y = lax.conv_general_dilated(x, w[:, None, :].astype(x.dtype), window_strides=(1,),
                                 padding=((width - 1, 0),),
                                 dimension_numbers=('NWC', 'WIO', 'NWC'),
                                 feature_group_count=ch)
    return y + b.astype(x.dtype)


def masked_softmax(s, mask):
    s = jnp.where(mask, s.astype(F32), -jnp.inf)
    m = jnp.max(s, axis=-1, keepdims=True)
    m = jnp.where(jnp.isfinite(m), m, 0.0)
    p = jnp.exp(s - m)
    return p / jnp.maximum(jnp.sum(p, axis=-1, keepdims=True), 1e-30)


def gla_mixer(q, k, v, fg, w_fg2, b_fg2):
    B, S, _ = q.shape
    H, C = N_HEADS_BR, GLA_CHUNK
    N = S // C

    def chunks(t, d):
        return t.reshape(B, N, C, H, d).transpose(0, 3, 1, 2, 4)

    log_f = jax.nn.log_sigmoid((fg @ w_fg2 + b_fg2).astype(F32)) / GLA_TAU
    qc = chunks(q.astype(F32) * GLA_HEAD_K ** -0.5, GLA_HEAD_K)
    kc = chunks(k.astype(F32), GLA_HEAD_K)
    vc = chunks(v.astype(F32), HEAD_DIM)
    bcum = jnp.cumsum(chunks(log_f, GLA_HEAD_K), axis=3)
    b_last = bcum[:, :, :, -1:, :]
    q_dec = qc * jnp.exp(bcum)
    causal = jnp.tril(jnp.ones((C, C), dtype=bool))
    attn = jnp.where(causal, jnp.einsum('bhncd,bhnjd->bhncj', q_dec, kc * jnp.exp(-bcum)), 0.0)
    o_intra = jnp.einsum('bhncj,bhnje->bhnce', attn, vc)
    dstate = jnp.einsum('bhncd,bhnce->bhnde', kc * jnp.exp(b_last - bcum), vc)
    decay = jnp.exp(b_last[:, :, :, 0, :])

    def step(state, inp):
        dec, ds = inp
        return dec[..., None] * state + ds, state

    _, s_prev = lax.scan(step, jnp.zeros((B, H, GLA_HEAD_K, HEAD_DIM), F32),
                         (jnp.moveaxis(decay, 2, 0), jnp.moveaxis(dstate, 2, 0)))
    s_prev = jnp.moveaxis(s_prev, 0, 2)
    o = o_intra + jnp.einsum('bhncd,bhnde->bhnce', q_dec, s_prev)
    return o.transpose(0, 2, 3, 1, 4).reshape(B, S, H * HEAD_DIM)


def rglru_mixer(xb, conv_w, conv_b, w_a, b_a, w_x, b_x, lam):
    B, S, W = xb.shape
    xc = causal_dwconv(xb, conv_w, conv_b)
    xh = xc.reshape(B, S, N_HEADS_BR, HEAD_DIM)
    r = jax.nn.sigmoid((jnp.einsum('bshi,hij->bshj', xh, w_a).reshape(B, S, W) + b_a).astype(F32))
    ig = jax.nn.sigmoid((jnp.einsum('bshi,hij->bshj', xh, w_x).reshape(B, S, W) + b_x).astype(F32))
    log_a = -LRU_C * r * jax.nn.softplus(-lam.astype(F32))
    a = jnp.exp(log_a)
    u = jnp.sqrt(-jnp.expm1(2.0 * log_a)) * (ig * xc.astype(F32))

    def combine(left, right):
        a1, b1 = left
        a2, b2 = right
        return a1 * a2, a2 * b1 + b2

    _, h = lax.associative_scan(combine, (a, u), axis=1)
    return h


def nsa_mixer(q, kc, vc, ks, vs, kw, vw, gates, pos_k, w1_k, w2_k, pos_v, w1_v, w2_v):
    B, S, _ = q.shape
    H, Dh = N_HEADS_BR, NSA_DH
    scale = Dh ** -0.5
    qh = q.reshape(B, S, H, Dh)
    t = jnp.arange(S)

    n_cmp = (S - CMP_LEN) // CMP_STRIDE + 1
    cstart = jnp.arange(n_cmp) * CMP_STRIDE
    blk = cstart[:, None] + jnp.arange(CMP_LEN)[None]

    def compress(z, pos, w1, w2):
        zb = (z[:, blk] + pos).reshape(B, n_cmp, CMP_LEN * Dh)
        return jax.nn.silu(zb @ w1) @ w2

    k_cmp = compress(kc, pos_k, w1_k, w2_k)
    v_cmp = compress(vc, pos_v, w1_v, w2_v)
    cmp_mask = (t[:, None] >= (cstart + CMP_LEN - 1)[None])[None, None]
    p_cmp = masked_softmax(jnp.einsum('bshd,bnd->bhsn', qh, k_cmp) * scale, cmp_mask)
    o_cmp = jnp.einsum('bhsn,bnd->bshd', p_cmp, v_cmp)

    n_sel = S // SLC_LEN
    j = jnp.arange(n_sel)
    overlap = ((cstart[:, None] < (j[None] + 1) * SLC_LEN) &
               (cstart[:, None] + CMP_LEN > j[None] * SLC_LEN)).astype(F32)
    imp = jnp.einsum('bhsn,nj->bsj', p_cmp, overlap)
    cur = (t // SLC_LEN)[:, None]
    forced = (j[None] == 0) | (j[None] == cur) | (j[None] == cur - 1)
    imp = jnp.where(j[None] > cur, -jnp.inf, jnp.where(forced, FORCED_SCORE, imp))
    n_top = min(SLC_TOPN, n_sel)
    _, sel = lax.top_k(imp, n_top)

    ks_blk = ks.reshape(B, n_sel, SLC_LEN, Dh)
    vs_blk = vs.reshape(B, n_sel, SLC_LEN, Dh)
    nq = S // SLC_Q_BLOCK

    def sel_block(args):
        q_b, sel_b, t_b = args
        kg = jax.vmap(lambda kb, ib: kb[ib])(ks_blk, sel_b)
        vg = jax.vmap(lambda vb, ib: vb[ib])(vs_blk, sel_b)
        kpos = sel_b[..., None] * SLC_LEN + jnp.arange(SLC_LEN)
        mask = (kpos <= t_b[None, :, None, None]).reshape(B, SLC_Q_BLOCK, 1, n_top * SLC_LEN)
        s = jnp.einsum('bqhd,bqkld->bqhkl', q_b, kg).reshape(B, SLC_Q_BLOCK, H, n_top * SLC_LEN) * scale
        p = masked_softmax(s, mask).reshape(B, SLC_Q_BLOCK, H, n_top, SLC_LEN)
        return jnp.einsum('bqhkl,bqkld->bqhd', p, vg)

    o_slc = lax.map(sel_block, (qh.reshape(B, nq, SLC_Q_BLOCK, H, Dh).swapaxes(0, 1),
                                sel.reshape(B, nq, SLC_Q_BLOCK, n_top).swapaxes(0, 1),
                                t.reshape(nq, SLC_Q_BLOCK)))
    o_slc = o_slc.swapaxes(0, 1).reshape(B, S, H, Dh)

    nb = S // Q_BLOCK
    pad = WIN_BLOCKS * Q_BLOCK

    def band(z):
        zb = jnp.pad(z, ((0, 0), (pad, 0), (0, 0))).reshape(B, nb + WIN_BLOCKS, Q_BLOCK, Dh)
        return jnp.concatenate([zb[:, i:i + nb] for i in range(WIN_BLOCKS + 1)], axis=2)

    kwin, vwin = band(kw), band(vw)
    qpos = t.reshape(nb, Q_BLOCK)
    kpos = (jnp.arange(nb)[:, None] - WIN_BLOCKS) * Q_BLOCK + jnp.arange((WIN_BLOCKS + 1) * Q_BLOCK)[None]
    dist = qpos[:, :, None] - kpos[:, None, :]
    wmask = (dist >= 0) & (dist < WIN) & (kpos[:, None, :] >= 0)
    s = jnp.einsum('bnqhd,bnkd->bnhqk', qh.reshape(B, nb, Q_BLOCK, H, Dh), kwin) * scale
    p = masked_softmax(s, wmask[None, :, None])
    o_win = jnp.einsum('bnhqk,bnkd->bnqhd', p, vwin).reshape(B, S, H, Dh)

    g = jax.nn.sigmoid(gates.astype(F32)).reshape(B, S, H, 3)
    o = g[..., 0:1] * o_cmp + g[..., 1:2] * o_slc + g[..., 2:3] * o_win
    return o.reshape(B, S, H * Dh)


def conformer_mixer(val, glu, dw_w, dw_b, ln_g, ln_b, pw_w, pw_b):
    y = val * jax.nn.sigmoid(glu)
    y = causal_dwconv(y, dw_w, dw_b)
    y = layernorm(y, ln_g, ln_b)
    return jax.nn.silu(y) @ pw_w + pw_b


def hybrid_layer(x, pre_g, w_in, gla_w_fg2, gla_b_fg2, lru_conv_w, lru_conv_b, lru_w_a, lru_b_a,
                 lru_w_x, lru_b_x, lru_lambda, cmp_pos_k, cmp_w1_k, cmp_w2_k, cmp_pos_v, cmp_w1_v,
                 cmp_w2_v, conf_dw_w, conf_dw_b, conf_ln_g, conf_ln_b, conf_pw_w, conf_pw_b,
                 branch_g, w_out, post_g):
    dt = x.dtype
    h = rmsnorm(x, pre_g)
    proj = h @ w_in
    (gla_q, gla_k, gla_v, gla_fg, gla_z, lru_x, lru_z, nsa_q, nsa_kc, nsa_vc, nsa_ks, nsa_vs,
     nsa_kw, nsa_vw, nsa_g, nsa_z, conv_v, conv_glu, conv_z) = jnp.split(
        proj, np.cumsum(PROJ_SIZES)[:-1].tolist(), axis=-1)
    o_a = gla_mixer(gla_q, gla_k, gla_v, gla_fg, gla_w_fg2, gla_b_fg2)
    o_b = rglru_mixer(lru_x, lru_conv_w, lru_conv_b, lru_w_a, lru_b_a, lru_w_x, lru_b_x, lru_lambda)
    o_c = nsa_mixer(nsa_q, nsa_kc, nsa_vc, nsa_ks, nsa_vs, nsa_kw, nsa_vw, nsa_g,
                    cmp_pos_k, cmp_w1_k, cmp_w2_k, cmp_pos_v, cmp_w1_v, cmp_w2_v)
    o_d = conformer_mixer(conv_v, conv_glu, conf_dw_w, conf_dw_b, conf_ln_g, conf_ln_b, conf_pw_w, conf_pw_b)
    mixed = jnp.concatenate([o_a.astype(dt), o_b.astype(dt), o_c.astype(dt), o_d.astype(dt)], axis=-1)
    gate = jax.nn.silu(jnp.concatenate([gla_z, lru_z, nsa_z, conv_z], axis=-1))
    mixed = (group_rmsnorm(mixed, branch_g) * gate).astype(dt)
    y = mixed @ w_out
    return x + rmsnorm(y, post_g)


def setup_inputs(seed: int = 0) -> dict:
    key = jax.random.key(seed)
    keys = iter(jax.random.split(key, 32))
    L = DEPTH

    def nrm(shape, s):
        return jax.random.normal(next(keys), shape, F32) * s

    def gain(shape):
        return 1.0 + nrm(shape, 0.02)

    x = nrm((BATCH, SEQ, D_MODEL), 1.0)
    pre_norm_g = gain((L, D_MODEL))
    w_in = nrm((L, D_MODEL, D_PROJ), D_MODEL ** -0.5)
    gla_w_fg2 = nrm((L, GLA_LOWRANK, GLA_DK), GLA_LOWRANK ** -0.5)
    gla_b_fg2 = nrm((L, GLA_DK), 0.01)
    lru_conv_w = nrm((L, LRU_CONV, W_BR), LRU_CONV ** -0.5)
    lru_conv_b = nrm((L, W_BR), 0.01)
    lru_w_a = nrm((L, N_HEADS_BR, HEAD_DIM, HEAD_DIM), HEAD_DIM ** -0.5)
    lru_b_a = nrm((L, W_BR), 0.01)
    lru_w_x = nrm((L, N_HEADS_BR, HEAD_DIM, HEAD_DIM), HEAD_DIM ** -0.5)
    lru_b_x = nrm((L, W_BR), 0.01)
    u = jax.random.uniform(next(keys), (L, W_BR), F32, 0.9, 0.999)
    a0 = u ** (1.0 / LRU_C)
    lru_lambda = jnp.log(a0) - jnp.log1p(-a0)
    nsa_cmp_pos_k = nrm((L, CMP_LEN, NSA_DH), 0.02)
    nsa_cmp_w1_k = nrm((L, CMP_LEN * NSA_DH, CMP_HIDDEN), (CMP_LEN * NSA_DH) ** -0.5)
    nsa_cmp_w2_k = nrm((L, CMP_HIDDEN, NSA_DH), CMP_HIDDEN ** -0.5)
    nsa_cmp_pos_v = nrm((L, CMP_LEN, NSA_DH), 0.02)
    nsa_cmp_w1_v = nrm((L, CMP_LEN * NSA_DH, CMP_HIDDEN), (CMP_LEN * NSA_DH) ** -0.5)
    nsa_cmp_w2_v = nrm((L, CMP_HIDDEN, NSA_DH), CMP_HIDDEN ** -0.5)
    conf_dw_w = nrm((L, CONF_KERNEL, W_BR), CONF_KERNEL ** -0.5)
    conf_dw_b = nrm((L, W_BR), 0.01)
    conf_ln_g = gain((L, W_BR))
    conf_ln_b = nrm((L, W_BR), 0.01)
    conf_pw_w = nrm((L, W_BR, W_BR), W_BR ** -0.5)
    conf_pw_b = nrm((L, W_BR), 0.01)
    branch_norm_g = gain((L, D_MIX))
    w_out = nrm((L, D_MIX, D_MODEL), D_MIX ** -0.5)
    post_norm_g = gain((L, D_MODEL))
    return {'x': x, 'pre_norm_g': pre_norm_g, 'w_in': w_in, 'gla_w_fg2': gla_w_fg2,
            'gla_b_fg2': gla_b_fg2, 'lru_conv_w': lru_conv_w, 'lru_conv_b': lru_conv_b,
            'lru_w_a': lru_w_a, 'lru_b_a': lru_b_a, 'lru_w_x': lru_w_x, 'lru_b_x': lru_b_x,
            'lru_lambda': lru_lambda, 'nsa_cmp_pos_k': nsa_cmp_pos_k, 'nsa_cmp_w1_k': nsa_cmp_w1_k,
            'nsa_cmp_w2_k': nsa_cmp_w2_k, 'nsa_cmp_pos_v': nsa_cmp_pos_v, 'nsa_cmp_w1_v': nsa_cmp_w1_v,
            'nsa_cmp_w2_v': nsa_cmp_w2_v, 'conf_dw_w': conf_dw_w, 'conf_dw_b': conf_dw_b,
            'conf_ln_g': conf_ln_g, 'conf_ln_b': conf_ln_b, 'conf_pw_w': conf_pw_w,
            'conf_pw_b': conf_pw_b, 'branch_norm_g': branch_norm_g, 'w_out': w_out,
            'post_norm_g': post_norm_g}


def reference(x, pre_norm_g, w_in, gla_w_fg2, gla_b_fg2, lru_conv_w, lru_conv_b, lru_w_a, lru_b_a,
              lru_w_x, lru_b_x, lru_lambda, nsa_cmp_pos_k, nsa_cmp_w1_k, nsa_cmp_w2_k, nsa_cmp_pos_v,
              nsa_cmp_w1_v, nsa_cmp_w2_v, conf_dw_w, conf_dw_b, conf_ln_g, conf_ln_b, conf_pw_w,
              conf_pw_b, branch_norm_g, w_out, post_norm_g):
    for l in range(DEPTH):
        x = hybrid_layer(x, pre_norm_g[l], w_in[l], gla_w_fg2[l], gla_b_fg2[l], lru_conv_w[l],
                         lru_conv_b[l], lru_w_a[l], lru_b_a[l], lru_w_x[l], lru_b_x[l], lru_lambda[l],
                         nsa_cmp_pos_k[l], nsa_cmp_w1_k[l], nsa_cmp_w2_k[l], nsa_cmp_pos_v[l],
                         nsa_cmp_w1_v[l], nsa_cmp_w2_v[l], conf_dw_w[l], conf_dw_b[l], conf_ln_g[l],
                         conf_ln_b[l], conf_pw_w[l], conf_pw_b[l], branch_norm_g[l], w_out[l],
                         post_norm_g[l])
    return x
```

```python
import functools

import jax
import jax.numpy as jnp
from jax import lax
from jax.experimental import pallas as pl
from jax.experimental.pallas import tpu as pltpu

F32 = jnp.float32
BF16 = jnp.bfloat16

D_MODEL = 2048
N_HEADS = 4
HEAD_DIM = 128
W_BR = N_HEADS * HEAD_DIM
GLA_HEAD_K = 64
GLA_TAU = 16.0
GLA_CHUNK = 64
LRU_C = 8.0
LRU_CONV = 4
CMP_LEN = 32
CMP_STRIDE = 16
SLC_LEN = 64
SLC_TOPN = 16
WIN = 512
Q_BLOCK = 128
FORCED_SCORE = 1e3
CONF_KERNEL = 31
EPS = 1e-6
LANES = 128
VMEM_LIMIT = 48 * 1024 * 1024

_SEGS = (('gla_q', 256), ('gla_k', 256), ('gla_v', 512), ('gla_fg', 16), ('gla_z', 512),
         ('lru_x', 512), ('lru_z', 512),
         ('nsa_q', 512), ('nsa_kc', 128), ('nsa_vc', 128), ('nsa_ks', 128), ('nsa_vs', 128),
         ('nsa_kw', 128), ('nsa_vw', 128), ('nsa_g', 12), ('nsa_z', 512),
         ('conv_v', 512), ('conv_glu', 512), ('conv_z', 512))
_ORDER = ('gla_v', 'gla_z', 'lru_x', 'lru_z', 'nsa_q', 'nsa_z', 'conv_v', 'conv_glu', 'conv_z',
          'gla_q', 'gla_k', 'gla_fg', 'nsa_kc', 'nsa_vc', 'nsa_ks', 'nsa_vs', 'nsa_kw', 'nsa_vw',
          'nsa_g')


def _layout():
    src, off = {}, 0
    for name, w in _SEGS:
        src[name] = (off, w)
        off += w
    dst, off = {}, 0
    for name in _ORDER:
        w = src[name][1]
        wp = -(-w // LANES) * LANES
        assert off % wp == 0
        dst[name] = (off, wp)
        off += wp
    return src, dst, off


_SRC, _DST, D_PROJ_PAD = _layout()


def _relayout_w_in(w_in):
    parts = []
    for name in _ORDER:
        o, w = _SRC[name]
        wp = _DST[name][1]
        p = w_in[..., o:o + w]
        if wp != w:
            p = jnp.pad(p, ((0, 0), (0, 0), (0, wp - w)))
        parts.append(p)
    return jnp.concatenate(parts, axis=-1).astype(BF16)


def _dot(a, b):
    return jnp.dot(a, b, preferred_element_type=F32)


def _dot_t(a, b):
    return lax.dot_general(a, b, (((1,), (1,)), ((), ())), preferred_element_type=F32)


def _softplus(x):
    return jnp.maximum(x, 0.0) + jnp.log1p(jnp.exp(-jnp.abs(x)))


def _expm1(x):
    return jnp.tanh(0.5 * x) * (jnp.exp(x) + 1.0)


def _silu(x):
    return x * jax.nn.sigmoid(x)


def _masked_softmax(s, mask):
    s = jnp.where(mask, s, -jnp.inf)
    m = jnp.max(s, axis=-1, keepdims=True)
    m = jnp.where(jnp.isfinite(m), m, 0.0)
    p = jnp.exp(s - m)
    return p / jnp.maximum(jnp.sum(p, axis=-1, keepdims=True), 1e-30)


def _head_norm_gate(o, z, g):
    outs = []
    for h in range(N_HEADS):
        oh = o[:, h * HEAD_DIM:(h + 1) * HEAD_DIM]
        outs.append(oh * lax.rsqrt(jnp.mean(oh * oh, axis=-1, keepdims=True) + EPS))
    on = jnp.concatenate(outs, axis=-1) * g
    return (on * _silu(z)).astype(BF16)


def _in_proj_kernel(x_ref, g_ref, w_ref, o_ref, hn_ref):
    @pl.when(pl.program_id(1) == 0)
    def _():
        x = x_ref[...]
        y = x * lax.rsqrt(jnp.mean(x * x, axis=-1, keepdims=True) + EPS)
        hn_ref[...] = (y * g_ref[...]).astype(BF16)

    o_ref[...] = _dot(hn_ref[...], w_ref[...])


def _in_proj(x2, g, w, tm=1024, tn=512):
    T, D = x2.shape
    NP = w.shape[1]
    return pl.pallas_call(
        _in_proj_kernel,
        out_shape=jax.ShapeDtypeStruct((T, NP), F32),
        grid=(T // tm, NP // tn),
        in_specs=[pl.BlockSpec((tm, D), lambda i, j: (i, 0)),
                  pl.BlockSpec((1, D), lambda i, j: (0, 0)),
                  pl.BlockSpec((D, tn), lambda i, j: (0, j))],
        out_specs=pl.BlockSpec((tm, tn), lambda i, j: (i, j)),
        scratch_shapes=[pltpu.VMEM((tm, D), BF16)],
        compiler_params=pltpu.CompilerParams(dimension_semantics=("parallel", "arbitrary"),
                                             vmem_limit_bytes=VMEM_LIMIT),
        name="in_proj",
    )(x2, g, w)


def _out_proj_kernel(ma_ref, mb_ref, mc_ref, md_ref, w_ref, x_ref, g_ref, o_ref):
    m = jnp.concatenate([ma_ref[...], mb_ref[...], mc_ref[...], md_ref[...]], axis=-1)
    y = _dot(m, w_ref[...])
    yn = y * lax.rsqrt(jnp.mean(y * y, axis=-1, keepdims=True) + EPS) * g_ref[...]
    o_ref[...] = x_ref[...] + yn


def _out_proj(ms, w, x2, g, tm=512):
    T, D = x2.shape
    mspec = pl.BlockSpec((tm, W_BR), lambda i: (i, 0))
    return pl.pallas_call(
        _out_proj_kernel,
        out_shape=jax.ShapeDtypeStruct((T, D), F32),
        grid=(T // tm,),
        in_specs=[mspec, mspec, mspec, mspec,
                  pl.BlockSpec((D, D), lambda i: (0, 0)),
                  pl.BlockSpec((tm, D), lambda i: (i, 0)),
                  pl.BlockSpec((1, D), lambda i: (0, 0))],
        out_specs=pl.BlockSpec((tm, D), lambda i: (i, 0)),
        compiler_params=pltpu.CompilerParams(dimension_semantics=("parallel",),
                                             vmem_limit_bytes=VMEM_LIMIT),
        name="out_proj",
    )(*ms, w, x2, g)


def _seg_spec(name, rows, nt):
    off, wp = _DST[name]
    cb = off // wp
    return pl.BlockSpec((rows, wp), lambda b, i: (b * nt + i, cb))


def _full_spec(shape):
    nd = len(shape)
    return pl.BlockSpec(shape, lambda b, i: (0,) * nd)


def _gla_kernel(q_ref, k_ref, v_ref, fg_ref, z_ref, wfg_ref, bfg_ref, g_ref, o_ref, state_ref, acc_ref):
    R = q_ref.shape[0]
    C = GLA_CHUNK

    @pl.when(pl.program_id(1) == 0)
    def _():
        state_ref[...] = jnp.zeros(state_ref.shape, F32)

    pre = _dot(fg_ref[...].astype(BF16), wfg_ref[...]) + bfg_ref[...]
    log_f = -_softplus(-pre) * (1.0 / GLA_TAU)
    row = lax.broadcasted_iota(jnp.int32, log_f.shape, 0) % C
    bcum = log_f
    d = 1
    while d < C:
        bcum = bcum + jnp.where(row >= d, pltpu.roll(bcum, d, 0), 0.0)
        d *= 2

    q = q_ref[...] * (GLA_HEAD_K ** -0.5)
    k = k_ref[...]
    v = v_ref[...]
    lane = lax.broadcasted_iota(jnp.int32, (C, LANES), 1)
    causal = (lax.broadcasted_iota(jnp.int32, (C, C), 0) >= lax.broadcasted_iota(jnp.int32, (C, C), 1))
    head_lanes = (lane < GLA_HEAD_K, lane >= GLA_HEAD_K)
    for c in range(R // C):
        rs = slice(c * C, (c + 1) * C)
        for p in range(2):
            ls = slice(p * LANES, (p + 1) * LANES)
            b = bcum[rs, ls]
            b_last = b[C - 1:C, :]
            kk = k[rs, ls]
            q_dec = q[rs, ls] * jnp.exp(b)
            k_dec = kk * jnp.exp(-b)
            k_last = kk * jnp.exp(b_last - b)
            decay = jnp.exp(b_last)
            for hh in range(2):
                h = 2 * p + hh
                m = head_lanes[hh]
                qd = jnp.where(m, q_dec, 0.0).astype(BF16)
                kd = jnp.where(m, k_dec, 0.0).astype(BF16)
                kl = jnp.where(m, k_last, 0.0).astype(BF16)
                vh = v[rs, h * HEAD_DIM:(h + 1) * HEAD_DIM]
                attn = jnp.where(causal, _dot_t(qd, kd), 0.0)
                st = state_ref[h]
                o = _dot(attn.astype(BF16), vh.astype(BF16)) + _dot_t(qd, st.astype(BF16))
                acc_ref[rs, h * HEAD_DIM:(h + 1) * HEAD_DIM] = o
                state_ref[h] = decay * st + _dot(vh.T.astype(BF16), kl)

    o_ref[...] = _head_norm_gate(acc_ref[...], z_ref[...], g_ref[...])


def _gla(proj, wfg, bfg, g, B, S, R=256):
    nt = S // R
    T = B * S
    return pl.pallas_call(
        _gla_kernel,
        out_shape=jax.ShapeDtypeStruct((T, W_BR), BF16),
        grid=(B, nt),
        in_specs=[_seg_spec('gla_q', R, nt), _seg_spec('gla_k', R, nt), _seg_spec('gla_v', R, nt),
                  _seg_spec('gla_fg', R, nt), _seg_spec('gla_z', R, nt),
                  _full_spec(wfg.shape), _full_spec(bfg.shape), _full_spec(g.shape)],
        out_specs=pl.BlockSpec((R, W_BR), lambda b, i: (b * nt + i, 0)),
        scratch_shapes=[pltpu.VMEM((N_HEADS, HEAD_DIM, LANES), F32), pltpu.VMEM((R, W_BR), F32)],
        compiler_params=pltpu.CompilerParams(dimension_semantics=("parallel", "arbitrary"),
                                             vmem_limit_bytes=VMEM_LIMIT),
        name="gla",
    )(proj, proj, proj, proj, proj, wfg, bfg, g)


def _lru_kernel(x_ref, z_ref, cw_ref, cb_ref, wa_ref, ba_ref, wx_ref, bx_ref, lam_ref, g_ref, o_ref,
                hist_ref, h_ref):
    R = x_ref.shape[0]
    HIST = hist_ref.shape[0]

    @pl.when(pl.program_id(1) == 0)
    def _():
        hist_ref[...] = jnp.zeros(hist_ref.shape, F32)
        h_ref[...] = jnp.zeros(h_ref.shape, F32)

    x = x_ref[...]
    xe = jnp.concatenate([hist_ref[...], x], axis=0)
    xc = cb_ref[...] + cw_ref[LRU_CONV - 1:LRU_CONV, :] * x
    for kk in range(LRU_CONV - 1):
        sh = LRU_CONV - 1 - kk
        xc = xc + cw_ref[kk:kk + 1, :] * pltpu.roll(xe, sh, 0)[HIST:HIST + R]
    hist_ref[...] = x[R - HIST:R]

    xb = xc.astype(BF16)
    ra, rx = [], []
    for h in range(N_HEADS):
        xh = xb[:, h * HEAD_DIM:(h + 1) * HEAD_DIM]
        ra.append(_dot(xh, wa_ref[h]))
        rx.append(_dot(xh, wx_ref[h]))
    r = jax.nn.sigmoid(jnp.concatenate(ra, axis=-1) + ba_ref[...])
    ig = jax.nn.sigmoid(jnp.concatenate(rx, axis=-1) + bx_ref[...])
    log_a = -LRU_C * r * _softplus(-lam_ref[...])
    a = jnp.exp(log_a)
    u = jnp.sqrt(-_expm1(2.0 * log_a)) * (ig * xc)

    row = lax.broadcasted_iota(jnp.int32, a.shape, 0)
    d = 1
    while d < R:
        keep = row >= d
        u = jnp.where(keep, a * pltpu.roll(u, d, 0) + u, u)
        a = jnp.where(keep, a * pltpu.roll(a, d, 0), a)
        d *= 2
    hcur = a * h_ref[...] + u
    h_ref[...] = hcur[R - 1:R]
    o_ref[...] = _head_norm_gate(hcur, z_ref[...], g_ref[...])


def _lru(proj, cw, cb, wa, ba, wx, bx, lam, g, B, S, R=256):
    nt = S // R
    T = B * S
    return pl.pallas_call(
        _lru_kernel,
        out_shape=jax.ShapeDtypeStruct((T, W_BR), BF16),
        grid=(B, nt),
        in_specs=[_seg_spec('lru_x', R, nt), _seg_spec('lru_z', R, nt),
                  _full_spec(cw.shape), _full_spec(cb.shape), _full_spec(wa.shape), _full_spec(ba.shape),
                  _full_spec(wx.shape), _full_spec(bx.shape), _full_spec(lam.shape), _full_spec(g.shape)],
        out_specs=pl.BlockSpec((R, W_BR), lambda b, i: (b * nt + i, 0)),
        scratch_shapes=[pltpu.VMEM((8, W_BR), F32), pltpu.VMEM((1, W_BR), F32)],
        compiler_params=pltpu.CompilerParams(dimension_semantics=("parallel", "arbitrary"),
                                             vmem_limit_bytes=VMEM_LIMIT),
        name="rglru",
    )(proj, proj, cw, cb, wa, ba, wx, bx, lam, g)


def _conf_kernel(v_ref, glu_ref, z_ref, dw_ref, db_ref, lg_ref, lb_ref, pw_ref, pb_ref, g_ref, o_ref,
                 hist_ref):
    R = v_ref.shape[0]
    HIST = hist_ref.shape[0]

    @pl.when(pl.program_id(1) == 0)
    def _():
        hist_ref[...] = jnp.zeros(hist_ref.shape, F32)

    y = v_ref[...] * jax.nn.sigmoid(glu_ref[...])
    ye = jnp.concatenate([hist_ref[...], y], axis=0)
    acc = db_ref[...] + dw_ref[CONF_KERNEL - 1:CONF_KERNEL, :] * y
    for kk in range(CONF_KERNEL - 1):
        sh = CONF_KERNEL - 1 - kk
        acc = acc + dw_ref[kk:kk + 1, :] * pltpu.roll(ye, sh, 0)[HIST:HIST + R]
    hist_ref[...] = y[R - HIST:R]

    mu = jnp.mean(acc, axis=-1, keepdims=True)
    xc = acc - mu
    var = jnp.mean(xc * xc, axis=-1, keepdims=True)
    yn = xc * lax.rsqrt(var + EPS) * lg_ref[...] + lb_ref[...]
    o = _dot(_silu(yn).astype(BF16), pw_ref[...]) + pb_ref[...]
    o_ref[...] = _head_norm_gate(o, z_ref[...], g_ref[...])


def _conf(proj, dw, db, lg, lb, pw, pb, g, B, S, R=256):
    nt = S // R
    T = B * S
    return pl.pallas_call(
        _conf_kernel,
        out_shape=jax.ShapeDtypeStruct((T, W_BR), BF16),
        grid=(B, nt),
        in_specs=[_seg_spec('conv_v', R, nt), _seg_spec('conv_glu', R, nt), _seg_spec('conv_z', R, nt),
                  _full_spec(dw.shape), _full_spec(db.shape), _full_spec(lg.shape), _full_spec(lb.shape),
                  _full_spec(pw.shape), _full_spec(pb.shape), _full_spec(g.shape)],
        out_specs=pl.BlockSpec((R, W_BR), lambda b, i: (b * nt + i, 0)),
        scratch_shapes=[pltpu.VMEM((32, W_BR), F32)],
        compiler_params=pltpu.CompilerParams(dimension_semantics=("parallel", "arbitrary"),
                                             vmem_limit_bytes=VMEM_LIMIT),
        name="conformer",
    )(proj, proj, proj, dw, db, lg, lb, pw, pb, g)


def _cmp_kernel(kc_ref, vc_ref, posk_ref, w1k_ref, w2k_ref, posv_ref, w1v_ref, w2v_ref, ko_ref, vo_ref):
    S = kc_ref.shape[0]
    NB = S // CMP_STRIDE
    HALF = CMP_LEN // 2

    def compress(z_ref, pos_ref, w1_ref, w2_ref):
        lo = jnp.zeros((NB, w1_ref.shape[2]), F32)
        hi = jnp.zeros((NB, w1_ref.shape[2]), F32)
        for l in range(HALF):
            grp = z_ref[pl.ds(l, NB, stride=CMP_STRIDE), :]
            lo = lo + _dot((grp + pos_ref[l:l + 1, :]).astype(BF16), w1_ref[l])
            hi = hi + _dot((grp + pos_ref[HALF + l:HALF + l + 1, :]).astype(BF16), w1_ref[HALF + l])
        hid = lo + pltpu.roll(hi, NB - 1, 0)
        blk = lax.broadcasted_iota(jnp.int32, hid.shape, 0)
        hid = jnp.where(blk < NB - 1, hid, 0.0)
        return _dot(_silu(hid).astype(BF16), w2_ref[...])

    ko_ref[...] = compress(kc_ref, posk_ref, w1k_ref, w2k_ref)
    vo_ref[...] = compress(vc_ref, posv_ref, w1v_ref, w2v_ref)


def _nsa_compress(proj, posk, w1k, w2k, posv, w1v, w2v, B, S):
    NB = S // CMP_STRIDE

    def seg(name):
        off, wp = _DST[name]
        return pl.BlockSpec((S, wp), lambda b: (b, off // wp))

    def full(shape):
        nd = len(shape)
        return pl.BlockSpec(shape, lambda b: (0,) * nd)

    out = jax.ShapeDtypeStruct((B, NB, HEAD_DIM), F32)
    ospec = pl.BlockSpec((None, NB, HEAD_DIM), lambda b: (b, 0, 0))
    return pl.pallas_call(
        _cmp_kernel,
        out_shape=(out, out),
        grid=(B,),
        in_specs=[seg('nsa_kc'), seg('nsa_vc'), full(posk.shape), full(w1k.shape), full(w2k.shape),
                  full(posv.shape), full(w1v.shape), full(w2v.shape)],
        out_specs=(ospec, ospec),
        compiler_params=pltpu.CompilerParams(dimension_semantics=("parallel",),
                                             vmem_limit_bytes=VMEM_LIMIT),
        name="nsa_compress",
    )(proj, proj, posk, w1k, w2k, posv, w1v, w2v)


def _nsa_kernel(q_ref, gt_ref, z_ref, kc_ref, vc_ref, ks_ref, vs_ref, kw_ref, vw_ref, g_ref, o_ref):
    TQ = q_ref.shape[0]
    S = ks_ref.shape[0]
    NB = kc_ref.shape[0]
    H, Dh = N_HEADS, HEAD_DIM
    BAND = WIN + Q_BLOCK
    scale = Dh ** -0.5
    i = pl.program_id(1)
    t0 = i * TQ

    q = q_ref[...]
    q4 = jnp.concatenate([q[:, h * Dh:(h + 1) * Dh] for h in range(H)], axis=0).astype(BF16)

    t_c = t0 + lax.broadcasted_iota(jnp.int32, (TQ, NB), 0)
    n_c = lax.broadcasted_iota(jnp.int32, (TQ, NB), 1)
    cmask = t_c >= n_c * CMP_STRIDE + (CMP_LEN - 1)
    cmask4 = jnp.concatenate([cmask] * H, axis=0)
    p4 = _masked_softmax(_dot_t(q4, kc_ref[...].astype(BF16)) * scale, cmask4)
    p4b = p4.astype(BF16)
    o_cmp4 = _dot(p4b, vc_ref[...].astype(BF16))

    nn = lax.broadcasted_iota(jnp.int32, (NB, LANES), 0)
    jj = lax.broadcasted_iota(jnp.int32, (NB, LANES), 1)
    n_sel = S // SLC_LEN
    ov = ((nn * CMP_STRIDE < (jj + 1) * SLC_LEN) & (nn * CMP_STRIDE + CMP_LEN > jj * SLC_LEN)
          & (jj < n_sel) & (nn < NB - 1))
    po = _dot(p4b, jnp.where(ov, 1.0, 0.0).astype(BF16))
    imp = po[0:TQ]
    for h in range(1, H):
        imp = imp + po[h * TQ:(h + 1) * TQ]

    t_j = t0 + lax.broadcasted_iota(jnp.int32, (TQ, LANES), 0)
    j = lax.broadcasted_iota(jnp.int32, (TQ, LANES), 1)
    cur = t_j // SLC_LEN
    forced = (j == 0) | (j == cur) | (j == cur - 1)
    val = jnp.where(j > cur, -jnp.inf, jnp.where(forced, FORCED_SCORE, imp))
    rank = jnp.zeros((TQ, LANES), F32)
    for ii in range(n_sel):
        vi = val[:, ii:ii + 1]
        beats = jnp.where(vi > val, 1.0, jnp.where(vi == val, jnp.where(j > ii, 1.0, 0.0), 0.0))
        rank = rank + beats
    sel = jnp.where(rank < SLC_TOPN, 1.0, 0.0).astype(BF16)

    eb = lax.broadcasted_iota(jnp.int32, (LANES, S), 0)
    ek = lax.broadcasted_iota(jnp.int32, (LANES, S), 1)
    expand = jnp.where(ek // SLC_LEN == eb, 1.0, 0.0).astype(BF16)
    selk = _dot(sel, expand)
    t_s = t0 + lax.broadcasted_iota(jnp.int32, (TQ, S), 0)
    k_s = lax.broadcasted_iota(jnp.int32, (TQ, S), 1)
    smask = (selk > 0.5) & (k_s <= t_s)
    ksb = ks_ref[...].astype(BF16)
    vsb = vs_ref[...].astype(BF16)

    kst = pl.multiple_of(jnp.clip(t0 + TQ - BAND, 0, S - BAND), Q_BLOCK)
    kwb = kw_ref[pl.ds(kst, BAND), :].astype(BF16)
    vwb = vw_ref[pl.ds(kst, BAND), :].astype(BF16)
    t_w = t0 + lax.broadcasted_iota(jnp.int32, (TQ, BAND), 0)
    k_w = kst + lax.broadcasted_iota(jnp.int32, (TQ, BAND), 1)
    dist = t_w - k_w
    wmask = (dist >= 0) & (dist < WIN)

    gs = jax.nn.sigmoid(gt_ref[...])
    outs = []
    for h in range(H):
        qh = q4[h * TQ:(h + 1) * TQ]
        p_s = _masked_softmax(_dot_t(qh, ksb) * scale, smask)
        o_slc = _dot(p_s.astype(BF16), vsb)
        p_w = _masked_softmax(_dot_t(qh, kwb) * scale, wmask)
        o_win = _dot(p_w.astype(BF16), vwb)
        outs.append(gs[:, 3 * h:3 * h + 1] * o_cmp4[h * TQ:(h + 1) * TQ]
                    + gs[:, 3 * h + 1:3 * h + 2] * o_slc
                    + gs[:, 3 * h + 2:3 * h + 3] * o_win)
    o_ref[...] = _head_norm_gate(jnp.concatenate(outs, axis=-1), z_ref[...], g_ref[...])


def _nsa(proj, k_cmp, v_cmp, g, B, S, TQ=Q_BLOCK):
    nt = S // TQ
    T = B * S
    NB = k_cmp.shape[1]

    def seq(name):
        off, wp = _DST[name]
        return pl.BlockSpec((S, wp), lambda b, i: (b, off // wp))

    cspec = pl.BlockSpec((None, NB, HEAD_DIM), lambda b, i: (b, 0, 0))
    return pl.pallas_call(
        _nsa_kernel,
        out_shape=jax.ShapeDtypeStruct((T, W_BR), BF16),
        grid=(B, nt),
        in_specs=[_seg_spec('nsa_q', TQ, nt), _seg_spec('nsa_g', TQ, nt), _seg_spec('nsa_z', TQ, nt),
                  cspec, cspec, seq('nsa_ks'), seq('nsa_vs'), seq('nsa_kw'), seq('nsa_vw'),
                  _full_spec(g.shape)],
        out_specs=pl.BlockSpec((TQ, W_BR), lambda b, i: (b * nt + i, 0)),
        compiler_params=pltpu.CompilerParams(dimension_semantics=("parallel", "arbitrary"),
                                             vmem_limit_bytes=VMEM_LIMIT),
        name="nsa_attention",
    )(proj, proj, proj, k_cmp, v_cmp, proj, proj, proj, proj, g)


def kernel(x, pre_norm_g, w_in, gla_w_fg2, gla_b_fg2, lru_conv_w, lru_conv_b, lru_w_a, lru_b_a, lru_w_x,
           lru_b_x, lru_lambda, nsa_cmp_pos_k, nsa_cmp_w1_k, nsa_cmp_w2_k, nsa_cmp_pos_v, nsa_cmp_w1_v,
           nsa_cmp_w2_v, conf_dw_w, conf_dw_b, conf_ln_g, conf_ln_b, conf_pw_w, conf_pw_b, branch_norm_g,
           w_out, post_norm_g):
    B, S, D = x.shape
    L = w_in.shape[0]
    T = B * S
    x2 = x.reshape(T, D)

    w_in_p = _relayout_w_in(w_in)
    w_out_b = w_out.astype(BF16)
    lowrank = gla_w_fg2.shape[1]
    wfg = jnp.pad(gla_w_fg2, ((0, 0), (0, LANES - lowrank), (0, 0))).astype(BF16)
    wa = lru_w_a.astype(BF16)
    wx = lru_w_x.astype(BF16)
    w1k = nsa_cmp_w1_k.reshape(L, CMP_LEN, HEAD_DIM, -1).astype(BF16)
    w1v = nsa_cmp_w1_v.reshape(L, CMP_LEN, HEAD_DIM, -1).astype(BF16)
    w2k = nsa_cmp_w2_k.astype(BF16)
    w2v = nsa_cmp_w2_v.astype(BF16)
    pw = conf_pw_w.astype(BF16)

    def row(a):
        return a.reshape(1, -1)

    for l in range(L):
        bg = branch_norm_g[l]
        proj = _in_proj(x2, row(pre_norm_g[l]), w_in_p[l])
        m_a = _gla(proj, wfg[l], row(gla_b_fg2[l]), row(bg[0 * W_BR:1 * W_BR]), B, S)
        m_b = _lru(proj, lru_conv_w[l], row(lru_conv_b[l]), wa[l], row(lru_b_a[l]), wx[l], row(lru_b_x[l]),
                   row(lru_lambda[l]), row(bg[1 * W_BR:2 * W_BR]), B, S)
        k_cmp, v_cmp = _nsa_compress(proj, nsa_cmp_pos_k[l], w1k[l], w2k[l], nsa_cmp_pos_v[l], w1v[l],
                                     w2v[l], B, S)
        m_c = _nsa(proj, k_cmp, v_cmp, row(bg[2 * W_BR:3 * W_BR]), B, S)
        m_d = _conf(proj, conf_dw_w[l], row(conf_dw_b[l]), row(conf_ln_g[l]), row(conf_ln_b[l]), pw[l],
                    row(conf_pw_b[l]), row(bg[3 * W_BR:4 * W_BR]), B, S)
        x2 = _out_proj((m_a, m_b, m_c, m_d), w_out_b[l], x2, row(post_norm_g[l]))
    return x2.reshape(B, S, D)
```

```python
import functools

import jax
import jax.numpy as jnp
from jax import lax
from jax.experimental import pallas as pl
from jax.experimental.pallas import tpu as pltpu

F32 = jnp.float32
BF16 = jnp.bfloat16

D_MODEL = 2048
N_HEADS = 4
HEAD_DIM = 128
W_BR = N_HEADS * HEAD_DIM
GLA_HEAD_K = 64
GLA_TAU = 16.0
GLA_CHUNK = 64
LRU_C = 8.0
LRU_CONV = 4
CMP_LEN = 32
CMP_STRIDE = 16
SLC_LEN = 64
SLC_TOPN = 16
WIN = 512
Q_BLOCK = 128
FORCED_SCORE = 1e3
CONF_KERNEL = 31
EPS = 1e-6
LOG2_E = 1.4426950408889634
LANES = 128
VMEM_LIMIT = 48 * 1024 * 1024

_SEGS = (('gla_q', 256), ('gla_k', 256), ('gla_v', 512), ('gla_fg', 16), ('gla_z', 512),
         ('lru_x', 512), ('lru_z', 512),
         ('nsa_q', 512), ('nsa_kc', 128), ('nsa_vc', 128), ('nsa_ks', 128), ('nsa_vs', 128),
         ('nsa_kw', 128), ('nsa_vw', 128), ('nsa_g', 12), ('nsa_z', 512),
         ('conv_v', 512), ('conv_glu', 512), ('conv_z', 512))
_ORDER = ('gla_v', 'gla_z', 'lru_x', 'lru_z', 'nsa_q', 'nsa_z', 'conv_v', 'conv_glu', 'conv_z',
          'gla_q', 'gla_k', 'gla_fg', 'nsa_kc', 'nsa_vc', 'nsa_ks', 'nsa_vs', 'nsa_kw', 'nsa_vw',
          'nsa_g')


def _layout():
    src, off = {}, 0
    for name, w in _SEGS:
        src[name] = (off, w)
        off += w
    dst, off = {}, 0
    for name in _ORDER:
        w = src[name][1]
        wp = -(-w // LANES) * LANES
        assert off % wp == 0
        dst[name] = (off, wp)
        off += wp
    return src, dst, off


_SRC, _DST, D_PROJ_PAD = _layout()


def _relayout_w_in(w_in):
    parts = []
    for name in _ORDER:
        o, w = _SRC[name]
        wp = _DST[name][1]
        p = w_in[..., o:o + w]
        if wp != w:
            p = jnp.pad(p, ((0, 0), (0, 0), (0, wp - w)))
        parts.append(p)
    return jnp.concatenate(parts, axis=-1).astype(BF16)


def _dot(a, b):
    return jnp.dot(a, b, preferred_element_type=F32)


def _dot_t(a, b):
    return lax.dot_general(a, b, (((1,), (1,)), ((), ())), preferred_element_type=F32)


def _softplus(x):
    return jnp.maximum(x, 0.0) + jnp.log1p(jnp.exp(-jnp.abs(x)))


def _expm1(x):
    return jnp.tanh(0.5 * x) * (jnp.exp(x) + 1.0)


def _silu(x):
    return x * jax.nn.sigmoid(x)


def _masked_softmax(s, mask):
    s = jnp.where(mask, s, -jnp.inf)
    m = jnp.max(s, axis=-1, keepdims=True)
    m = jnp.where(jnp.isfinite(m), m, 0.0)
    p = jnp.exp(s - m)
    return p / jnp.maximum(jnp.sum(p, axis=-1, keepdims=True), 1e-30)


def _head_norm_gate(o, z, g):
    outs = []
    for h in range(N_HEADS):
        oh = o[:, h * HEAD_DIM:(h + 1) * HEAD_DIM]
        outs.append(oh * lax.rsqrt(jnp.mean(oh * oh, axis=-1, keepdims=True) + EPS))
    on = jnp.concatenate(outs, axis=-1) * g
    return (on * _silu(z)).astype(BF16)


def _in_proj_kernel(x_ref, g_ref, w_ref, o_ref, hn_ref):
    @pl.when(pl.program_id(1) == 0)
    def _():
        x = x_ref[...]
        y = x * lax.rsqrt(jnp.mean(x * x, axis=-1, keepdims=True) + EPS)
        hn_ref[...] = (y * g_ref[...]).astype(BF16)

    o_ref[...] = _dot(hn_ref[...], w_ref[...])


def _in_proj(x2, g, w, tm=1024, tn=512):
    T, D = x2.shape
    NP = w.shape[1]
    return pl.pallas_call(
        _in_proj_kernel,
        out_shape=jax.ShapeDtypeStruct((T, NP), F32),
        grid=(T // tm, NP // tn),
        in_specs=[pl.BlockSpec((tm, D), lambda i, j: (i, 0)),
                  pl.BlockSpec((1, D), lambda i, j: (0, 0)),
                  pl.BlockSpec((D, tn), lambda i, j: (0, j))],
        out_specs=pl.BlockSpec((tm, tn), lambda i, j: (i, j)),
        scratch_shapes=[pltpu.VMEM((tm, D), BF16)],
        compiler_params=pltpu.CompilerParams(dimension_semantics=("parallel", "arbitrary"),
                                             vmem_limit_bytes=VMEM_LIMIT),
        name="in_proj",
    )(x2, g, w)


def _out_proj_kernel(ma_ref, mb_ref, mc_ref, md_ref, w_ref, x_ref, g_ref, o_ref):
    m = jnp.concatenate([ma_ref[...], mb_ref[...], mc_ref[...], md_ref[...]], axis=-1)
    y = _dot(m, w_ref[...])
    yn = y * lax.rsqrt(jnp.mean(y * y, axis=-1, keepdims=True) + EPS) * g_ref[...]
    o_ref[...] = x_ref[...] + yn


def _out_proj(ms, w, x2, g, tm=512):
    T, D = x2.shape
    mspec = pl.BlockSpec((tm, W_BR), lambda i: (i, 0))
    return pl.pallas_call(
        _out_proj_kernel,
        out_shape=jax.ShapeDtypeStruct((T, D), F32),
        grid=(T // tm,),
        in_specs=[mspec, mspec, mspec, mspec,
                  pl.BlockSpec((D, D), lambda i: (0, 0)),
                  pl.BlockSpec((tm, D), lambda i: (i, 0)),
                  pl.BlockSpec((1, D), lambda i: (0, 0))],
        out_specs=pl.BlockSpec((tm, D), lambda i: (i, 0)),
        compiler_params=pltpu.CompilerParams(dimension_semantics=("parallel",),
                                             vmem_limit_bytes=VMEM_LIMIT),
        name="out_proj",
    )(*ms, w, x2, g)


def _seg_spec(name, rows, nt):
    off, wp = _DST[name]
    cb = off // wp
    return pl.BlockSpec((rows, wp), lambda b, i: (b * nt + i, cb))


def _full_spec(shape):
    nd = len(shape)
    return pl.BlockSpec(shape, lambda b, i: (0,) * nd)


def _gla_kernel(q_ref, k_ref, v_ref, fg_ref, z_ref, wfg_ref, bfg_ref, g_ref, o_ref, state_ref, acc_ref):
    R = q_ref.shape[0]
    C = GLA_CHUNK

    @pl.when(pl.program_id(1) == 0)
    def _():
        state_ref[...] = jnp.zeros(state_ref.shape, F32)

    pre = _dot(fg_ref[...].astype(BF16), wfg_ref[...]) + bfg_ref[...]
    log_f = -_softplus(-pre) * (1.0 / GLA_TAU)
    row = lax.broadcasted_iota(jnp.int32, log_f.shape, 0) % C
    bcum = log_f
    d = 1
    while d < C:
        bcum = bcum + jnp.where(row >= d, pltpu.roll(bcum, d, 0), 0.0)
        d *= 2

    q = q_ref[...] * (GLA_HEAD_K ** -0.5)
    k = k_ref[...]
    v = v_ref[...]
    lane = lax.broadcasted_iota(jnp.int32, (C, LANES), 1)
    causal = (lax.broadcasted_iota(jnp.int32, (C, C), 0) >= lax.broadcasted_iota(jnp.int32, (C, C), 1))
    head_lanes = (lane < GLA_HEAD_K, lane >= GLA_HEAD_K)
    for c in range(R // C):
        rs = slice(c * C, (c + 1) * C)
        for p in range(2):
            ls = slice(p * LANES, (p + 1) * LANES)
            b = bcum[rs, ls]
            b_last = b[C - 1:C, :]
            kk = k[rs, ls]
            q_dec = q[rs, ls] * jnp.exp(b)
            k_dec = kk * jnp.exp(-b)
            k_last = kk * jnp.exp(b_last - b)
            decay = jnp.exp(b_last)
            for hh in range(2):
                h = 2 * p + hh
                m = head_lanes[hh]
                qd = jnp.where(m, q_dec, 0.0).astype(BF16)
                kd = jnp.where(m, k_dec, 0.0).astype(BF16)
                kl = jnp.where(m, k_last, 0.0).astype(BF16)
                vh = v[rs, h * HEAD_DIM:(h + 1) * HEAD_DIM]
                attn = jnp.where(causal, _dot_t(qd, kd), 0.0)
                st = state_ref[h]
                o = _dot(attn.astype(BF16), vh.astype(BF16)) + _dot_t(qd, st.astype(BF16))
                acc_ref[rs, h * HEAD_DIM:(h + 1) * HEAD_DIM] = o
                state_ref[h] = decay * st + _dot(vh.T.astype(BF16), kl)

    o_ref[...] = _head_norm_gate(acc_ref[...], z_ref[...], g_ref[...])


def _gla(proj, wfg, bfg, g, B, S, R=256):
    nt = S // R
    T = B * S
    return pl.pallas_call(
        _gla_kernel,
        out_shape=jax.ShapeDtypeStruct((T, W_BR), BF16),
        grid=(B, nt),
        in_specs=[_seg_spec('gla_q', R, nt), _seg_spec('gla_k', R, nt), _seg_spec('gla_v', R, nt),
                  _seg_spec('gla_fg', R, nt), _seg_spec('gla_z', R, nt),
                  _full_spec(wfg.shape), _full_spec(bfg.shape), _full_spec(g.shape)],
        out_specs=pl.BlockSpec((R, W_BR), lambda b, i: (b * nt + i, 0)),
        scratch_shapes=[pltpu.VMEM((N_HEADS, HEAD_DIM, LANES), F32), pltpu.VMEM((R, W_BR), F32)],
        compiler_params=pltpu.CompilerParams(dimension_semantics=("parallel", "arbitrary"),
                                             vmem_limit_bytes=VMEM_LIMIT),
        name="gla",
    )(proj, proj, proj, proj, proj, wfg, bfg, g)


def _lru_kernel(x_ref, z_ref, cw_ref, cb_ref, wa_ref, ba_ref, wx_ref, bx_ref, lam_ref, g_ref, o_ref,
                hist_ref, h_ref):
    R = x_ref.shape[0]
    HIST = hist_ref.shape[0]

    @pl.when(pl.program_id(1) == 0)
    def _():
        hist_ref[...] = jnp.zeros(hist_ref.shape, F32)
        h_ref[...] = jnp.zeros(h_ref.shape, F32)

    x = x_ref[...]
    xe = jnp.concatenate([hist_ref[...], x], axis=0)
    xc = cb_ref[...] + cw_ref[LRU_CONV - 1:LRU_CONV, :] * x
    for kk in range(LRU_CONV - 1):
        sh = LRU_CONV - 1 - kk
        xc = xc + cw_ref[kk:kk + 1, :] * pltpu.roll(xe, sh, 0)[HIST:HIST + R]
    hist_ref[...] = x[R - HIST:R]

    xb = xc.astype(BF16)
    ra, rx = [], []
    for h in range(N_HEADS):
        xh = xb[:, h * HEAD_DIM:(h + 1) * HEAD_DIM]
        ra.append(_dot(xh, wa_ref[h]))
        rx.append(_dot(xh, wx_ref[h]))
    r = jax.nn.sigmoid(jnp.concatenate(ra, axis=-1) + ba_ref[...])
    ig = jax.nn.sigmoid(jnp.concatenate(rx, axis=-1) + bx_ref[...])
    log_a = -LRU_C * r * _softplus(-lam_ref[...])
    a = jnp.exp(log_a)
    u = jnp.sqrt(-_expm1(2.0 * log_a)) * (ig * xc)

    row = lax.broadcasted_iota(jnp.int32, a.shape, 0)
    d = 1
    while d < R:
        keep = row >= d
        u = jnp.where(keep, a * pltpu.roll(u, d, 0) + u, u)
        a = jnp.where(keep, a * pltpu.roll(a, d, 0), a)
        d *= 2
    hcur = a * h_ref[...] + u
    h_ref[...] = hcur[R - 1:R]
    o_ref[...] = _head_norm_gate(hcur, z_ref[...], g_ref[...])


def _lru(proj, cw, cb, wa, ba, wx, bx, lam, g, B, S, R=256):
    nt = S // R
    T = B * S
    return pl.pallas_call(
        _lru_kernel,
        out_shape=jax.ShapeDtypeStruct((T, W_BR), BF16),
        grid=(B, nt),
        in_specs=[_seg_spec('lru_x', R, nt), _seg_spec('lru_z', R, nt),
                  _full_spec(cw.shape), _full_spec(cb.shape), _full_spec(wa.shape), _full_spec(ba.shape),
                  _full_spec(wx.shape), _full_spec(bx.shape), _full_spec(lam.shape), _full_spec(g.shape)],
        out_specs=pl.BlockSpec((R, W_BR), lambda b, i: (b * nt + i, 0)),
        scratch_shapes=[pltpu.VMEM((8, W_BR), F32), pltpu.VMEM((1, W_BR), F32)],
        compiler_params=pltpu.CompilerParams(dimension_semantics=("parallel", "arbitrary"),
                                             vmem_limit_bytes=VMEM_LIMIT),
        name="rglru",
    )(proj, proj, cw, cb, wa, ba, wx, bx, lam, g)


def _conf_kernel(v_ref, glu_ref, z_ref, dw_ref, db_ref, lg_ref, lb_ref, pw_ref, pb_ref, g_ref, o_ref,
                 hist_ref):
    R = v_ref.shape[0]
    HIST = hist_ref.shape[0]

    @pl.when(pl.program_id(1) == 0)
    def _():
        hist_ref[...] = jnp.zeros(hist_ref.shape, F32)

    y = v_ref[...] * jax.nn.sigmoid(glu_ref[...])
    ye = jnp.concatenate([hist_ref[...], y], axis=0)
    acc = db_ref[...] + dw_ref[CONF_KERNEL - 1:CONF_KERNEL, :] * y
    for kk in range(CONF_KERNEL - 1):
        sh = CONF_KERNEL - 1 - kk
        acc = acc + dw_ref[kk:kk + 1, :] * pltpu.roll(ye, sh, 0)[HIST:HIST + R]
    hist_ref[...] = y[R - HIST:R]

    mu = jnp.mean(acc, axis=-1, keepdims=True)
    xc = acc - mu
    var = jnp.mean(xc * xc, axis=-1, keepdims=True)
    yn = xc * lax.rsqrt(var + EPS) * lg_ref[...] + lb_ref[...]
    o = _dot(_silu(yn).astype(BF16), pw_ref[...]) + pb_ref[...]
    o_ref[...] = _head_norm_gate(o, z_ref[...], g_ref[...])


def _conf(proj, dw, db, lg, lb, pw, pb, g, B, S, R=256):
    nt = S // R
    T = B * S
    return pl.pallas_call(
        _conf_kernel,
        out_shape=jax.ShapeDtypeStruct((T, W_BR), BF16),
        grid=(B, nt),
        in_specs=[_seg_spec('conv_v', R, nt), _seg_spec('conv_glu', R, nt), _seg_spec('conv_z', R, nt),
                  _full_spec(dw.shape), _full_spec(db.shape), _full_spec(lg.shape), _full_spec(lb.shape),
                  _full_spec(pw.shape), _full_spec(pb.shape), _full_spec(g.shape)],
        out_specs=pl.BlockSpec((R, W_BR), lambda b, i: (b * nt + i, 0)),
        scratch_shapes=[pltpu.VMEM((32, W_BR), F32)],
        compiler_params=pltpu.CompilerParams(dimension_semantics=("parallel", "arbitrary"),
                                             vmem_limit_bytes=VMEM_LIMIT),
        name="conformer",
    )(proj, proj, proj, dw, db, lg, lb, pw, pb, g)


def _cmp_kernel(kc_ref, vc_ref, posk_ref, w1k_ref, w2k_ref, posv_ref, w1v_ref, w2v_ref, ko_ref, vo_ref):
    S = kc_ref.shape[0]
    NB = S // CMP_STRIDE
    HALF = CMP_LEN // 2

    def compress(z_ref, pos_ref, w1_ref, w2_ref):
        lo = jnp.zeros((NB, w1_ref.shape[2]), F32)
        hi = jnp.zeros((NB, w1_ref.shape[2]), F32)
        for l in range(HALF):
            grp = z_ref[pl.ds(l, NB, stride=CMP_STRIDE), :]
            lo = lo + _dot((grp + pos_ref[l:l + 1, :]).astype(BF16), w1_ref[l])
            hi = hi + _dot((grp + pos_ref[HALF + l:HALF + l + 1, :]).astype(BF16), w1_ref[HALF + l])
        hid = lo + pltpu.roll(hi, NB - 1, 0)
        blk = lax.broadcasted_iota(jnp.int32, hid.shape, 0)
        hid = jnp.where(blk < NB - 1, hid, 0.0)
        return _dot(_silu(hid).astype(BF16), w2_ref[...])

    ko_ref[...] = compress(kc_ref, posk_ref, w1k_ref, w2k_ref)
    vo_ref[...] = compress(vc_ref, posv_ref, w1v_ref, w2v_ref)


def _nsa_compress(proj, posk, w1k, w2k, posv, w1v, w2v, B, S):
    NB = S // CMP_STRIDE

    def seg(name):
        off, wp = _DST[name]
        return pl.BlockSpec((S, wp), lambda b: (b, off // wp))

    def full(shape):
        nd = len(shape)
        return pl.BlockSpec(shape, lambda b: (0,) * nd)

    out = jax.ShapeDtypeStruct((B, NB, HEAD_DIM), F32)
    ospec = pl.BlockSpec((None, NB, HEAD_DIM), lambda b: (b, 0, 0))
    return pl.pallas_call(
        _cmp_kernel,
        out_shape=(out, out),
        grid=(B,),
        in_specs=[seg('nsa_kc'), seg('nsa_vc'), full(posk.shape), full(w1k.shape), full(w2k.shape),
                  full(posv.shape), full(w1v.shape), full(w2v.shape)],
        out_specs=(ospec, ospec),
        compiler_params=pltpu.CompilerParams(dimension_semantics=("parallel",),
                                             vmem_limit_bytes=VMEM_LIMIT),
        name="nsa_compress",
    )(proj, proj, posk, w1k, w2k, posv, w1v, w2v)


def _nsa_kernel(q_ref, gt_ref, z_ref, kc_ref, vc_ref, ks_ref, vs_ref, kw_ref, vw_ref, g_ref, o_ref,
                m_ref, acc_ref):
    TQ = q_ref.shape[0]
    S = ks_ref.shape[0]
    NB = kc_ref.shape[0]
    H, Dh = N_HEADS, HEAD_DIM
    scale = Dh ** -0.5
    i = pl.program_id(1)
    t0 = i * TQ

    q = q_ref[...]
    q4 = jnp.concatenate([q[:, h * Dh:(h + 1) * Dh] for h in range(H)], axis=0).astype(BF16)

    t_c = t0 + lax.broadcasted_iota(jnp.int32, (TQ, NB), 0)
    n_c = lax.broadcasted_iota(jnp.int32, (TQ, NB), 1)
    cmask = t_c >= n_c * CMP_STRIDE + (CMP_LEN - 1)
    cmask4 = jnp.concatenate([cmask] * H, axis=0)
    p4 = _masked_softmax(_dot_t(q4, kc_ref[...].astype(BF16)) * scale, cmask4)
    p4b = p4.astype(BF16)
    o_cmp4 = _dot(p4b, vc_ref[...].astype(BF16))

    n_sel = S // SLC_LEN
    jj = lax.broadcasted_iota(jnp.int32, (n_sel, NB), 0)
    nn = lax.broadcasted_iota(jnp.int32, (n_sel, NB), 1)
    ov = ((nn * CMP_STRIDE < (jj + 1) * SLC_LEN) & (nn * CMP_STRIDE + CMP_LEN > jj * SLC_LEN)
          & (nn < NB - 1))
    po = _dot_t(jnp.where(ov, 1.0, 0.0).astype(BF16), p4b)
    imp = po[:, 0:TQ]
    for h in range(1, H):
        imp = imp + po[:, h * TQ:(h + 1) * TQ]

    j = lax.broadcasted_iota(jnp.int32, (n_sel, TQ), 0)
    cur = (t0 + lax.broadcasted_iota(jnp.int32, (n_sel, TQ), 1)) // SLC_LEN
    forced = (j == 0) | (j == cur) | (j == cur - 1)
    val = jnp.where(j > cur, -jnp.inf, jnp.where(forced, FORCED_SCORE, imp))
    rank = jnp.zeros((n_sel, TQ), F32)
    for ii in range(n_sel):
        vi = val[ii:ii + 1, :]
        beats = jnp.where(vi > val, 1.0, jnp.where(vi == val, jnp.where(j > ii, 1.0, 0.0), 0.0))
        rank = rank + beats
    sel_t = jnp.where(rank < SLC_TOPN, 1.0, 0.0)
    sel = jnp.concatenate([sel_t, jnp.zeros((LANES - n_sel, TQ), F32)], axis=0).T.astype(BF16)

    t_k = t0 + lax.broadcasted_iota(jnp.int32, (TQ, TQ), 0)
    k_i = lax.broadcasted_iota(jnp.int32, (TQ, TQ), 1)
    ones_half = jnp.ones((TQ, Dh), BF16)
    neg_inf = jnp.full((TQ, TQ), -jnp.inf, F32)
    scale_log2e = scale * LOG2_E

    def reset():
        m_ref[...] = jnp.full(m_ref.shape, -jnp.inf, F32)
        acc_ref[...] = jnp.zeros(acc_ref.shape, F32)

    def attend_chunk(k_ref, v_ref, c, bias):
        k0 = pl.multiple_of(c * TQ, TQ)
        kb = k_ref[pl.ds(k0, TQ), :].astype(BF16)
        vb = jnp.concatenate([v_ref[pl.ds(k0, TQ), :].astype(BF16), ones_half], axis=-1)
        s4 = _dot_t(q4, kb)
        for h in range(H):
            rs = slice(h * TQ, (h + 1) * TQ)
            s = s4[rs] if bias is None else s4[rs] + bias
            m_old = m_ref[rs]
            m_new = jnp.maximum(m_old, jnp.max(s, axis=-1, keepdims=True))
            m_safe = jnp.where(m_new == -jnp.inf, 0.0, m_new)
            p = jnp.exp2((s - jnp.concatenate([m_safe] * (TQ // LANES), axis=-1)) * scale_log2e)
            alpha = jnp.exp2((m_old - m_safe) * scale_log2e)
            acc_ref[rs] = jnp.concatenate([alpha, alpha], axis=-1) * acc_ref[rs] + _dot(p.astype(BF16), vb)
            m_ref[rs] = m_new

    def finish():
        acc = acc_ref[...]
        return acc[:, :Dh] / jnp.maximum(acc[:, Dh:], 1e-30)

    eb = lax.broadcasted_iota(jnp.int32, (LANES, TQ), 0)
    ek = lax.broadcasted_iota(jnp.int32, (LANES, TQ), 1) // SLC_LEN

    def sel_keys(c):
        expand = jnp.where(eb == ek + c * (TQ // SLC_LEN), 1.0, 0.0).astype(BF16)
        return _dot(sel, expand) > 0.5

    reset()

    def slc_body(c, carry):
        attend_chunk(ks_ref, vs_ref, c, jnp.where(sel_keys(c), 0.0, neg_inf))
        return carry

    lax.fori_loop(0, i, slc_body, 0)
    attend_chunk(ks_ref, vs_ref, i, jnp.where(sel_keys(i) & (t0 + k_i <= t_k), 0.0, neg_inf))
    o_slc4 = finish()

    reset()
    n_back = WIN // TQ
    for back in range(n_back, -1, -1):
        def win_chunk(back=back):
            c = i - back
            if 0 < back < n_back:
                bias = None
            else:
                dist = t_k - (c * TQ + k_i)
                bias = jnp.where((dist >= 0) & (dist < WIN), 0.0, neg_inf)
            attend_chunk(kw_ref, vw_ref, c, bias)
        if back == 0:
            win_chunk()
        else:
            pl.when(i >= back)(win_chunk)
    o_win4 = finish()

    gs = jax.nn.sigmoid(gt_ref[...])
    outs = []
    for h in range(H):
        rs = slice(h * TQ, (h + 1) * TQ)
        outs.append(gs[:, 3 * h:3 * h + 1] * o_cmp4[rs] + gs[:, 3 * h + 1:3 * h + 2] * o_slc4[rs]
                    + gs[:, 3 * h + 2:3 * h + 3] * o_win4[rs])
    o_ref[...] = _head_norm_gate(jnp.concatenate(outs, axis=-1), z_ref[...], g_ref[...])


def _nsa(proj, k_cmp, v_cmp, g, B, S, TQ=256):
    assert WIN % TQ == 0 and TQ % SLC_LEN == 0 and TQ % LANES == 0
    nt = S // TQ
    T = B * S
    NB = k_cmp.shape[1]

    def seq(name):
        off, wp = _DST[name]
        return pl.BlockSpec((S, wp), lambda b, i: (b, off // wp))

    cspec = pl.BlockSpec((None, NB, HEAD_DIM), lambda b, i: (b, 0, 0))
    return pl.pallas_call(
        _nsa_kernel,
        out_shape=jax.ShapeDtypeStruct((T, W_BR), BF16),
        grid=(B, nt),
        in_specs=[_seg_spec('nsa_q', TQ, nt), _seg_spec('nsa_g', TQ, nt), _seg_spec('nsa_z', TQ, nt),
                  cspec, cspec, seq('nsa_ks'), seq('nsa_vs'), seq('nsa_kw'), seq('nsa_vw'),
                  _full_spec(g.shape)],
        out_specs=pl.BlockSpec((TQ, W_BR), lambda b, i: (b * nt + i, 0)),
        scratch_shapes=[pltpu.VMEM((N_HEADS * TQ, LANES), F32),
                        pltpu.VMEM((N_HEADS * TQ, 2 * HEAD_DIM), F32)],
        compiler_params=pltpu.CompilerParams(dimension_semantics=("parallel", "arbitrary"),
                                             vmem_limit_bytes=VMEM_LIMIT),
        name="nsa_attention",
    )(proj, proj, proj, k_cmp, v_cmp, proj, proj, proj, proj, g)


def kernel(x, pre_norm_g, w_in, gla_w_fg2, gla_b_fg2, lru_conv_w, lru_conv_b, lru_w_a, lru_b_a, lru_w_x,
           lru_b_x, lru_lambda, nsa_cmp_pos_k, nsa_cmp_w1_k, nsa_cmp_w2_k, nsa_cmp_pos_v, nsa_cmp_w1_v,
           nsa_cmp_w2_v, conf_dw_w, conf_dw_b, conf_ln_g, conf_ln_b, conf_pw_w, conf_pw_b, branch_norm_g,
           w_out, post_norm_g):
    B, S, D = x.shape
    L = w_in.shape[0]
    T = B * S
    x2 = x.reshape(T, D)

    w_in_p = _relayout_w_in(w_in)
    w_out_b = w_out.astype(BF16)
    lowrank = gla_w_fg2.shape[1]
    wfg = jnp.pad(gla_w_fg2, ((0, 0), (0, LANES - lowrank), (0, 0))).astype(BF16)
    wa = lru_w_a.astype(BF16)
    wx = lru_w_x.astype(BF16)
    w1k = nsa_cmp_w1_k.reshape(L, CMP_LEN, HEAD_DIM, -1).astype(BF16)
    w1v = nsa_cmp_w1_v.reshape(L, CMP_LEN, HEAD_DIM, -1).astype(BF16)
    w2k = nsa_cmp_w2_k.astype(BF16)
    w2v = nsa_cmp_w2_v.astype(BF16)
    pw = conf_pw_w.astype(BF16)

    def row(a):
        return a.reshape(1, -1)

    for l in range(L):
        bg = branch_norm_g[l]
        proj = _in_proj(x2, row(pre_norm_g[l]), w_in_p[l])
        m_a = _gla(proj, wfg[l], row(gla_b_fg2[l]), row(bg[0 * W_BR:1 * W_BR]), B, S)
        m_b = _lru(proj, lru_conv_w[l], row(lru_conv_b[l]), wa[l], row(lru_b_a[l]), wx[l], row(lru_b_x[l]),
                   row(lru_lambda[l]), row(bg[1 * W_BR:2 * W_BR]), B, S)
        k_cmp, v_cmp = _nsa_compress(proj, nsa_cmp_pos_k[l], w1k[l], w2k[l], nsa_cmp_pos_v[l], w1v[l],
                                     w2v[l], B, S)
        m_c = _nsa(proj, k_cmp, v_cmp, row(bg[2 * W_BR:3 * W_BR]), B, S)
        m_d = _conf(proj, conf_dw_w[l], row(conf_dw_b[l]), row(conf_ln_g[l]), row(conf_ln_b[l]), pw[l],
                    row(conf_pw_b[l]), row(bg[3 * W_BR:4 * W_BR]), B, S)
        x2 = _out_proj((m_a, m_b, m_c, m_d), w_out_b[l], x2, row(post_norm_g[l]))
    return x2.reshape(B, S, D)
```

```python
import jax
import jax.numpy as jnp
from jax import lax
from jax.experimental import pallas as pl
from jax.experimental.pallas import tpu as pltpu

F32 = jnp.float32
BF16 = jnp.bfloat16

D_MODEL = 2048
N_HEADS = 4
HEAD_DIM = 128
W_BR = N_HEADS * HEAD_DIM
N_MIXERS = 4
GLA_HEAD_K = 64
GLA_TAU = 16.0
GLA_CHUNK = 64
LRU_C = 8.0
LRU_CONV = 4
CMP_LEN = 32
CMP_STRIDE = 16
SLC_LEN = 64
SLC_TOPN = 16
WIN = 512
FORCED_SCORE = 1e3
CONF_KERNEL = 31
EPS = 1e-6
LOG2_E = 1.4426950408889634
SUBLANES = 8
LANES = 128
VMEM_LIMIT = 48 * 1024 * 1024

_SEGS = (('gla_q', 256), ('gla_k', 256), ('gla_v', 512), ('gla_fg', 16), ('gla_z', 512),
         ('lru_x', 512), ('lru_z', 512),
         ('nsa_q', 512), ('nsa_kc', 128), ('nsa_vc', 128), ('nsa_ks', 128), ('nsa_vs', 128),
         ('nsa_kw', 128), ('nsa_vw', 128), ('nsa_g', 12), ('nsa_z', 512),
         ('conv_v', 512), ('conv_glu', 512), ('conv_z', 512))
_ORDER = ('gla_v', 'gla_z', 'lru_x', 'lru_z', 'nsa_q', 'nsa_z', 'conv_v', 'conv_glu', 'conv_z',
          'gla_q', 'gla_k', 'gla_fg', 'nsa_kc', 'nsa_vc', 'nsa_ks', 'nsa_vs', 'nsa_kw', 'nsa_vw',
          'nsa_g')


def _layout():
    src, off = {}, 0
    for name, w in _SEGS:
        src[name] = (off, w)
        off += w
    dst, off = {}, 0
    for name in _ORDER:
        w = src[name][1]
        wp = -(-w // LANES) * LANES
        assert off % wp == 0
        dst[name] = (off, wp)
        off += wp
    return src, dst, off


_SRC, _DST, D_PROJ_PAD = _layout()


def _relayout_kernel(w_ref, o_ref):
    w = w_ref[...]
    for name in _ORDER:
        o, n = _SRC[name]
        d, npad = _DST[name]
        o_ref[:, d:d + n] = w[:, o:o + n].astype(BF16)
        if npad != n:
            o_ref[:, d + n:d + npad] = jnp.zeros((w.shape[0], npad - n), BF16)


def _relayout_w_in(w_in, tr=256):
    L, D, NP = w_in.shape
    return pl.pallas_call(
        _relayout_kernel,
        out_shape=jax.ShapeDtypeStruct((L, D, D_PROJ_PAD), BF16),
        grid=(L, D // tr),
        in_specs=[pl.BlockSpec((None, tr, NP), lambda l, i: (l, i, 0))],
        out_specs=pl.BlockSpec((None, tr, D_PROJ_PAD), lambda l, i: (l, i, 0)),
        compiler_params=pltpu.CompilerParams(dimension_semantics=("parallel", "parallel"),
                                             vmem_limit_bytes=VMEM_LIMIT),
        name="relayout_w_in",
    )(w_in)


def _dot(a, b):
    return jnp.dot(a, b, preferred_element_type=F32)


def _dot_t(a, b):
    return lax.dot_general(a, b, (((1,), (1,)), ((), ())), preferred_element_type=F32)


def _softplus(x):
    return jnp.maximum(x, 0.0) + jnp.log1p(jnp.exp(-jnp.abs(x)))


def _expm1(x):
    return jnp.tanh(0.5 * x) * (jnp.exp(x) + 1.0)


def _silu(x):
    return x * jax.nn.sigmoid(x)


def _rmsnorm(x, g):
    return x * lax.rsqrt(jnp.mean(x * x, axis=-1, keepdims=True) + EPS) * g


def _masked_softmax(s, mask):
    s = jnp.where(mask, s, -jnp.inf)
    m = jnp.max(s, axis=-1, keepdims=True)
    m = jnp.where(jnp.isfinite(m), m, 0.0)
    p = jnp.exp(s - m)
    return p / jnp.maximum(jnp.sum(p, axis=-1, keepdims=True), 1e-30)


def _head_norm_gate(o, z, g):
    outs = []
    for h in range(N_HEADS):
        oh = o[:, h * HEAD_DIM:(h + 1) * HEAD_DIM]
        outs.append(oh * lax.rsqrt(jnp.mean(oh * oh, axis=-1, keepdims=True) + EPS))
    on = jnp.concatenate(outs, axis=-1) * g
    return (on * _silu(z)).astype(BF16)


def _layer_spec(arr, *lead):
    rest = arr.shape[len(lead):]
    idx = tuple(lead) + (0,) * len(rest)
    return pl.BlockSpec((None,) * len(lead) + rest, lambda *_: idx)


def _seg_spec(name, rows, nt):
    off, wp = _DST[name]
    cb = off // wp
    return pl.BlockSpec((rows, wp), lambda b, i: (b * nt + i, cb))


def _params(*sem):
    return pltpu.CompilerParams(dimension_semantics=sem, vmem_limit_bytes=VMEM_LIMIT)


def _prenorm_kernel(x_ref, g_ref, o_ref):
    o_ref[...] = _rmsnorm(x_ref[...], g_ref[...]).astype(BF16)


def _prenorm(x2, g, l, tm=512):
    T, D = x2.shape
    return pl.pallas_call(
        _prenorm_kernel,
        out_shape=jax.ShapeDtypeStruct((T, D), BF16),
        grid=(T // tm,),
        in_specs=[pl.BlockSpec((tm, D), lambda i: (i, 0)), _layer_spec(g, l)],
        out_specs=pl.BlockSpec((tm, D), lambda i: (i, 0)),
        compiler_params=_params("parallel"),
        name="prenorm",
    )(x2, g)


def _in_proj_kernel(h_ref, w_ref, o_ref):
    o_ref[...] = _dot(h_ref[...], w_ref[...])


def _in_proj(hn, w, l, tm=1024, tn=1024):
    T, D = hn.shape
    NP = w.shape[2]
    return pl.pallas_call(
        _in_proj_kernel,
        out_shape=jax.ShapeDtypeStruct((T, NP), F32),
        grid=(T // tm, NP // tn),
        in_specs=[pl.BlockSpec((tm, D), lambda i, j: (i, 0)),
                  pl.BlockSpec((None, D, tn), lambda i, j: (l, 0, j))],
        out_specs=pl.BlockSpec((tm, tn), lambda i, j: (i, j)),
        compiler_params=_params("parallel", "arbitrary"),
        name="in_proj",
    )(hn, w)


def _out_proj_kernel(ma_ref, mb_ref, mc_ref, md_ref, w_ref, x_ref, g_ref, *rest):
    m = jnp.concatenate([ma_ref[...], mb_ref[...], mc_ref[...], md_ref[...]], axis=-1)
    x_new = x_ref[...] + _rmsnorm(_dot(m, w_ref[...]), g_ref[...])
    if len(rest) == 1:
        rest[0][...] = x_new
    else:
        gn_ref, o_ref, hn_ref = rest
        o_ref[...] = x_new
        hn_ref[...] = _rmsnorm(x_new, gn_ref[...]).astype(BF16)


def _out_proj(ms, w, x2, post_g, l, pre_g=None, tm=512):
    T, D = x2.shape
    mspec = pl.BlockSpec((tm, W_BR), lambda i: (i, 0))
    xspec = pl.BlockSpec((tm, D), lambda i: (i, 0))
    in_specs = [mspec] * N_MIXERS + [_layer_spec(w, l), xspec, _layer_spec(post_g, l)]
    args = list(ms) + [w, x2, post_g]
    out_shape = jax.ShapeDtypeStruct((T, D), F32)
    out_specs = xspec
    if pre_g is not None:
        in_specs.append(_layer_spec(pre_g, l + 1))
        args.append(pre_g)
        out_shape = (out_shape, jax.ShapeDtypeStruct((T, D), BF16))
        out_specs = (xspec, xspec)
    return pl.pallas_call(
        _out_proj_kernel,
        out_shape=out_shape,
        grid=(T // tm,),
        in_specs=in_specs,
        out_specs=out_specs,
        compiler_params=_params("parallel"),
        name="out_proj",
    )(*args)


def _gla_kernel(q_ref, k_ref, v_ref, fg_ref, z_ref, wfg_ref, bfg_ref, g_ref, o_ref, state_ref, acc_ref):
    R = q_ref.shape[0]
    C = GLA_CHUNK

    @pl.when(pl.program_id(1) == 0)
    def _():
        state_ref[...] = jnp.zeros(state_ref.shape, F32)

    pre = _dot(fg_ref[...].astype(BF16), wfg_ref[...]) + bfg_ref[...]
    log_f = -_softplus(-pre) * (1.0 / GLA_TAU)
    row = lax.broadcasted_iota(jnp.int32, log_f.shape, 0) % C
    bcum = log_f
    d = 1
    while d < C:
        bcum = bcum + jnp.where(row >= d, pltpu.roll(bcum, d, 0), 0.0)
        d *= 2

    q = q_ref[...] * (GLA_HEAD_K ** -0.5)
    k = k_ref[...]
    v = v_ref[...]
    lane = lax.broadcasted_iota(jnp.int32, (C, LANES), 1)
    causal = (lax.broadcasted_iota(jnp.int32, (C, C), 0) >= lax.broadcasted_iota(jnp.int32, (C, C), 1))
    head_lanes = (lane < GLA_HEAD_K, lane >= GLA_HEAD_K)
    for c in range(R // C):
        rs = slice(c * C, (c + 1) * C)
        for p in range(2):
            ls = slice(p * LANES, (p + 1) * LANES)
            b = bcum[rs, ls]
            b_last = b[C - 1:C, :]
            kk = k[rs, ls]
            q_dec = q[rs, ls] * jnp.exp(b)
            k_dec = kk * jnp.exp(-b)
            k_last = kk * jnp.exp(b_last - b)
            decay = jnp.exp(b_last)
            for hh in range(2):
                h = 2 * p + hh
                m = head_lanes[hh]
                qd = jnp.where(m, q_dec, 0.0).astype(BF16)
                kd = jnp.where(m, k_dec, 0.0).astype(BF16)
                kl = jnp.where(m, k_last, 0.0).astype(BF16)
                vh = v[rs, h * HEAD_DIM:(h + 1) * HEAD_DIM]
                attn = jnp.where(causal, _dot_t(qd, kd), 0.0)
                st = state_ref[h]
                o = _dot(attn.astype(BF16), vh.astype(BF16)) + _dot_t(qd, st.astype(BF16))
                acc_ref[rs, h * HEAD_DIM:(h + 1) * HEAD_DIM] = o
                state_ref[h] = decay * st + _dot(vh.T.astype(BF16), kl)

    o_ref[...] = _head_norm_gate(acc_ref[...], z_ref[...], g_ref[...])


def _gla(proj, wfg, bfg, bg, l, B, S, R=256):
    nt = S // R
    T = B * S
    return pl.pallas_call(
        _gla_kernel,
        out_shape=jax.ShapeDtypeStruct((T, W_BR), BF16),
        grid=(B, nt),
        in_specs=[_seg_spec('gla_q', R, nt), _seg_spec('gla_k', R, nt), _seg_spec('gla_v', R, nt),
                  _seg_spec('gla_fg', R, nt), _seg_spec('gla_z', R, nt),
                  _layer_spec(wfg, l), _layer_spec(bfg, l), _layer_spec(bg, l, 0)],
        out_specs=pl.BlockSpec((R, W_BR), lambda b, i: (b * nt + i, 0)),
        scratch_shapes=[pltpu.VMEM((N_HEADS, HEAD_DIM, LANES), F32), pltpu.VMEM((R, W_BR), F32)],
        compiler_params=_params("parallel", "arbitrary"),
        name="gla",
    )(proj, proj, proj, proj, proj, wfg, bfg, bg)


def _lru_kernel(x_ref, z_ref, cw_ref, cb_ref, wa_ref, ba_ref, wx_ref, bx_ref, lam_ref, g_ref, o_ref,
                hist_ref, h_ref, acc_ref):
    R = x_ref.shape[0]
    HIST = hist_ref.shape[0]

    @pl.when(pl.program_id(1) == 0)
    def _():
        hist_ref[...] = jnp.zeros(hist_ref.shape, F32)
        h_ref[...] = jnp.zeros(h_ref.shape, F32)

    x = x_ref[...]
    xe = jnp.concatenate([hist_ref[...], x], axis=0)
    xc = cb_ref[...] + cw_ref[LRU_CONV - 1:LRU_CONV, :] * x
    for kk in range(LRU_CONV - 1):
        sh = LRU_CONV - 1 - kk
        xc = xc + cw_ref[kk:kk + 1, :] * pltpu.roll(xe, sh, 0)[HIST:HIST + R]
    hist_ref[...] = x[R - HIST:R]

    xb = xc.astype(BF16)
    ra, rx = [], []
    for h in range(N_HEADS):
        xh = xb[:, h * HEAD_DIM:(h + 1) * HEAD_DIM]
        ra.append(_dot(xh, wa_ref[h]))
        rx.append(_dot(xh, wx_ref[h]))
    r = jax.nn.sigmoid(jnp.concatenate(ra, axis=-1) + ba_ref[...])
    ig = jax.nn.sigmoid(jnp.concatenate(rx, axis=-1) + bx_ref[...])
    log_a = -LRU_C * r * _softplus(-lam_ref[...])
    a = jnp.exp(log_a)
    u = jnp.sqrt(-_expm1(2.0 * log_a)) * (ig * xc)

    G = R // SUBLANES
    a3 = a.reshape(G, SUBLANES, W_BR)
    u3 = u.reshape(G, SUBLANES, W_BR)
    row = lax.broadcasted_iota(jnp.int32, a3.shape, 1)
    d = 1
    while d < SUBLANES:
        keep = row >= d
        u3 = jnp.where(keep, a3 * pltpu.roll(u3, d, 1) + u3, u3)
        a3 = jnp.where(keep, a3 * pltpu.roll(a3, d, 1), a3)
        d *= 2
    carry = h_ref[...]
    for gi in range(G):
        hg = a3[gi] * carry + u3[gi]
        acc_ref[gi * SUBLANES:(gi + 1) * SUBLANES, :] = hg
        carry = hg[SUBLANES - 1:SUBLANES, :]
    h_ref[...] = carry
    o_ref[...] = _head_norm_gate(acc_ref[...], z_ref[...], g_ref[...])


def _lru(proj, cw, cb, wa, ba, wx, bx, lam, bg, l, B, S, R=256):
    nt = S // R
    T = B * S
    return pl.pallas_call(
        _lru_kernel,
        out_shape=jax.ShapeDtypeStruct((T, W_BR), BF16),
        grid=(B, nt),
        in_specs=[_seg_spec('lru_x', R, nt), _seg_spec('lru_z', R, nt),
                  _layer_spec(cw, l), _layer_spec(cb, l), _layer_spec(wa, l), _layer_spec(ba, l),
                  _layer_spec(wx, l), _layer_spec(bx, l), _layer_spec(lam, l), _layer_spec(bg, l, 1)],
        out_specs=pl.BlockSpec((R, W_BR), lambda b, i: (b * nt + i, 0)),
        scratch_shapes=[pltpu.VMEM((SUBLANES, W_BR), F32), pltpu.VMEM((1, W_BR), F32),
                        pltpu.VMEM((R, W_BR), F32)],
        compiler_params=_params("parallel", "arbitrary"),
        name="rglru",
    )(proj, proj, cw, cb, wa, ba, wx, bx, lam, bg)


def _conf_kernel(v_ref, glu_ref, z_ref, dw_ref, db_ref, lg_ref, lb_ref, pw_ref, pb_ref, g_ref, o_ref,
                 hist_ref):
    R = v_ref.shape[0]
    HIST = hist_ref.shape[0]
    assert HIST >= CONF_KERNEL - 1 and HIST % SUBLANES == 0

    @pl.when(pl.program_id(1) == 0)
    def _():
        hist_ref[...] = jnp.zeros(hist_ref.shape, F32)

    y = v_ref[...] * jax.nn.sigmoid(glu_ref[...])
    ye = jnp.concatenate([hist_ref[...], y], axis=0)
    acc = db_ref[...]
    for r in range(SUBLANES):
        yr = ye if r == 0 else pltpu.roll(ye, r, 0)
        for q in range((CONF_KERNEL - 1 - r) // SUBLANES + 1):
            kk = CONF_KERNEL - 1 - (SUBLANES * q + r)
            start = HIST - SUBLANES * q
            acc = acc + dw_ref[kk:kk + 1, :] * yr[start:start + R]
    hist_ref[...] = y[R - HIST:R]

    mu = jnp.mean(acc, axis=-1, keepdims=True)
    xc = acc - mu
    var = jnp.mean(xc * xc, axis=-1, keepdims=True)
    yn = xc * lax.rsqrt(var + EPS) * lg_ref[...] + lb_ref[...]
    o = _dot(_silu(yn).astype(BF16), pw_ref[...]) + pb_ref[...]
    o_ref[...] = _head_norm_gate(o, z_ref[...], g_ref[...])


def _conf(proj, dw, db, lg, lb, pw, pb, bg, l, B, S, R=256):
    nt = S // R
    T = B * S
    return pl.pallas_call(
        _conf_kernel,
        out_shape=jax.ShapeDtypeStruct((T, W_BR), BF16),
        grid=(B, nt),
        in_specs=[_seg_spec('conv_v', R, nt), _seg_spec('conv_glu', R, nt), _seg_spec('conv_z', R, nt),
                  _layer_spec(dw, l), _layer_spec(db, l), _layer_spec(lg, l), _layer_spec(lb, l),
                  _layer_spec(pw, l), _layer_spec(pb, l), _layer_spec(bg, l, 3)],
        out_specs=pl.BlockSpec((R, W_BR), lambda b, i: (b * nt + i, 0)),
        scratch_shapes=[pltpu.VMEM((32, W_BR), F32)],
        compiler_params=_params("parallel", "arbitrary"),
        name="conformer",
    )(proj, proj, proj, dw, db, lg, lb, pw, pb, bg)


def _cmp_kernel(kc_ref, vc_ref, posk_ref, w1k_ref, w2k_ref, posv_ref, w1v_ref, w2v_ref, ko_ref, vo_ref):
    S = kc_ref.shape[0]
    NB = S // CMP_STRIDE
    HALF = CMP_LEN // 2

    def compress(z_ref, pos_ref, w1_ref, w2_ref):
        lo = jnp.zeros((NB, w1_ref.shape[2]), F32)
        hi = jnp.zeros((NB, w1_ref.shape[2]), F32)
        for l in range(HALF):
            grp = z_ref[pl.ds(l, NB, stride=CMP_STRIDE), :]
            lo = lo + _dot((grp + pos_ref[l:l + 1, :]).astype(BF16), w1_ref[l])
            hi = hi + _dot((grp + pos_ref[HALF + l:HALF + l + 1, :]).astype(BF16), w1_ref[HALF + l])
        hid = lo + pltpu.roll(hi, NB - 1, 0)
        blk = lax.broadcasted_iota(jnp.int32, hid.shape, 0)
        hid = jnp.where(blk < NB - 1, hid, 0.0)
        return _dot(_silu(hid).astype(BF16), w2_ref[...])

    ko_ref[...] = compress(kc_ref, posk_ref, w1k_ref, w2k_ref)
    vo_ref[...] = compress(vc_ref, posv_ref, w1v_ref, w2v_ref)


def _nsa_compress(proj, posk, w1k, w2k, posv, w1v, w2v, l, B, S):
    NB = S // CMP_STRIDE

    def seg(name):
        off, wp = _DST[name]
        return pl.BlockSpec((S, wp), lambda b: (b, off // wp))

    out = jax.ShapeDtypeStruct((B, NB, HEAD_DIM), F32)
    ospec = pl.BlockSpec((None, NB, HEAD_DIM), lambda b: (b, 0, 0))
    return pl.pallas_call(
        _cmp_kernel,
        out_shape=(out, out),
        grid=(B,),
        in_specs=[seg('nsa_kc'), seg('nsa_vc'), _layer_spec(posk, l), _layer_spec(w1k, l),
                  _layer_spec(w2k, l), _layer_spec(posv, l), _layer_spec(w1v, l), _layer_spec(w2v, l)],
        out_specs=(ospec, ospec),
        compiler_params=_params("parallel"),
        name="nsa_compress",
    )(proj, proj, posk, w1k, w2k, posv, w1v, w2v)


def _nsa_kernel(q_ref, gt_ref, z_ref, kc_ref, vc_ref, ks_ref, vs_ref, kw_ref, vw_ref, g_ref, o_ref,
                m_ref, acc_ref):
    TQ = q_ref.shape[0]
    S = ks_ref.shape[0]
    NB = kc_ref.shape[0]
    H, Dh = N_HEADS, HEAD_DIM
    scale = Dh ** -0.5
    i = pl.program_id(1)
    t0 = i * TQ

    q = q_ref[...]
    q4 = jnp.concatenate([q[:, h * Dh:(h + 1) * Dh] for h in range(H)], axis=0).astype(BF16)

    t_c = t0 + lax.broadcasted_iota(jnp.int32, (TQ, NB), 0)
    n_c = lax.broadcasted_iota(jnp.int32, (TQ, NB), 1)
    cmask = t_c >= n_c * CMP_STRIDE + (CMP_LEN - 1)
    cmask4 = jnp.concatenate([cmask] * H, axis=0)
    p4 = _masked_softmax(_dot_t(q4, kc_ref[...].astype(BF16)) * scale, cmask4)
    p4b = p4.astype(BF16)
    o_cmp4 = _dot(p4b, vc_ref[...].astype(BF16))

    n_sel = S // SLC_LEN
    jj = lax.broadcasted_iota(jnp.int32, (n_sel, NB), 0)
    nn = lax.broadcasted_iota(jnp.int32, (n_sel, NB), 1)
    ov = ((nn * CMP_STRIDE < (jj + 1) * SLC_LEN) & (nn * CMP_STRIDE + CMP_LEN > jj * SLC_LEN)
          & (nn < NB - 1))
    po = _dot_t(jnp.where(ov, 1.0, 0.0).astype(BF16), p4b)
    imp = po[:, 0:TQ]
    for h in range(1, H):
        imp = imp + po[:, h * TQ:(h + 1) * TQ]

    j = lax.broadcasted_iota(jnp.int32, (n_sel, TQ), 0)
    cur = (t0 + lax.broadcasted_iota(jnp.int32, (n_sel, TQ), 1)) // SLC_LEN
    forced = (j == 0) | (j == cur) | (j == cur - 1)
    val = jnp.where(j > cur, -jnp.inf, jnp.where(forced, FORCED_SCORE, imp))
    rank = jnp.zeros((n_sel, TQ), F32)
    for ii in range(n_sel):
        vi = val[ii:ii + 1, :]
        beats = jnp.where(vi > val, 1.0, jnp.where(vi == val, jnp.where(j > ii, 1.0, 0.0), 0.0))
        rank = rank + beats
    sel_t = jnp.where(rank < SLC_TOPN, 1.0, 0.0)
    sel = jnp.concatenate([sel_t, jnp.zeros((LANES - n_sel, TQ), F32)], axis=0).T.astype(BF16)

    t_k = t0 + lax.broadcasted_iota(jnp.int32, (TQ, TQ), 0)
    k_i = lax.broadcasted_iota(jnp.int32, (TQ, TQ), 1)
    ones_half = jnp.ones((TQ, Dh), BF16)
    neg_inf = jnp.full((TQ, TQ), -jnp.inf, F32)
    scale_log2e = scale * LOG2_E

    def reset():
        m_ref[...] = jnp.full(m_ref.shape, -jnp.inf, F32)
        acc_ref[...] = jnp.zeros(acc_ref.shape, F32)

    def attend_chunk(k_ref, v_ref, c, bias):
        k0 = pl.multiple_of(c * TQ, TQ)
        kb = k_ref[pl.ds(k0, TQ), :].astype(BF16)
        vb = jnp.concatenate([v_ref[pl.ds(k0, TQ), :].astype(BF16), ones_half], axis=-1)
        s4 = _dot_t(q4, kb)
        for h in range(H):
            rs = slice(h * TQ, (h + 1) * TQ)
            s = s4[rs] if bias is None else s4[rs] + bias
            m_old = m_ref[rs]
            m_new = jnp.maximum(m_old, jnp.max(s, axis=-1, keepdims=True))
            m_safe = jnp.where(m_new == -jnp.inf, 0.0, m_new)
            p = jnp.exp2((s - jnp.concatenate([m_safe] * (TQ // LANES), axis=-1)) * scale_log2e)
            alpha = jnp.exp2((m_old - m_safe) * scale_log2e)
            acc_ref[rs] = jnp.concatenate([alpha, alpha], axis=-1) * acc_ref[rs] + _dot(p.astype(BF16), vb)
            m_ref[rs] = m_new

    def finish():
        acc = acc_ref[...]
        return acc[:, :Dh] / jnp.maximum(acc[:, Dh:], 1e-30)

    eb = lax.broadcasted_iota(jnp.int32, (LANES, TQ), 0)
    ek = lax.broadcasted_iota(jnp.int32, (LANES, TQ), 1) // SLC_LEN

    def sel_keys(c):
        expand = jnp.where(eb == ek + c * (TQ // SLC_LEN), 1.0, 0.0).astype(BF16)
        return _dot(sel, expand) > 0.5

    reset()

    def slc_body(c, carry):
        attend_chunk(ks_ref, vs_ref, c, jnp.where(sel_keys(c), 0.0, neg_inf))
        return carry

    lax.fori_loop(0, i, slc_body, 0)
    attend_chunk(ks_ref, vs_ref, i, jnp.where(sel_keys(i) & (t0 + k_i <= t_k), 0.0, neg_inf))
    o_slc4 = finish()

    reset()
    n_back = WIN // TQ
    for back in range(n_back, -1, -1):
        def win_chunk(back=back):
            c = i - back
            if 0 < back < n_back:
                bias = None
            else:
                dist = t_k - (c * TQ + k_i)
                bias = jnp.where((dist >= 0) & (dist < WIN), 0.0, neg_inf)
            attend_chunk(kw_ref, vw_ref, c, bias)
        if back == 0:
            win_chunk()
        else:
            pl.when(i >= back)(win_chunk)
    o_win4 = finish()

    gs = jax.nn.sigmoid(gt_ref[...])
    outs = []
    for h in range(H):
        rs = slice(h * TQ, (h + 1) * TQ)
        outs.append(gs[:, 3 * h:3 * h + 1] * o_cmp4[rs] + gs[:, 3 * h + 1:3 * h + 2] * o_slc4[rs]
                    + gs[:, 3 * h + 2:3 * h + 3] * o_win4[rs])
    o_ref[...] = _head_norm_gate(jnp.concatenate(outs, axis=-1), z_ref[...], g_ref[...])


def _nsa(proj, k_cmp, v_cmp, bg, l, B, S, TQ=256):
    assert WIN % TQ == 0 and TQ % SLC_LEN == 0 and TQ % LANES == 0
    nt = S // TQ
    T = B * S
    NB = k_cmp.shape[1]

    def seq(name):
        off, wp = _DST[name]
        return pl.BlockSpec((S, wp), lambda b, i: (b, off // wp))

    cspec = pl.BlockSpec((None, NB, HEAD_DIM), lambda b, i: (b, 0, 0))
    return pl.pallas_call(
        _nsa_kernel,
        out_shape=jax.ShapeDtypeStruct((T, W_BR), BF16),
        grid=(B, nt),
        in_specs=[_seg_spec('nsa_q', TQ, nt), _seg_spec('nsa_g', TQ, nt), _seg_spec('nsa_z', TQ, nt),
                  cspec, cspec, seq('nsa_ks'), seq('nsa_vs'), seq('nsa_kw'), seq('nsa_vw'),
                  _layer_spec(bg, l, 2)],
        out_specs=pl.BlockSpec((TQ, W_BR), lambda b, i: (b * nt + i, 0)),
        scratch_shapes=[pltpu.VMEM((N_HEADS * TQ, LANES), F32),
                        pltpu.VMEM((N_HEADS * TQ, 2 * HEAD_DIM), F32)],
        compiler_params=_params("parallel", "arbitrary"),
        name="nsa_attention",
    )(proj, proj, proj, k_cmp, v_cmp, proj, proj, proj, proj, bg)


def kernel(x, pre_norm_g, w_in, gla_w_fg2, gla_b_fg2, lru_conv_w, lru_conv_b, lru_w_a, lru_b_a, lru_w_x,
           lru_b_x, lru_lambda, nsa_cmp_pos_k, nsa_cmp_w1_k, nsa_cmp_w2_k, nsa_cmp_pos_v, nsa_cmp_w1_v,
           nsa_cmp_w2_v, conf_dw_w, conf_dw_b, conf_ln_g, conf_ln_b, conf_pw_w, conf_pw_b, branch_norm_g,
           w_out, post_norm_g):
    B, S, D = x.shape
    L = w_in.shape[0]
    T = B * S
    x2 = x.reshape(T, D)

    def rows(a):
        return a.reshape(L, 1, -1)

    w_in_p = _relayout_w_in(w_in)
    w_out_b = w_out.astype(BF16)
    lowrank = gla_w_fg2.shape[1]
    wfg = jnp.pad(gla_w_fg2, ((0, 0), (0, LANES - lowrank), (0, 0))).astype(BF16)
    wa = lru_w_a.astype(BF16)
    wx = lru_w_x.astype(BF16)
    w1k = nsa_cmp_w1_k.reshape(L, CMP_LEN, HEAD_DIM, -1).astype(BF16)
    w1v = nsa_cmp_w1_v.reshape(L, CMP_LEN, HEAD_DIM, -1).astype(BF16)
    w2k = nsa_cmp_w2_k.astype(BF16)
    w2v = nsa_cmp_w2_v.astype(BF16)
    pw = conf_pw_w.astype(BF16)
    bg = branch_norm_g.reshape(L, N_MIXERS, 1, W_BR)
    pre_g, post_g = rows(pre_norm_g), rows(post_norm_g)
    bfg = rows(gla_b_fg2)
    lcb, lba, lbx, lam = rows(lru_conv_b), rows(lru_b_a), rows(lru_b_x), rows(lru_lambda)
    cdb, clg, clb, cpb = rows(conf_dw_b), rows(conf_ln_g), rows(conf_ln_b), rows(conf_pw_b)

    hn = _prenorm(x2, pre_g, 0)
    for l in range(L):
        proj = _in_proj(hn, w_in_p, l)
        m_a = _gla(proj, wfg, bfg, bg, l, B, S)
        m_b = _lru(proj, lru_conv_w, lcb, wa, lba, wx, lbx, lam, bg, l, B, S)
        k_cmp, v_cmp = _nsa_compress(proj, nsa_cmp_pos_k, w1k, w2k, nsa_cmp_pos_v, w1v, w2v, l, B, S)
        m_c = _nsa(proj, k_cmp, v_cmp, bg, l, B, S)
        m_d = _conf(proj, conf_dw_w, cdb, clg, clb, pw, cpb, bg, l, B, S)
        ms = (m_a, m_b, m_c, m_d)
        if l + 1 < L:
            x2, hn = _out_proj(ms, w_out_b, x2, post_g, l, pre_g)
        else:
            x2 = _out_proj(ms, w_out_b, x2, post_g, l)
    return x2.reshape(B, S, D)
```

```python
import functools

import jax
import jax.numpy as jnp
from jax import lax
from jax.experimental import pallas as pl
from jax.experimental.pallas import tpu as pltpu

F32 = jnp.float32
BF16 = jnp.bfloat16

D_MODEL = 2048
N_HEADS = 4
HEAD_DIM = 128
W_BR = N_HEADS * HEAD_DIM
N_MIXERS = 4
GLA_HEAD_K = 64
GLA_TAU = 16.0
GLA_CHUNK = 64
LRU_C = 8.0
LRU_CONV = 4
CMP_LEN = 32
CMP_STRIDE = 16
SLC_LEN = 64
SLC_TOPN = 16
WIN = 512
FORCED_SCORE = 1e3
CONF_KERNEL = 31
CONF_HIST = 32
EPS = 1e-6
LOG2_E = 1.4426950408889634
SUBLANES = 8
LANES = 128
VMEM_LIMIT = 48 * 1024 * 1024

_SEGS = (('gla_q', 256), ('gla_k', 256), ('gla_v', 512), ('gla_fg', 16), ('gla_z', 512),
         ('lru_x', 512), ('lru_z', 512),
         ('nsa_q', 512), ('nsa_kc', 128), ('nsa_vc', 128), ('nsa_ks', 128), ('nsa_vs', 128),
         ('nsa_kw', 128), ('nsa_vw', 128), ('nsa_g', 12), ('nsa_z', 512),
         ('conv_v', 512), ('conv_glu', 512), ('conv_z', 512))
_ORDER = ('gla_v', 'gla_z', 'lru_x', 'lru_z', 'nsa_q', 'nsa_z', 'conv_v', 'conv_glu', 'conv_z',
          'gla_q', 'gla_k', 'gla_fg', 'nsa_kc', 'nsa_vc', 'nsa_ks', 'nsa_vs', 'nsa_kw', 'nsa_vw',
          'nsa_g')


def _layout():
    src, off = {}, 0
    for name, w in _SEGS:
        src[name] = (off, w)
        off += w
    dst, off = {}, 0
    for name in _ORDER:
        w = src[name][1]
        wp = -(-w // LANES) * LANES
        assert off % wp == 0
        dst[name] = (off, wp)
        off += wp
    return src, dst, off


_SRC, _DST, D_PROJ_PAD = _layout()


def _relayout_w_in(w_in):
    wt = jnp.transpose(w_in, (2, 0, 1))
    parts = []
    for name in _ORDER:
        o, w = _SRC[name]
        wp = _DST[name][1]
        p = wt[o:o + w]
        if wp != w:
            p = jnp.pad(p, ((0, wp - w), (0, 0), (0, 0)))
        parts.append(p)
    return jnp.transpose(jnp.concatenate(parts, axis=0).astype(BF16), (1, 0, 2))


def _dot(a, b):
    return jnp.dot(a, b, preferred_element_type=F32)


def _dot_t(a, b):
    return lax.dot_general(a, b, (((1,), (1,)), ((), ())), preferred_element_type=F32)


def _softplus(x):
    return jnp.maximum(x, 0.0) + jnp.log1p(jnp.exp(-jnp.abs(x)))


def _expm1(x):
    return jnp.tanh(0.5 * x) * (jnp.exp(x) + 1.0)


def _silu(x):
    return x * jax.nn.sigmoid(x)


def _rmsnorm(x, g):
    return x * lax.rsqrt(jnp.mean(x * x, axis=-1, keepdims=True) + EPS) * g


def _masked_softmax(s, mask):
    s = jnp.where(mask, s, -jnp.inf)
    m = jnp.max(s, axis=-1, keepdims=True)
    m = jnp.where(jnp.isfinite(m), m, 0.0)
    p = jnp.exp(s - m)
    return p / jnp.maximum(jnp.sum(p, axis=-1, keepdims=True), 1e-30)


def _head_norm_gate(o, z, g):
    outs = []
    for h in range(N_HEADS):
        oh = o[:, h * HEAD_DIM:(h + 1) * HEAD_DIM]
        outs.append(oh * lax.rsqrt(jnp.mean(oh * oh, axis=-1, keepdims=True) + EPS))
    on = jnp.concatenate(outs, axis=-1) * g
    return (on * _silu(z)).astype(BF16)


def _layer_spec(arr, *lead):
    rest = arr.shape[len(lead):]
    idx = tuple(lead) + (0,) * len(rest)
    return pl.BlockSpec((None,) * len(lead) + rest, lambda *_: idx)


def _seg_spec(name, rows, nt):
    off, wp = _DST[name]
    cb = off // wp
    return pl.BlockSpec((rows, wp), lambda b, i: (b * nt + i, cb))


def _params(*sem):
    return pltpu.CompilerParams(dimension_semantics=sem, vmem_limit_bytes=VMEM_LIMIT)


def _prenorm_kernel(x_ref, g_ref, o_ref):
    o_ref[...] = _rmsnorm(x_ref[...], g_ref[...]).astype(BF16)


def _prenorm(x2, g, l, tm=512):
    T, D = x2.shape
    return pl.pallas_call(
        _prenorm_kernel,
        out_shape=jax.ShapeDtypeStruct((T, D), BF16),
        grid=(T // tm,),
        in_specs=[pl.BlockSpec((tm, D), lambda i: (i, 0)), _layer_spec(g, l)],
        out_specs=pl.BlockSpec((tm, D), lambda i: (i, 0)),
        compiler_params=_params("parallel"),
        name="prenorm",
    )(x2, g)


def _in_proj_kernel(h_ref, wt_ref, o_ref):
    o_ref[...] = _dot_t(h_ref[...], wt_ref[...])


def _in_proj(hn, wt, l, tm=1024, tn=1024):
    T, D = hn.shape
    NP = wt.shape[1]
    return pl.pallas_call(
        _in_proj_kernel,
        out_shape=jax.ShapeDtypeStruct((T, NP), F32),
        grid=(T // tm, NP // tn),
        in_specs=[pl.BlockSpec((tm, D), lambda i, j: (i, 0)),
                  pl.BlockSpec((None, tn, D), lambda i, j: (l, j, 0))],
        out_specs=pl.BlockSpec((tm, tn), lambda i, j: (i, j)),
        compiler_params=_params("parallel", "arbitrary"),
        name="in_proj",
    )(hn, wt)


N_GLA_IN, N_LRU_IN, N_CONF_IN = 8, 10, 10


def _mix_out_kernel(*refs, emit_hn):
    refs = list(refs)
    gla_in = [refs.pop(0) for _ in range(N_GLA_IN)]
    lru_in = [refs.pop(0) for _ in range(N_LRU_IN)]
    conf_in = [refs.pop(0) for _ in range(N_CONF_IN)]
    mc_ref, w_ref, x_ref, pg_ref = [refs.pop(0) for _ in range(4)]
    gn_ref = refs.pop(0) if emit_hn else None
    o_ref = refs.pop(0)
    hn_ref = refs.pop(0) if emit_hn else None
    gla_state, gla_acc, lru_hist, lru_h, lru_acc, conf_hist = refs

    @pl.when(pl.program_id(1) == 0)
    def _():
        for ref in (gla_state, lru_hist, lru_h, conf_hist):
            ref[...] = jnp.zeros(ref.shape, F32)

    def w_rows(m):
        return w_ref[m * W_BR:(m + 1) * W_BR, :]

    y = _dot(mc_ref[...], w_rows(2))
    y = y + _dot(_gla_mix(*gla_in, gla_state, gla_acc), w_rows(0))
    y = y + _dot(_lru_mix(*lru_in, lru_hist, lru_h, lru_acc), w_rows(1))
    y = y + _dot(_conf_mix(*conf_in, conf_hist), w_rows(3))
    x_new = x_ref[...] + _rmsnorm(y, pg_ref[...])
    o_ref[...] = x_new
    if emit_hn:
        hn_ref[...] = _rmsnorm(x_new, gn_ref[...]).astype(BF16)


def _mix_out(proj, m_c, x2, p, l, B, S, emit_hn, R=256):
    nt = S // R
    T, D = x2.shape
    bg = p['bg']

    def tile(width):
        return pl.BlockSpec((R, width), lambda b, i: (b * nt + i, 0))

    def seg(*names):
        return [_seg_spec(n, R, nt) for n in names]

    def layer(*names):
        return [_layer_spec(p[n], l) for n in names]

    gla = seg('gla_q', 'gla_k', 'gla_v', 'gla_fg', 'gla_z') + layer('wfg', 'bfg') + [_layer_spec(bg, l, 0)]
    lru = (seg('lru_x', 'lru_z') + layer('lru_cw', 'lru_cb', 'wa', 'lru_ba', 'wx', 'lru_bx', 'lru_lam')
           + [_layer_spec(bg, l, 1)])
    conf = (seg('conv_v', 'conv_glu', 'conv_z') + layer('conf_dw', 'conf_db', 'conf_lg', 'conf_lb', 'pw', 'conf_pb')
            + [_layer_spec(bg, l, 3)])
    assert (len(gla), len(lru), len(conf)) == (N_GLA_IN, N_LRU_IN, N_CONF_IN)
    in_specs = gla + lru + conf + [tile(W_BR), _layer_spec(p['w_out'], l), tile(D), _layer_spec(p['post_g'], l)]
    args = ([proj] * 5 + [p['wfg'], p['bfg'], bg]
            + [proj] * 2 + [p[n] for n in ('lru_cw', 'lru_cb', 'wa', 'lru_ba', 'wx', 'lru_bx', 'lru_lam')] + [bg]
            + [proj] * 3 + [p[n] for n in ('conf_dw', 'conf_db', 'conf_lg', 'conf_lb', 'pw', 'conf_pb')] + [bg]
            + [m_c, p['w_out'], x2, p['post_g']])
    out_shape = jax.ShapeDtypeStruct((T, D), F32)
    out_specs = tile(D)
    if emit_hn:
        in_specs.append(_layer_spec(p['pre_g'], l + 1))
        args.append(p['pre_g'])
        out_shape = (out_shape, jax.ShapeDtypeStruct((T, D), BF16))
        out_specs = (tile(D), tile(D))
    return pl.pallas_call(
        functools.partial(_mix_out_kernel, emit_hn=emit_hn),
        out_shape=out_shape,
        grid=(B, nt),
        in_specs=in_specs,
        out_specs=out_specs,
        scratch_shapes=[pltpu.VMEM((N_HEADS, HEAD_DIM, LANES), F32), pltpu.VMEM((R, W_BR), F32),
                        pltpu.VMEM((SUBLANES, W_BR), F32), pltpu.VMEM((1, W_BR), F32),
                        pltpu.VMEM((R, W_BR), F32), pltpu.VMEM((CONF_HIST, W_BR), F32)],
        compiler_params=_params("parallel", "arbitrary"),
        name="mix_out",
    )(*args)


def _gla_mix(q_ref, k_ref, v_ref, fg_ref, z_ref, wfg_ref, bfg_ref, g_ref, state_ref, acc_ref):
    R = q_ref.shape[0]
    C = GLA_CHUNK

    pre = _dot(fg_ref[...].astype(BF16), wfg_ref[...]) + bfg_ref[...]
    log_f = -_softplus(-pre) * (1.0 / GLA_TAU)
    row = lax.broadcasted_iota(jnp.int32, log_f.shape, 0) % C
    bcum = log_f
    d = 1
    while d < C:
        bcum = bcum + jnp.where(row >= d, pltpu.roll(bcum, d, 0), 0.0)
        d *= 2

    q = q_ref[...] * (GLA_HEAD_K ** -0.5)
    k = k_ref[...]
    v = v_ref[...]
    lane = lax.broadcasted_iota(jnp.int32, (C, LANES), 1)
    causal = (lax.broadcasted_iota(jnp.int32, (C, C), 0) >= lax.broadcasted_iota(jnp.int32, (C, C), 1))
    head_lanes = (lane < GLA_HEAD_K, lane >= GLA_HEAD_K)
    for c in range(R // C):
        rs = slice(c * C, (c + 1) * C)
        for p in range(2):
            ls = slice(p * LANES, (p + 1) * LANES)
            b = bcum[rs, ls]
            b_last = b[C - 1:C, :]
            kk = k[rs, ls]
            q_dec = q[rs, ls] * jnp.exp(b)
            k_dec = kk * jnp.exp(-b)
            k_last = kk * jnp.exp(b_last - b)
            decay = jnp.exp(b_last)
            for hh in range(2):
                h = 2 * p + hh
                m = head_lanes[hh]
                qd = jnp.where(m, q_dec, 0.0).astype(BF16)
                kd = jnp.where(m, k_dec, 0.0).astype(BF16)
                kl = jnp.where(m, k_last, 0.0).astype(BF16)
                vh = v[rs, h * HEAD_DIM:(h + 1) * HEAD_DIM]
                attn = jnp.where(causal, _dot_t(qd, kd), 0.0)
                st = state_ref[h]
                o = _dot(attn.astype(BF16), vh.astype(BF16)) + _dot_t(qd, st.astype(BF16))
                acc_ref[rs, h * HEAD_DIM:(h + 1) * HEAD_DIM] = o
                state_ref[h] = decay * st + _dot(vh.T.astype(BF16), kl)

    return _head_norm_gate(acc_ref[...], z_ref[...], g_ref[...])


def _lru_mix(x_ref, z_ref, cw_ref, cb_ref, wa_ref, ba_ref, wx_ref, bx_ref, lam_ref, g_ref,
             hist_ref, h_ref, acc_ref):
    R = x_ref.shape[0]
    HIST = hist_ref.shape[0]

    x = x_ref[...]
    xe = jnp.concatenate([hist_ref[...], x], axis=0)
    xc = cb_ref[...] + cw_ref[LRU_CONV - 1:LRU_CONV, :] * x
    for kk in range(LRU_CONV - 1):
        sh = LRU_CONV - 1 - kk
        xc = xc + cw_ref[kk:kk + 1, :] * pltpu.roll(xe, sh, 0)[HIST:HIST + R]
    hist_ref[...] = x[R - HIST:R]

    xb = xc.astype(BF16)
    ra, rx = [], []
    for h in range(N_HEADS):
        xh = xb[:, h * HEAD_DIM:(h + 1) * HEAD_DIM]
        ra.append(_dot(xh, wa_ref[h]))
        rx.append(_dot(xh, wx_ref[h]))
    r = jax.nn.sigmoid(jnp.concatenate(ra, axis=-1) + ba_ref[...])
    ig = jax.nn.sigmoid(jnp.concatenate(rx, axis=-1) + bx_ref[...])
    log_a = -LRU_C * r * _softplus(-lam_ref[...])
    a = jnp.exp(log_a)
    u = jnp.sqrt(-_expm1(2.0 * log_a)) * (ig * xc)

    G = R // SUBLANES
    a3 = a.reshape(G, SUBLANES, W_BR)
    u3 = u.reshape(G, SUBLANES, W_BR)
    row = lax.broadcasted_iota(jnp.int32, a3.shape, 1)
    d = 1
    while d < SUBLANES:
        keep = row >= d
        u3 = jnp.where(keep, a3 * pltpu.roll(u3, d, 1) + u3, u3)
        a3 = jnp.where(keep, a3 * pltpu.roll(a3, d, 1), a3)
        d *= 2
    carry = h_ref[...]
    for gi in range(G):
        hg = a3[gi] * carry + u3[gi]
        acc_ref[gi * SUBLANES:(gi + 1) * SUBLANES, :] = hg
        carry = hg[SUBLANES - 1:SUBLANES, :]
    h_ref[...] = carry
    return _head_norm_gate(acc_ref[...], z_ref[...], g_ref[...])


def _conf_mix(v_ref, glu_ref, z_ref, dw_ref, db_ref, lg_ref, lb_ref, pw_ref, pb_ref, g_ref, hist_ref):
    R = v_ref.shape[0]
    HIST = hist_ref.shape[0]

    y = v_ref[...] * jax.nn.sigmoid(glu_ref[...])
    ye = jnp.concatenate([hist_ref[...], y], axis=0)
    acc = db_ref[...]
    for r in range(SUBLANES):
        yr = ye if r == 0 else pltpu.roll(ye, r, 0)
        for q in range((CONF_KERNEL - 1 - r) // SUBLANES + 1):
            kk = CONF_KERNEL - 1 - (SUBLANES * q + r)
            start = HIST - SUBLANES * q
            acc = acc + dw_ref[kk:kk + 1, :] * yr[start:start + R]
    hist_ref[...] = y[R - HIST:R]

    mu = jnp.mean(acc, axis=-1, keepdims=True)
    xc = acc - mu
    var = jnp.mean(xc * xc, axis=-1, keepdims=True)
    yn = xc * lax.rsqrt(var + EPS) * lg_ref[...] + lb_ref[...]
    o = _dot(_silu(yn).astype(BF16), pw_ref[...]) + pb_ref[...]
    return _head_norm_gate(o, z_ref[...], g_ref[...])


def _cmp_kernel(kc_ref, vc_ref, posk_ref, w1k_ref, w2k_ref, posv_ref, w1v_ref, w2v_ref, ko_ref, vo_ref):
    S = kc_ref.shape[0]
    NB = S // CMP_STRIDE
    HALF = CMP_LEN // 2

    def compress(z_ref, pos_ref, w1_ref, w2_ref):
        lo = jnp.zeros((NB, w1_ref.shape[2]), F32)
        hi = jnp.zeros((NB, w1_ref.shape[2]), F32)
        for l in range(HALF):
            grp = z_ref[pl.ds(l, NB, stride=CMP_STRIDE), :]
            lo = lo + _dot((grp + pos_ref[l:l + 1, :]).astype(BF16), w1_ref[l])
            hi = hi + _dot((grp + pos_ref[HALF + l:HALF + l + 1, :]).astype(BF16), w1_ref[HALF + l])
        hid = lo + pltpu.roll(hi, NB - 1, 0)
        blk = lax.broadcasted_iota(jnp.int32, hid.shape, 0)
        hid = jnp.where(blk < NB - 1, hid, 0.0)
        return _dot(_silu(hid).astype(BF16), w2_ref[...])

    ko_ref[...] = compress(kc_ref, posk_ref, w1k_ref, w2k_ref)
    vo_ref[...] = compress(vc_ref, posv_ref, w1v_ref, w2v_ref)


def _nsa_compress(proj, posk, w1k, w2k, posv, w1v, w2v, l, B, S):
    NB = S // CMP_STRIDE

    def seg(name):
        off, wp = _DST[name]
        return pl.BlockSpec((S, wp), lambda b: (b, off // wp))

    out = jax.ShapeDtypeStruct((B, NB, HEAD_DIM), F32)
    ospec = pl.BlockSpec((None, NB, HEAD_DIM), lambda b: (b, 0, 0))
    return pl.pallas_call(
        _cmp_kernel,
        out_shape=(out, out),
        grid=(B,),
        in_specs=[seg('nsa_kc'), seg('nsa_vc'), _layer_spec(posk, l), _layer_spec(w1k, l),
                  _layer_spec(w2k, l), _layer_spec(posv, l), _layer_spec(w1v, l), _layer_spec(w2v, l)],
        out_specs=(ospec, ospec),
        compiler_params=_params("parallel"),
        name="nsa_compress",
    )(proj, proj, posk, w1k, w2k, posv, w1v, w2v)


def _nsa_kernel(q_ref, gt_ref, z_ref, kc_ref, vc_ref, ks_ref, vs_ref, kw_ref, vw_ref, g_ref, o_ref,
                m_ref, acc_ref):
    TQ = q_ref.shape[0]
    S = ks_ref.shape[0]
    NB = kc_ref.shape[0]
    H, Dh = N_HEADS, HEAD_DIM
    scale = Dh ** -0.5
    i = pl.program_id(1)
    t0 = i * TQ

    q = q_ref[...]
    q4 = jnp.concatenate([q[:, h * Dh:(h + 1) * Dh] for h in range(H)], axis=0).astype(BF16)

    t_c = t0 + lax.broadcasted_iota(jnp.int32, (TQ, NB), 0)
    n_c = lax.broadcasted_iota(jnp.int32, (TQ, NB), 1)
    cmask = t_c >= n_c * CMP_STRIDE + (CMP_LEN - 1)
    cmask4 = jnp.concatenate([cmask] * H, axis=0)
    p4 = _masked_softmax(_dot_t(q4, kc_ref[...].astype(BF16)) * scale, cmask4)
    p4b = p4.astype(BF16)
    o_cmp4 = _dot(p4b, vc_ref[...].astype(BF16))

    n_sel = S // SLC_LEN
    jj = lax.broadcasted_iota(jnp.int32, (n_sel, NB), 0)
    nn = lax.broadcasted_iota(jnp.int32, (n_sel, NB), 1)
    ov = ((nn * CMP_STRIDE < (jj + 1) * SLC_LEN) & (nn * CMP_STRIDE + CMP_LEN > jj * SLC_LEN)
          & (nn < NB - 1))
    po = _dot_t(jnp.where(ov, 1.0, 0.0).astype(BF16), p4b)
    imp = po[:, 0:TQ]
    for h in range(1, H):
        imp = imp + po[:, h * TQ:(h + 1) * TQ]

    j = lax.broadcasted_iota(jnp.int32, (n_sel, TQ), 0)
    cur = (t0 + lax.broadcasted_iota(jnp.int32, (n_sel, TQ), 1)) // SLC_LEN
    forced = (j == 0) | (j == cur) | (j == cur - 1)
    val = jnp.where(j > cur, -jnp.inf, jnp.where(forced, FORCED_SCORE, imp))
    rank = jnp.zeros((n_sel, TQ), F32)
    for ii in range(n_sel):
        vi = val[ii:ii + 1, :]
        beats = jnp.where(vi > val, 1.0, jnp.where(vi == val, jnp.where(j > ii, 1.0, 0.0), 0.0))
        rank = rank + beats
    sel_t = jnp.where(rank < SLC_TOPN, 1.0, 0.0)
    sel = jnp.concatenate([sel_t, jnp.zeros((LANES - n_sel, TQ), F32)], axis=0).T.astype(BF16)

    t_k = t0 + lax.broadcasted_iota(jnp.int32, (TQ, TQ), 0)
    k_i = lax.broadcasted_iota(jnp.int32, (TQ, TQ), 1)
    ones_half = jnp.ones((TQ, Dh), BF16)
    neg_inf = jnp.full((TQ, TQ), -jnp.inf, F32)
    scale_log2e = scale * LOG2_E

    def reset():
        m_ref[...] = jnp.full(m_ref.shape, -jnp.inf, F32)
        acc_ref[...] = jnp.zeros(acc_ref.shape, F32)

    def attend_chunk(k_ref, v_ref, c, bias):
        k0 = pl.multiple_of(c * TQ, TQ)
        kb = k_ref[pl.ds(k0, TQ), :].astype(BF16)
        vb = jnp.concatenate([v_ref[pl.ds(k0, TQ), :].astype(BF16), ones_half], axis=-1)
        s4 = _dot_t(q4, kb)
        for h in range(H):
            rs = slice(h * TQ, (h + 1) * TQ)
            s = s4[rs] if bias is None else s4[rs] + bias
            m_old = m_ref[rs]
            m_new = jnp.maximum(m_old, jnp.max(s, axis=-1, keepdims=True))
            m_safe = jnp.where(m_new == -jnp.inf, 0.0, m_new)
            p = jnp.exp2((s - jnp.concatenate([m_safe] * (TQ // LANES), axis=-1)) * scale_log2e)
            alpha = jnp.exp2((m_old - m_safe) * scale_log2e)
            acc_ref[rs] = jnp.concatenate([alpha, alpha], axis=-1) * acc_ref[rs] + _dot(p.astype(BF16), vb)
            m_ref[rs] = m_new

    def finish():
        acc = acc_ref[...]
        return acc[:, :Dh] / jnp.maximum(acc[:, Dh:], 1e-30)

    eb = lax.broadcasted_iota(jnp.int32, (LANES, TQ), 0)
    ek = lax.broadcasted_iota(jnp.int32, (LANES, TQ), 1) // SLC_LEN

    def sel_keys(c):
        expand = jnp.where(eb == ek + c * (TQ // SLC_LEN), 1.0, 0.0).astype(BF16)
        return _dot(sel, expand) > 0.5

    reset()

    def slc_body(c, carry):
        attend_chunk(ks_ref, vs_ref, c, jnp.where(sel_keys(c), 0.0, neg_inf))
        return carry

    lax.fori_loop(0, i, slc_body, 0)
    attend_chunk(ks_ref, vs_ref, i, jnp.where(sel_keys(i) & (t0 + k_i <= t_k), 0.0, neg_inf))
    o_slc4 = finish()

    reset()
    n_back = WIN // TQ
    for back in range(n_back, -1, -1):
        def win_chunk(back=back):
            c = i - back
            if 0 < back < n_back:
                bias = None
            else:
                dist = t_k - (c * TQ + k_i)
                bias = jnp.where((dist >= 0) & (dist < WIN), 0.0, neg_inf)
            attend_chunk(kw_ref, vw_ref, c, bias)
        if back == 0:
            win_chunk()
        else:
            pl.when(i >= back)(win_chunk)
    o_win4 = finish()

    gs = jax.nn.sigmoid(gt_ref[...])
    outs = []
    for h in range(H):
        rs = slice(h * TQ, (h + 1) * TQ)
        outs.append(gs[:, 3 * h:3 * h + 1] * o_cmp4[rs] + gs[:, 3 * h + 1:3 * h + 2] * o_slc4[rs]
                    + gs[:, 3 * h + 2:3 * h + 3] * o_win4[rs])
    o_ref[...] = _head_norm_gate(jnp.concatenate(outs, axis=-1), z_ref[...], g_ref[...])


def _nsa(proj, k_cmp, v_cmp, bg, l, B, S, TQ=256):
    assert WIN % TQ == 0 and TQ % SLC_LEN == 0 and TQ % LANES == 0
    nt = S // TQ
    T = B * S
    NB = k_cmp.shape[1]

    def seq(name):
        off, wp = _DST[name]
        return pl.BlockSpec((S, wp), lambda b, i: (b, off // wp))

    cspec = pl.BlockSpec((None, NB, HEAD_DIM), lambda b, i: (b, 0, 0))
    return pl.pallas_call(
        _nsa_kernel,
        out_shape=jax.ShapeDtypeStruct((T, W_BR), BF16),
        grid=(B, nt),
        in_specs=[_seg_spec('nsa_q', TQ, nt), _seg_spec('nsa_g', TQ, nt), _seg_spec('nsa_z', TQ, nt),
                  cspec, cspec, seq('nsa_ks'), seq('nsa_vs'), seq('nsa_kw'), seq('nsa_vw'),
                  _layer_spec(bg, l, 2)],
        out_specs=pl.BlockSpec((TQ, W_BR), lambda b, i: (b * nt + i, 0)),
        scratch_shapes=[pltpu.VMEM((N_HEADS * TQ, LANES), F32),
                        pltpu.VMEM((N_HEADS * TQ, 2 * HEAD_DIM), F32)],
        compiler_params=_params("parallel", "arbitrary"),
        name="nsa_attention",
    )(proj, proj, proj, k_cmp, v_cmp, proj, proj, proj, proj, bg)


def kernel(x, pre_norm_g, w_in, gla_w_fg2, gla_b_fg2, lru_conv_w, lru_conv_b, lru_w_a, lru_b_a, lru_w_x,
           lru_b_x, lru_lambda, nsa_cmp_pos_k, nsa_cmp_w1_k, nsa_cmp_w2_k, nsa_cmp_pos_v, nsa_cmp_w1_v,
           nsa_cmp_w2_v, conf_dw_w, conf_dw_b, conf_ln_g, conf_ln_b, conf_pw_w, conf_pw_b, branch_norm_g,
           w_out, post_norm_g):
    B, S, D = x.shape
    L = w_in.shape[0]
    T = B * S
    x2 = x.reshape(T, D)

    def rows(a):
        return a.reshape(L, 1, -1)

    w_in_t = _relayout_w_in(w_in)
    lowrank = gla_w_fg2.shape[1]
    w1k = nsa_cmp_w1_k.reshape(L, CMP_LEN, HEAD_DIM, -1).astype(BF16)
    w1v = nsa_cmp_w1_v.reshape(L, CMP_LEN, HEAD_DIM, -1).astype(BF16)
    w2k = nsa_cmp_w2_k.astype(BF16)
    w2v = nsa_cmp_w2_v.astype(BF16)
    bg = branch_norm_g.reshape(L, N_MIXERS, 1, W_BR)
    p = dict(
        bg=bg, pre_g=rows(pre_norm_g), post_g=rows(post_norm_g), w_out=w_out.astype(BF16),
        wfg=jnp.pad(gla_w_fg2, ((0, 0), (0, LANES - lowrank), (0, 0))).astype(BF16), bfg=rows(gla_b_fg2),
        lru_cw=lru_conv_w, lru_cb=rows(lru_conv_b), wa=lru_w_a.astype(BF16), lru_ba=rows(lru_b_a),
        wx=lru_w_x.astype(BF16), lru_bx=rows(lru_b_x), lru_lam=rows(lru_lambda),
        conf_dw=conf_dw_w, conf_db=rows(conf_dw_b), conf_lg=rows(conf_ln_g), conf_lb=rows(conf_ln_b),
        pw=conf_pw_w.astype(BF16), conf_pb=rows(conf_pw_b))

    hn = _prenorm(x2, p['pre_g'], 0)
    for l in range(L):
        proj = _in_proj(hn, w_in_t, l)
        k_cmp, v_cmp = _nsa_compress(proj, nsa_cmp_pos_k, w1k, w2k, nsa_cmp_pos_v, w1v, w2v, l, B, S)
        m_c = _nsa(proj, k_cmp, v_cmp, bg, l, B, S)
        if l + 1 < L:
            x2, hn = _mix_out(proj, m_c, x2, p, l, B, S, emit_hn=True)
        else:
            x2 = _mix_out(proj, m_c, x2, p, l, B, S, emit_hn=False)
    return x2.reshape(B, S, D)
```

```python
import functools

import jax
import jax.numpy as jnp
from jax import lax
from jax.experimental import pallas as pl
from jax.experimental.pallas import tpu as pltpu

F32 = jnp.float32
BF16 = jnp.bfloat16

D_MODEL = 2048
N_HEADS = 4
HEAD_DIM = 128
W_BR = N_HEADS * HEAD_DIM
N_MIXERS = 4
GLA_HEAD_K = 64
GLA_TAU = 16.0
GLA_CHUNK = 64
LRU_C = 8.0
LRU_CONV = 4
CMP_LEN = 32
CMP_STRIDE = 16
SLC_LEN = 64
SLC_TOPN = 16
WIN = 512
FORCED_SCORE = 1e3
CONF_KERNEL = 31
CONF_HIST = 32
EPS = 1e-6
LOG2_E = 1.4426950408889634
SUBLANES = 8
LANES = 128
MXU_COLS = 256
VMEM_LIMIT = 48 * 1024 * 1024
VMEM_LIMIT_MIX = 57 * 1024 * 1024

_SEGS = (('gla_q', 256), ('gla_k', 256), ('gla_v', 512), ('gla_fg', 16), ('gla_z', 512),
         ('lru_x', 512), ('lru_z', 512),
         ('nsa_q', 512), ('nsa_kc', 128), ('nsa_vc', 128), ('nsa_ks', 128), ('nsa_vs', 128),
         ('nsa_kw', 128), ('nsa_vw', 128), ('nsa_g', 12), ('nsa_z', 512),
         ('conv_v', 512), ('conv_glu', 512), ('conv_z', 512))
_ORDER_NSA = ('nsa_q', 'nsa_z', 'nsa_kc', 'nsa_vc', 'nsa_ks', 'nsa_vs', 'nsa_kw', 'nsa_vw', 'nsa_g')
_ORDER_MIX = ('gla_v', 'gla_z', 'lru_x', 'lru_z', 'conv_v', 'conv_glu', 'conv_z', 'gla_q', 'gla_k', 'gla_fg')


def _source_layout():
    src, off = {}, 0
    for name, w in _SEGS:
        src[name] = (off, w)
        off += w
    return src


def _padded_layout(order, src):
    dst, off = {}, 0
    for name in order:
        w = src[name][1]
        wp = -(-w // LANES) * LANES
        assert off % wp == 0
        dst[name] = (off, wp)
        off += wp
    return dst, off


_SRC = _source_layout()
_DST, N_PROJ_NSA = _padded_layout(_ORDER_NSA, _SRC)
_DST_MIX, N_PROJ_MIX = _padded_layout(_ORDER_MIX, _SRC)


def _relayout_w_in(w_in):
    wt = jnp.transpose(w_in, (2, 0, 1))

    def gather(order, dst):
        parts = []
        for name in order:
            o, w = _SRC[name]
            wp = dst[name][1]
            p = wt[o:o + w]
            if wp != w:
                p = jnp.pad(p, ((0, wp - w), (0, 0), (0, 0)))
            parts.append(p)
        return jnp.transpose(jnp.concatenate(parts, axis=0).astype(BF16), (1, 0, 2))

    return gather(_ORDER_NSA, _DST), gather(_ORDER_MIX, _DST_MIX)


def _dot(a, b):
    return jnp.dot(a, b, preferred_element_type=F32)


def _dot_t(a, b):
    return lax.dot_general(a, b, (((1,), (1,)), ((), ())), preferred_element_type=F32)


def _softplus(x):
    return jnp.maximum(x, 0.0) + jnp.log1p(jnp.exp(-jnp.abs(x)))


def _expm1(x):
    return jnp.tanh(0.5 * x) * (jnp.exp(x) + 1.0)


def _silu(x):
    return x * jax.nn.sigmoid(x)


def _rmsnorm(x, g):
    return x * lax.rsqrt(jnp.mean(x * x, axis=-1, keepdims=True) + EPS) * g


def _masked_softmax(s, mask):
    s = jnp.where(mask, s, -jnp.inf)
    m = jnp.max(s, axis=-1, keepdims=True)
    m = jnp.where(jnp.isfinite(m), m, 0.0)
    p = jnp.exp(s - m)
    return p / jnp.maximum(jnp.sum(p, axis=-1, keepdims=True), 1e-30)


def _head_norm_gate(o, z, g):
    outs = []
    for h in range(N_HEADS):
        oh = o[:, h * HEAD_DIM:(h + 1) * HEAD_DIM]
        outs.append(oh * lax.rsqrt(jnp.mean(oh * oh, axis=-1, keepdims=True) + EPS))
    on = jnp.concatenate(outs, axis=-1) * g
    return (on * _silu(z)).astype(BF16)


def _layer_spec(arr, *lead):
    rest = arr.shape[len(lead):]
    idx = tuple(lead) + (0,) * len(rest)
    return pl.BlockSpec((None,) * len(lead) + rest, lambda *_: idx)


def _seg_spec(name, rows, nt):
    off, wp = _DST[name]
    cb = off // wp
    return pl.BlockSpec((rows, wp), lambda b, i: (b * nt + i, cb))


def _params(*sem):
    return pltpu.CompilerParams(dimension_semantics=sem, vmem_limit_bytes=VMEM_LIMIT)


def _prenorm_kernel(x_ref, g_ref, o_ref):
    o_ref[...] = _rmsnorm(x_ref[...], g_ref[...]).astype(BF16)


def _prenorm(x2, g, l, tm=512):
    T, D = x2.shape
    return pl.pallas_call(
        _prenorm_kernel,
        out_shape=jax.ShapeDtypeStruct((T, D), BF16),
        grid=(T // tm,),
        in_specs=[pl.BlockSpec((tm, D), lambda i: (i, 0)), _layer_spec(g, l)],
        out_specs=pl.BlockSpec((tm, D), lambda i: (i, 0)),
        compiler_params=_params("parallel"),
        name="prenorm",
    )(x2, g)


def _in_proj_kernel(h_ref, wt_ref, o_ref):
    o_ref[...] = _dot_t(h_ref[...], wt_ref[...])


def _in_proj(hn, wt, l, tm=1024):
    T, D = hn.shape
    NP = wt.shape[1]
    return pl.pallas_call(
        _in_proj_kernel,
        out_shape=jax.ShapeDtypeStruct((T, NP), F32),
        grid=(T // tm,),
        in_specs=[pl.BlockSpec((tm, D), lambda i: (i, 0)), _layer_spec(wt, l)],
        out_specs=pl.BlockSpec((tm, NP), lambda i: (i, 0)),
        compiler_params=_params("parallel"),
        name="in_proj",
    )(hn, wt)


N_GLA_W, N_LRU_W, N_CONF_W = 3, 8, 7


def _mix_out_kernel(*refs, emit_hn):
    refs = list(refs)
    hn0_ref, hn_next_ref, wt_ref = [refs.pop(0) for _ in range(3)]
    gla_w = [refs.pop(0) for _ in range(N_GLA_W)]
    lru_w = [refs.pop(0) for _ in range(N_LRU_W)]
    conf_w = [refs.pop(0) for _ in range(N_CONF_W)]
    mc_ref, w_ref, x_ref, pg_ref = [refs.pop(0) for _ in range(4)]
    gn_ref = refs.pop(0) if emit_hn else None
    o_ref = refs.pop(0)
    hn_ref = refs.pop(0) if emit_hn else None
    proj_even, proj_odd, gla_state, gla_acc, lru_hist, lru_h, lru_acc, conf_hist = refs
    i = pl.program_id(1)

    @pl.when(i == 0)
    def _():
        for ref in (gla_state, lru_hist, lru_h, conf_hist):
            ref[...] = jnp.zeros(ref.shape, F32)
        proj_even[...] = _dot_t(hn0_ref[...], wt_ref[...])

    def w_rows(m):
        return w_ref[m * W_BR:(m + 1) * W_BR, :]

    def step(proj_cur, proj_next):
        col_blocks = list(range(0, wt_ref.shape[0], MXU_COLS))

        def beside():
            if col_blocks:
                c0 = col_blocks.pop(0)
                c1 = min(c0 + MXU_COLS, wt_ref.shape[0])
                proj_next[:, c0:c1] = _dot_t(hn_next_ref[...], wt_ref[c0:c1, :])

        def seg(*names):
            return [proj_cur.at[:, _DST_MIX[n][0]:_DST_MIX[n][0] + _DST_MIX[n][1]] for n in names]

        y = _dot(mc_ref[...], w_rows(2))
        m_a = _gla_mix(*seg('gla_q', 'gla_k', 'gla_v', 'gla_fg', 'gla_z'), *gla_w, gla_state, gla_acc)
        y = y + _dot(m_a, w_rows(0))
        m_b = _lru_mix(*seg('lru_x', 'lru_z'), *lru_w, lru_hist, lru_h, lru_acc, beside)
        y = y + _dot(m_b, w_rows(1))
        m_d = _conf_mix(*seg('conv_v', 'conv_glu', 'conv_z'), *conf_w, conf_hist, beside)
        y = y + _dot(m_d, w_rows(3))
        while col_blocks:
            beside()
        x_new = x_ref[...] + _rmsnorm(y, pg_ref[...])
        o_ref[...] = x_new
        if emit_hn:
            hn_ref[...] = _rmsnorm(x_new, gn_ref[...]).astype(BF16)

    pl.when(i % 2 == 0)(functools.partial(step, proj_even, proj_odd))
    pl.when(i % 2 == 1)(functools.partial(step, proj_odd, proj_even))


def _mix_out(hn, wt, m_c, x2, p, l, B, S, emit_hn, R=256):
    nt = S // R
    T, D = x2.shape
    bg = p['bg']

    def tile(width):
        return pl.BlockSpec((R, width), lambda b, i: (b * nt + i, 0))

    gla_names = ('wfg', 'bfg')
    lru_names = ('lru_cw', 'lru_cb', 'wa', 'lru_ba', 'wx', 'lru_bx', 'lru_lam')
    conf_names = ('conf_dw', 'conf_db', 'conf_lg', 'conf_lb', 'pw', 'conf_pb')
    assert (len(gla_names), len(lru_names), len(conf_names)) == (N_GLA_W - 1, N_LRU_W - 1, N_CONF_W - 1)
    in_specs = [pl.BlockSpec((R, D), lambda b, i: (b * nt, 0)),
                pl.BlockSpec((R, D), lambda b, i: (b * nt + jnp.minimum(i + 1, nt - 1), 0)),
                _layer_spec(wt, l)]
    args = [hn, hn, wt]
    for names, mixer in ((gla_names, 0), (lru_names, 1), (conf_names, 3)):
        in_specs += [_layer_spec(p[n], l) for n in names] + [_layer_spec(bg, l, mixer)]
        args += [p[n] for n in names] + [bg]
    in_specs += [tile(W_BR), _layer_spec(p['w_out'], l), tile(D), _layer_spec(p['post_g'], l)]
    args += [m_c, p['w_out'], x2, p['post_g']]
    out_shape = jax.ShapeDtypeStruct((T, D), F32)
    out_specs = tile(D)
    if emit_hn:
        in_specs.append(_layer_spec(p['pre_g'], l + 1))
        args.append(p['pre_g'])
        out_shape = (out_shape, jax.ShapeDtypeStruct((T, D), BF16))
        out_specs = (tile(D), tile(D))
    return pl.pallas_call(
        functools.partial(_mix_out_kernel, emit_hn=emit_hn),
        out_shape=out_shape,
        grid=(B, nt),
        in_specs=in_specs,
        out_specs=out_specs,
        scratch_shapes=[pltpu.VMEM((R, N_PROJ_MIX), F32), pltpu.VMEM((R, N_PROJ_MIX), F32),
                        pltpu.VMEM((N_HEADS, HEAD_DIM, LANES), F32), pltpu.VMEM((R, W_BR), F32),
                        pltpu.VMEM((SUBLANES, W_BR), F32), pltpu.VMEM((1, W_BR), F32),
                        pltpu.VMEM((R, W_BR), F32), pltpu.VMEM((CONF_HIST, W_BR), F32)],
        compiler_params=pltpu.CompilerParams(dimension_semantics=("parallel", "arbitrary"),
                                             vmem_limit_bytes=VMEM_LIMIT_MIX),
        name="mix_out",
    )(*args)


def _gla_mix(q_ref, k_ref, v_ref, fg_ref, z_ref, wfg_ref, bfg_ref, g_ref, state_ref, acc_ref):
    R = q_ref.shape[0]
    C = GLA_CHUNK

    pre = _dot(fg_ref[...].astype(BF16), wfg_ref[...]) + bfg_ref[...]
    log_f = -_softplus(-pre) * (1.0 / GLA_TAU)
    row = lax.broadcasted_iota(jnp.int32, log_f.shape, 0) % C
    bcum = log_f
    d = 1
    while d < C:
        bcum = bcum + jnp.where(row >= d, pltpu.roll(bcum, d, 0), 0.0)
        d *= 2

    q = q_ref[...] * (GLA_HEAD_K ** -0.5)
    k = k_ref[...]
    v = v_ref[...]
    lane = lax.broadcasted_iota(jnp.int32, (C, LANES), 1)
    causal = (lax.broadcasted_iota(jnp.int32, (C, C), 0) >= lax.broadcasted_iota(jnp.int32, (C, C), 1))
    head_lanes = (lane < GLA_HEAD_K, lane >= GLA_HEAD_K)
    for c in range(R // C):
        rs = slice(c * C, (c + 1) * C)
        for p in range(2):
            ls = slice(p * LANES, (p + 1) * LANES)
            b = bcum[rs, ls]
            b_last = b[C - 1:C, :]
            kk = k[rs, ls]
            q_dec = q[rs, ls] * jnp.exp(b)
            k_dec = kk * jnp.exp(-b)
            k_last = kk * jnp.exp(b_last - b)
            decay = jnp.exp(b_last)
            for hh in range(2):
                h = 2 * p + hh
                m = head_lanes[hh]
                qd = jnp.where(m, q_dec, 0.0).astype(BF16)
                kd = jnp.where(m, k_dec, 0.0).astype(BF16)
                kl = jnp.where(m, k_last, 0.0).astype(BF16)
                vh = v[rs, h * HEAD_DIM:(h + 1) * HEAD_DIM]
                attn = jnp.where(causal, _dot_t(qd, kd), 0.0)
                st = state_ref[h]
                o = _dot(attn.astype(BF16), vh.astype(BF16)) + _dot_t(qd, st.astype(BF16))
                acc_ref[rs, h * HEAD_DIM:(h + 1) * HEAD_DIM] = o
                state_ref[h] = decay * st + _dot(vh.T.astype(BF16), kl)

    return _head_norm_gate(acc_ref[...], z_ref[...], g_ref[...])


def _lru_mix(x_ref, z_ref, cw_ref, cb_ref, wa_ref, ba_ref, wx_ref, bx_ref, lam_ref, g_ref,
             hist_ref, h_ref, acc_ref, beside):
    R = x_ref.shape[0]
    HIST = hist_ref.shape[0]

    x = x_ref[...]
    xe = jnp.concatenate([hist_ref[...], x], axis=0)
    xc = cb_ref[...] + cw_ref[LRU_CONV - 1:LRU_CONV, :] * x
    for kk in range(LRU_CONV - 1):
        sh = LRU_CONV - 1 - kk
        xc = xc + cw_ref[kk:kk + 1, :] * pltpu.roll(xe, sh, 0)[HIST:HIST + R]
    hist_ref[...] = x[R - HIST:R]

    beside()
    xb = xc.astype(BF16)
    ra, rx = [], []
    for h in range(N_HEADS):
        xh = xb[:, h * HEAD_DIM:(h + 1) * HEAD_DIM]
        ra.append(_dot(xh, wa_ref[h]))
        rx.append(_dot(xh, wx_ref[h]))
    r = jax.nn.sigmoid(jnp.concatenate(ra, axis=-1) + ba_ref[...])
    ig = jax.nn.sigmoid(jnp.concatenate(rx, axis=-1) + bx_ref[...])
    log_a = -LRU_C * r * _softplus(-lam_ref[...])
    a = jnp.exp(log_a)
    u = jnp.sqrt(-_expm1(2.0 * log_a)) * (ig * xc)

    beside()
    G = R // SUBLANES
    a3 = a.reshape(G, SUBLANES, W_BR)
    u3 = u.reshape(G, SUBLANES, W_BR)
    row = lax.broadcasted_iota(jnp.int32, a3.shape, 1)
    d = 1
    while d < SUBLANES:
        keep = row >= d
        u3 = jnp.where(keep, a3 * pltpu.roll(u3, d, 1) + u3, u3)
        a3 = jnp.where(keep, a3 * pltpu.roll(a3, d, 1), a3)
        d *= 2
    beside()
    carry = h_ref[...]
    for gi in range(G):
        hg = a3[gi] * carry + u3[gi]
        acc_ref[gi * SUBLANES:(gi + 1) * SUBLANES, :] = hg
        carry = hg[SUBLANES - 1:SUBLANES, :]
    h_ref[...] = carry
    beside()
    return _head_norm_gate(acc_ref[...], z_ref[...], g_ref[...])


def _conf_mix(v_ref, glu_ref, z_ref, dw_ref, db_ref, lg_ref, lb_ref, pw_ref, pb_ref, g_ref, hist_ref, beside):
    R = v_ref.shape[0]
    HIST = hist_ref.shape[0]

    y = v_ref[...] * jax.nn.sigmoid(glu_ref[...])
    ye = jnp.concatenate([hist_ref[...], y], axis=0)
    acc = db_ref[...]
    for r in range(SUBLANES):
        yr = ye if r == 0 else pltpu.roll(ye, r, 0)
        for q in range((CONF_KERNEL - 1 - r) // SUBLANES + 1):
            kk = CONF_KERNEL - 1 - (SUBLANES * q + r)
            start = HIST - SUBLANES * q
            acc = acc + dw_ref[kk:kk + 1, :] * yr[start:start + R]
        beside()
    hist_ref[...] = y[R - HIST:R]

    mu = jnp.mean(acc, axis=-1, keepdims=True)
    xc = acc - mu
    var = jnp.mean(xc * xc, axis=-1, keepdims=True)
    yn = xc * lax.rsqrt(var + EPS) * lg_ref[...] + lb_ref[...]
    o = _dot(_silu(yn).astype(BF16), pw_ref[...]) + pb_ref[...]
    return _head_norm_gate(o, z_ref[...], g_ref[...])


def _cmp_kernel(kc_ref, vc_ref, posk_ref, w1k_ref, w2k_ref, posv_ref, w1v_ref, w2v_ref, ko_ref, vo_ref):
    S = kc_ref.shape[0]
    NB = S // CMP_STRIDE
    HALF = CMP_LEN // 2

    def compress(z_ref, pos_ref, w1_ref, w2_ref):
        lo = jnp.zeros((NB, w1_ref.shape[2]), F32)
        hi = jnp.zeros((NB, w1_ref.shape[2]), F32)
        for l in range(HALF):
            grp = z_ref[pl.ds(l, NB, stride=CMP_STRIDE), :]
            lo = lo + _dot((grp + pos_ref[l:l + 1, :]).astype(BF16), w1_ref[l])
            hi = hi + _dot((grp + pos_ref[HALF + l:HALF + l + 1, :]).astype(BF16), w1_ref[HALF + l])
        hid = lo + pltpu.roll(hi, NB - 1, 0)
        blk = lax.broadcasted_iota(jnp.int32, hid.shape, 0)
        hid = jnp.where(blk < NB - 1, hid, 0.0)
        return _dot(_silu(hid).astype(BF16), w2_ref[...])

    ko_ref[...] = compress(kc_ref, posk_ref, w1k_ref, w2k_ref)
    vo_ref[...] = compress(vc_ref, posv_ref, w1v_ref, w2v_ref)


def _nsa_compress(proj, posk, w1k, w2k, posv, w1v, w2v, l, B, S):
    NB = S // CMP_STRIDE

    def seg(name):
        off, wp = _DST[name]
        return pl.BlockSpec((S, wp), lambda b: (b, off // wp))

    out = jax.ShapeDtypeStruct((B, NB, HEAD_DIM), F32)
    ospec = pl.BlockSpec((None, NB, HEAD_DIM), lambda b: (b, 0, 0))
    return pl.pallas_call(
        _cmp_kernel,
        out_shape=(out, out),
        grid=(B,),
        in_specs=[seg('nsa_kc'), seg('nsa_vc'), _layer_spec(posk, l), _layer_spec(w1k, l),
                  _layer_spec(w2k, l), _layer_spec(posv, l), _layer_spec(w1v, l), _layer_spec(w2v, l)],
        out_specs=(ospec, ospec),
        compiler_params=_params("parallel"),
        name="nsa_compress",
    )(proj, proj, posk, w1k, w2k, posv, w1v, w2v)


def _nsa_kernel(q_ref, gt_ref, z_ref, kc_ref, vc_ref, ks_ref, vs_ref, kw_ref, vw_ref, g_ref, o_ref,
                m_ref, acc_ref):
    TQ = q_ref.shape[0]
    S = ks_ref.shape[0]
    NB = kc_ref.shape[0]
    H, Dh = N_HEADS, HEAD_DIM
    scale = Dh ** -0.5
    i = pl.program_id(1)
    t0 = i * TQ

    q = q_ref[...]
    q4 = jnp.concatenate([q[:, h * Dh:(h + 1) * Dh] for h in range(H)], axis=0).astype(BF16)

    t_c = t0 + lax.broadcasted_iota(jnp.int32, (TQ, NB), 0)
    n_c = lax.broadcasted_iota(jnp.int32, (TQ, NB), 1)
    cmask = t_c >= n_c * CMP_STRIDE + (CMP_LEN - 1)
    cmask4 = jnp.concatenate([cmask] * H, axis=0)
    p4 = _masked_softmax(_dot_t(q4, kc_ref[...].astype(BF16)) * scale, cmask4)
    p4b = p4.astype(BF16)
    o_cmp4 = _dot(p4b, vc_ref[...].astype(BF16))

    n_sel = S // SLC_LEN
    jj = lax.broadcasted_iota(jnp.int32, (n_sel, NB), 0)
    nn = lax.broadcasted_iota(jnp.int32, (n_sel, NB), 1)
    ov = ((nn * CMP_STRIDE < (jj + 1) * SLC_LEN) & (nn * CMP_STRIDE + CMP_LEN > jj * SLC_LEN)
          & (nn < NB - 1))
    po = _dot_t(jnp.where(ov, 1.0, 0.0).astype(BF16), p4b)
    imp = po[:, 0:TQ]
    for h in range(1, H):
        imp = imp + po[:, h * TQ:(h + 1) * TQ]

    j = lax.broadcasted_iota(jnp.int32, (n_sel, TQ), 0)
    cur = (t0 + lax.broadcasted_iota(jnp.int32, (n_sel, TQ), 1)) // SLC_LEN
    forced = (j == 0) | (j == cur) | (j == cur - 1)
    val = jnp.where(j > cur, -jnp.inf, jnp.where(forced, FORCED_SCORE, imp))
    rank = jnp.zeros((n_sel, TQ), F32)
    for ii in range(n_sel):
        vi = val[ii:ii + 1, :]
        beats = jnp.where(vi > val, 1.0, jnp.where(vi == val, jnp.where(j > ii, 1.0, 0.0), 0.0))
        rank = rank + beats
    sel_t = jnp.where(rank < SLC_TOPN, 1.0, 0.0)
    sel = jnp.concatenate([sel_t, jnp.zeros((LANES - n_sel, TQ), F32)], axis=0).T.astype(BF16)

    t_k = t0 + lax.broadcasted_iota(jnp.int32, (TQ, TQ), 0)
    k_i = lax.broadcasted_iota(jnp.int32, (TQ, TQ), 1)
    ones_half = jnp.ones((TQ, Dh), BF16)
    neg_inf = jnp.full((TQ, TQ), -jnp.inf, F32)
    scale_log2e = scale * LOG2_E

    def reset():
        m_ref[...] = jnp.full(m_ref.shape, -jnp.inf, F32)
        acc_ref[...] = jnp.zeros(acc_ref.shape, F32)

    def attend_chunk(k_ref, v_ref, c, bias):
        k0 = pl.multiple_of(c * TQ, TQ)
        kb = k_ref[pl.ds(k0, TQ), :].astype(BF16)
        vb = jnp.concatenate([v_ref[pl.ds(k0, TQ), :].astype(BF16), ones_half], axis=-1)
        s4 = _dot_t(q4, kb)
        for h in range(H):
            rs = slice(h * TQ, (h + 1) * TQ)
            s = s4[rs] if bias is None else s4[rs] + bias
            m_old = m_ref[rs]
            m_new = jnp.maximum(m_old, jnp.max(s, axis=-1, keepdims=True))
            m_safe = jnp.where(m_new == -jnp.inf, 0.0, m_new)
            p = jnp.exp2((s - jnp.concatenate([m_safe] * (TQ // LANES), axis=-1)) * scale_log2e)
            alpha = jnp.exp2((m_old - m_safe) * scale_log2e)
            acc_ref[rs] = jnp.concatenate([alpha, alpha], axis=-1) * acc_ref[rs] + _dot(p.astype(BF16), vb)
            m_ref[rs] = m_new

    def finish():
        acc = acc_ref[...]
        return acc[:, :Dh] / jnp.maximum(acc[:, Dh:], 1e-30)

    eb = lax.broadcasted_iota(jnp.int32, (LANES, TQ), 0)
    ek = lax.broadcasted_iota(jnp.int32, (LANES, TQ), 1) // SLC_LEN

    def sel_keys(c):
        expand = jnp.where(eb == ek + c * (TQ // SLC_LEN), 1.0, 0.0).astype(BF16)
        return _dot(sel, expand) > 0.5

    reset()

    def slc_body(c, carry):
        attend_chunk(ks_ref, vs_ref, c, jnp.where(sel_keys(c), 0.0, neg_inf))
        return carry

    lax.fori_loop(0, i, slc_body, 0)
    attend_chunk(ks_ref, vs_ref, i, jnp.where(sel_keys(i) & (t0 + k_i <= t_k), 0.0, neg_inf))
    o_slc4 = finish()

    reset()
    n_back = WIN // TQ
    for back in range(n_back, -1, -1):
        def win_chunk(back=back):
            c = i - back
            if 0 < back < n_back:
                bias = None
            else:
                dist = t_k - (c * TQ + k_i)
                bias = jnp.where((dist >= 0) & (dist < WIN), 0.0, neg_inf)
            attend_chunk(kw_ref, vw_ref, c, bias)
        if back == 0:
            win_chunk()
        else:
            pl.when(i >= back)(win_chunk)
    o_win4 = finish()

    gs = jax.nn.sigmoid(gt_ref[...])
    outs = []
    for h in range(H):
        rs = slice(h * TQ, (h + 1) * TQ)
        outs.append(gs[:, 3 * h:3 * h + 1] * o_cmp4[rs] + gs[:, 3 * h + 1:3 * h + 2] * o_slc4[rs]
                    + gs[:, 3 * h + 2:3 * h + 3] * o_win4[rs])
    o_ref[...] = _head_norm_gate(jnp.concatenate(outs, axis=-1), z_ref[...], g_ref[...])


def _nsa(proj, k_cmp, v_cmp, bg, l, B, S, TQ=256):
    assert WIN % TQ == 0 and TQ % SLC_LEN == 0 and TQ % LANES == 0
    nt = S // TQ
    T = B * S
    NB = k_cmp.shape[1]

    def seq(name):
        off, wp = _DST[name]
        return pl.BlockSpec((S, wp), lambda b, i: (b, off // wp))

    cspec = pl.BlockSpec((None, NB, HEAD_DIM), lambda b, i: (b, 0, 0))
    return pl.pallas_call(
        _nsa_kernel,
        out_shape=jax.ShapeDtypeStruct((T, W_BR), BF16),
        grid=(B, nt),
        in_specs=[_seg_spec('nsa_q', TQ, nt), _seg_spec('nsa_g', TQ, nt), _seg_spec('nsa_z', TQ, nt),
                  cspec, cspec, seq('nsa_ks'), seq('nsa_vs'), seq('nsa_kw'), seq('nsa_vw'),
                  _layer_spec(bg, l, 2)],
        out_specs=pl.BlockSpec((TQ, W_BR), lambda b, i: (b * nt + i, 0)),
        scratch_shapes=[pltpu.VMEM((N_HEADS * TQ, LANES), F32),
                        pltpu.VMEM((N_HEADS * TQ, 2 * HEAD_DIM), F32)],
        compiler_params=_params("parallel", "arbitrary"),
        name="nsa_attention",
    )(proj, proj, proj, k_cmp, v_cmp, proj, proj, proj, proj, bg)


def kernel(x, pre_norm_g, w_in, gla_w_fg2, gla_b_fg2, lru_conv_w, lru_conv_b, lru_w_a, lru_b_a, lru_w_x,
           lru_b_x, lru_lambda, nsa_cmp_pos_k, nsa_cmp_w1_k, nsa_cmp_w2_k, nsa_cmp_pos_v, nsa_cmp_w1_v,
           nsa_cmp_w2_v, conf_dw_w, conf_dw_b, conf_ln_g, conf_ln_b, conf_pw_w, conf_pw_b, branch_norm_g,
           w_out, post_norm_g):
    B, S, D = x.shape
    L = w_in.shape[0]
    T = B * S
    x2 = x.reshape(T, D)

    def rows(a):
        return a.reshape(L, 1, -1)

    wt_nsa, wt_mix = _relayout_w_in(w_in)
    lowrank = gla_w_fg2.shape[1]
    w1k = nsa_cmp_w1_k.reshape(L, CMP_LEN, HEAD_DIM, -1).astype(BF16)
    w1v = nsa_cmp_w1_v.reshape(L, CMP_LEN, HEAD_DIM, -1).astype(BF16)
    w2k = nsa_cmp_w2_k.astype(BF16)
    w2v = nsa_cmp_w2_v.astype(BF16)
    bg = branch_norm_g.reshape(L, N_MIXERS, 1, W_BR)
    p = dict(
        bg=bg, pre_g=rows(pre_norm_g), post_g=rows(post_norm_g), w_out=w_out.astype(BF16),
        wfg=jnp.pad(gla_w_fg2, ((0, 0), (0, LANES - lowrank), (0, 0))).astype(BF16), bfg=rows(gla_b_fg2),
        lru_cw=lru_conv_w, lru_cb=rows(lru_conv_b), wa=lru_w_a.astype(BF16), lru_ba=rows(lru_b_a),
        wx=lru_w_x.astype(BF16), lru_bx=rows(lru_b_x), lru_lam=rows(lru_lambda),
        conf_dw=conf_dw_w, conf_db=rows(conf_dw_b), conf_lg=rows(conf_ln_g), conf_lb=rows(conf_ln_b),
        pw=conf_pw_w.astype(BF16), conf_pb=rows(conf_pw_b))

    hn = _prenorm(x2, p['pre_g'], 0)
    for l in range(L):
        proj = _in_proj(hn, wt_nsa, l)
        k_cmp, v_cmp = _nsa_compress(proj, nsa_cmp_pos_k, w1k, w2k, nsa_cmp_pos_v, w1v, w2v, l, B, S)
        m_c = _nsa(proj, k_cmp, v_cmp, bg, l, B, S)
        if l + 1 < L:
            x2, hn = _mix_out(hn, wt_mix, m_c, x2, p, l, B, S, emit_hn=True)
        else:
            x2 = _mix_out(hn, wt_mix, m_c, x2, p, l, B, S, emit_hn=False)
    return x2.reshape(B, S, D)
```

```python
import functools

import jax
import jax.numpy as jnp
from jax import lax
from jax.experimental import pallas as pl
from jax.experimental.pallas import tpu as pltpu

F32 = jnp.float32
BF16 = jnp.bfloat16

D_MODEL = 2048
N_HEADS = 4
HEAD_DIM = 128
W_BR = N_HEADS * HEAD_DIM
N_MIXERS = 4
GLA_HEAD_K = 64
GLA_TAU = 16.0
GLA_CHUNK = 64
LRU_C = 8.0
LRU_CONV = 4
CMP_LEN = 32
CMP_STRIDE = 16
SLC_LEN = 64
SLC_TOPN = 16
SLC_GROUP = 4
WIN = 512
FORCED_SCORE = 1e3
CONF_KERNEL = 31
CONF_HIST = 32
EPS = 1e-6
LOG2_E = 1.4426950408889634
SUBLANES = 8
LANES = 128
VMEM_LIMIT = 48 * 1024 * 1024

_SEGS = (('gla_q', 256), ('gla_k', 256), ('gla_v', 512), ('gla_fg', 16), ('gla_z', 512),
         ('lru_x', 512), ('lru_z', 512),
         ('nsa_q', 512), ('nsa_kc', 128), ('nsa_vc', 128), ('nsa_ks', 128), ('nsa_vs', 128),
         ('nsa_kw', 128), ('nsa_vw', 128), ('nsa_g', 12), ('nsa_z', 512),
         ('conv_v', 512), ('conv_glu', 512), ('conv_z', 512))
_ORDER = ('gla_v', 'gla_z', 'lru_x', 'lru_z', 'nsa_q', 'nsa_z', 'conv_v', 'conv_glu', 'conv_z',
          'gla_q', 'gla_k', 'gla_fg', 'nsa_kc', 'nsa_vc', 'nsa_ks', 'nsa_vs', 'nsa_kw', 'nsa_vw',
          'nsa_g')


def _layout():
    src, off = {}, 0
    for name, w in _SEGS:
        src[name] = (off, w)
        off += w
    dst, off = {}, 0
    for name in _ORDER:
        w = src[name][1]
        wp = -(-w // LANES) * LANES
        assert off % wp == 0
        dst[name] = (off, wp)
        off += wp
    return src, dst, off


_SRC, _DST, D_PROJ_PAD = _layout()


def _relayout_w_in(w_in):
    wt = jnp.transpose(w_in, (2, 0, 1))
    parts = []
    for name in _ORDER:
        o, w = _SRC[name]
        wp = _DST[name][1]
        p = wt[o:o + w]
        if wp != w:
            p = jnp.pad(p, ((0, wp - w), (0, 0), (0, 0)))
        parts.append(p)
    return jnp.transpose(jnp.concatenate(parts, axis=0).astype(BF16), (1, 0, 2))


def _dot(a, b):
    return jnp.dot(a, b, preferred_element_type=F32)


def _dot_t(a, b):
    return lax.dot_general(a, b, (((1,), (1,)), ((), ())), preferred_element_type=F32)


def _softplus(x):
    return jnp.maximum(x, 0.0) + jnp.log1p(jnp.exp(-jnp.abs(x)))


def _expm1(x):
    return jnp.tanh(0.5 * x) * (jnp.exp(x) + 1.0)


def _silu(x):
    return x * jax.nn.sigmoid(x)


def _rmsnorm(x, g):
    return x * lax.rsqrt(jnp.mean(x * x, axis=-1, keepdims=True) + EPS) * g


def _masked_softmax(s, mask):
    s = jnp.where(mask, s, -jnp.inf)
    m = jnp.max(s, axis=-1, keepdims=True)
    m = jnp.where(jnp.isfinite(m), m, 0.0)
    p = jnp.exp(s - m)
    return p / jnp.maximum(jnp.sum(p, axis=-1, keepdims=True), 1e-30)


def _head_norm_gate(o, z, g):
    outs = []
    for h in range(N_HEADS):
        oh = o[:, h * HEAD_DIM:(h + 1) * HEAD_DIM]
        outs.append(oh * lax.rsqrt(jnp.mean(oh * oh, axis=-1, keepdims=True) + EPS))
    on = jnp.concatenate(outs, axis=-1) * g
    return (on * _silu(z)).astype(BF16)


def _layer_spec(arr, *lead):
    rest = arr.shape[len(lead):]
    idx = tuple(lead) + (0,) * len(rest)
    return pl.BlockSpec((None,) * len(lead) + rest, lambda *_: idx)


def _seg_spec(name, rows, nt):
    off, wp = _DST[name]
    cb = off // wp
    return pl.BlockSpec((rows, wp), lambda b, i: (b * nt + i, cb))


def _params(*sem):
    return pltpu.CompilerParams(dimension_semantics=sem, vmem_limit_bytes=VMEM_LIMIT)


def _prenorm_kernel(x_ref, g_ref, o_ref):
    o_ref[...] = _rmsnorm(x_ref[...], g_ref[...]).astype(BF16)


def _prenorm(x2, g, l, tm=512):
    T, D = x2.shape
    return pl.pallas_call(
        _prenorm_kernel,
        out_shape=jax.ShapeDtypeStruct((T, D), BF16),
        grid=(T // tm,),
        in_specs=[pl.BlockSpec((tm, D), lambda i: (i, 0)), _layer_spec(g, l)],
        out_specs=pl.BlockSpec((tm, D), lambda i: (i, 0)),
        compiler_params=_params("parallel"),
        name="prenorm",
    )(x2, g)


def _in_proj_kernel(h_ref, wt_ref, o_ref):
    o_ref[...] = _dot_t(h_ref[...], wt_ref[...])


def _in_proj(hn, wt, l, tm=1024, tn=2048):
    T, D = hn.shape
    NP = wt.shape[1]
    return pl.pallas_call(
        _in_proj_kernel,
        out_shape=jax.ShapeDtypeStruct((T, NP), F32),
        grid=(T // tm, NP // tn),
        in_specs=[pl.BlockSpec((tm, D), lambda i, j: (i, 0)),
                  pl.BlockSpec((None, tn, D), lambda i, j: (l, j, 0))],
        out_specs=pl.BlockSpec((tm, tn), lambda i, j: (i, j)),
        compiler_params=_params("parallel", "arbitrary"),
        name="in_proj",
    )(hn, wt)


N_GLA_IN, N_LRU_IN, N_CONF_IN = 8, 10, 10


def _mix_out_kernel(*refs, emit_hn):
    refs = list(refs)
    gla_in = [refs.pop(0) for _ in range(N_GLA_IN)]
    lru_in = [refs.pop(0) for _ in range(N_LRU_IN)]
    conf_in = [refs.pop(0) for _ in range(N_CONF_IN)]
    mc_ref, w_ref, x_ref, pg_ref = [refs.pop(0) for _ in range(4)]
    gn_ref = refs.pop(0) if emit_hn else None
    o_ref = refs.pop(0)
    hn_ref = refs.pop(0) if emit_hn else None
    gla_state, gla_acc, lru_hist, lru_h, lru_acc, conf_hist = refs

    @pl.when(pl.program_id(1) == 0)
    def _():
        for ref in (gla_state, lru_hist, lru_h, conf_hist):
            ref[...] = jnp.zeros(ref.shape, F32)

    def w_rows(m):
        return w_ref[m * W_BR:(m + 1) * W_BR, :]

    y = _dot(mc_ref[...], w_rows(2))
    y = y + _dot(_gla_mix(*gla_in, gla_state, gla_acc), w_rows(0))
    y = y + _dot(_lru_mix(*lru_in, lru_hist, lru_h, lru_acc), w_rows(1))
    y = y + _dot(_conf_mix(*conf_in, conf_hist), w_rows(3))
    x_new = x_ref[...] + _rmsnorm(y, pg_ref[...])
    o_ref[...] = x_new
    if emit_hn:
        hn_ref[...] = _rmsnorm(x_new, gn_ref[...]).astype(BF16)


def _mix_out(proj, m_c, x2, p, l, B, S, emit_hn, R=256):
    nt = S // R
    T, D = x2.shape
    bg = p['bg']

    def tile(width):
        return pl.BlockSpec((R, width), lambda b, i: (b * nt + i, 0))

    def seg(*names):
        return [_seg_spec(n, R, nt) for n in names]

    def layer(*names):
        return [_layer_spec(p[n], l) for n in names]

    gla = seg('gla_q', 'gla_k', 'gla_v', 'gla_fg', 'gla_z') + layer('wfg', 'bfg') + [_layer_spec(bg, l, 0)]
    lru = (seg('lru_x', 'lru_z') + layer('lru_cw', 'lru_cb', 'wa', 'lru_ba', 'wx', 'lru_bx', 'lru_lam')
           + [_layer_spec(bg, l, 1)])
    conf = (seg('conv_v', 'conv_glu', 'conv_z') + layer('conf_dw', 'conf_db', 'conf_lg', 'conf_lb', 'pw', 'conf_pb')
            + [_layer_spec(bg, l, 3)])
    assert (len(gla), len(lru), len(conf)) == (N_GLA_IN, N_LRU_IN, N_CONF_IN)
    in_specs = gla + lru + conf + [tile(W_BR), _layer_spec(p['w_out'], l), tile(D), _layer_spec(p['post_g'], l)]
    args = ([proj] * 5 + [p['wfg'], p['bfg'], bg]
            + [proj] * 2 + [p[n] for n in ('lru_cw', 'lru_cb', 'wa', 'lru_ba', 'wx', 'lru_bx', 'lru_lam')] + [bg]
            + [proj] * 3 + [p[n] for n in ('conf_dw', 'conf_db', 'conf_lg', 'conf_lb', 'pw', 'conf_pb')] + [bg]
            + [m_c, p['w_out'], x2, p['post_g']])
    out_shape = jax.ShapeDtypeStruct((T, D), F32)
    out_specs = tile(D)
    if emit_hn:
        in_specs.append(_layer_spec(p['pre_g'], l + 1))
        args.append(p['pre_g'])
        out_shape = (out_shape, jax.ShapeDtypeStruct((T, D), BF16))
        out_specs = (tile(D), tile(D))
    return pl.pallas_call(
        functools.partial(_mix_out_kernel, emit_hn=emit_hn),
        out_shape=out_shape,
        grid=(B, nt),
        in_specs=in_specs,
        out_specs=out_specs,
        scratch_shapes=[pltpu.VMEM((N_HEADS, HEAD_DIM, LANES), F32), pltpu.VMEM((R, W_BR), F32),
                        pltpu.VMEM((SUBLANES, W_BR), F32), pltpu.VMEM((1, W_BR), F32),
                        pltpu.VMEM((R, W_BR), F32), pltpu.VMEM((CONF_HIST, W_BR), F32)],
        compiler_params=_params("parallel", "arbitrary"),
        name="mix_out",
    )(*args)


def _gla_mix(q_ref, k_ref, v_ref, fg_ref, z_ref, wfg_ref, bfg_ref, g_ref, state_ref, acc_ref):
    R = q_ref.shape[0]
    C = GLA_CHUNK

    pre = _dot(fg_ref[...].astype(BF16), wfg_ref[...]) + bfg_ref[...]
    log_f = -_softplus(-pre) * (1.0 / GLA_TAU)
    row = lax.broadcasted_iota(jnp.int32, log_f.shape, 0) % C
    bcum = log_f
    d = 1
    while d < C:
        bcum = bcum + jnp.where(row >= d, pltpu.roll(bcum, d, 0), 0.0)
        d *= 2

    q = q_ref[...] * (GLA_HEAD_K ** -0.5)
    k = k_ref[...]
    v = v_ref[...]
    lane = lax.broadcasted_iota(jnp.int32, (C, LANES), 1)
    causal = (lax.broadcasted_iota(jnp.int32, (C, C), 0) >= lax.broadcasted_iota(jnp.int32, (C, C), 1))
    head_lanes = (lane < GLA_HEAD_K, lane >= GLA_HEAD_K)
    for c in range(R // C):
        rs = slice(c * C, (c + 1) * C)
        for p in range(2):
            ls = slice(p * LANES, (p + 1) * LANES)
            b = bcum[rs, ls]
            b_last = b[C - 1:C, :]
            kk = k[rs, ls]
            q_dec = q[rs, ls] * jnp.exp(b)
            k_dec = kk * jnp.exp(-b)
            k_last = kk * jnp.exp(b_last - b)
            decay = jnp.exp(b_last)
            for hh in range(2):
                h = 2 * p + hh
                m = head_lanes[hh]
                qd = jnp.where(m, q_dec, 0.0).astype(BF16)
                kd = jnp.where(m, k_dec, 0.0).astype(BF16)
                kl = jnp.where(m, k_last, 0.0).astype(BF16)
                vh = v[rs, h * HEAD_DIM:(h + 1) * HEAD_DIM]
                attn = jnp.where(causal, _dot_t(qd, kd), 0.0)
                st = state_ref[h]
                o = _dot(attn.astype(BF16), vh.astype(BF16)) + _dot_t(qd, st.astype(BF16))
                acc_ref[rs, h * HEAD_DIM:(h + 1) * HEAD_DIM] = o
                state_ref[h] = decay * st + _dot(vh.T.astype(BF16), kl)

    return _head_norm_gate(acc_ref[...], z_ref[...], g_ref[...])


def _lru_mix(x_ref, z_ref, cw_ref, cb_ref, wa_ref, ba_ref, wx_ref, bx_ref, lam_ref, g_ref,
             hist_ref, h_ref, acc_ref):
    R = x_ref.shape[0]
    HIST = hist_ref.shape[0]

    x = x_ref[...]
    xe = jnp.concatenate([hist_ref[...], x], axis=0)
    xc = cb_ref[...] + cw_ref[LRU_CONV - 1:LRU_CONV, :] * x
    for kk in range(LRU_CONV - 1):
        sh = LRU_CONV - 1 - kk
        xc = xc + cw_ref[kk:kk + 1, :] * pltpu.roll(xe, sh, 0)[HIST:HIST + R]
    hist_ref[...] = x[R - HIST:R]

    xb = xc.astype(BF16)
    ra, rx = [], []
    for h in range(N_HEADS):
        xh = xb[:, h * HEAD_DIM:(h + 1) * HEAD_DIM]
        ra.append(_dot(xh, wa_ref[h]))
        rx.append(_dot(xh, wx_ref[h]))
    r = jax.nn.sigmoid(jnp.concatenate(ra, axis=-1) + ba_ref[...])
    ig = jax.nn.sigmoid(jnp.concatenate(rx, axis=-1) + bx_ref[...])
    log_a = -LRU_C * r * _softplus(-lam_ref[...])
    a = jnp.exp(log_a)
    u = jnp.sqrt(-_expm1(2.0 * log_a)) * (ig * xc)

    G = R // SUBLANES
    a3 = a.reshape(G, SUBLANES, W_BR)
    u3 = u.reshape(G, SUBLANES, W_BR)
    row = lax.broadcasted_iota(jnp.int32, a3.shape, 1)
    d = 1
    while d < SUBLANES:
        keep = row >= d
        u3 = jnp.where(keep, a3 * pltpu.roll(u3, d, 1) + u3, u3)
        a3 = jnp.where(keep, a3 * pltpu.roll(a3, d, 1), a3)
        d *= 2
    carry = h_ref[...]
    for gi in range(G):
        hg = a3[gi] * carry + u3[gi]
        acc_ref[gi * SUBLANES:(gi + 1) * SUBLANES, :] = hg
        carry = hg[SUBLANES - 1:SUBLANES, :]
    h_ref[...] = carry
    return _head_norm_gate(acc_ref[...], z_ref[...], g_ref[...])


def _conf_mix(v_ref, glu_ref, z_ref, dw_ref, db_ref, lg_ref, lb_ref, pw_ref, pb_ref, g_ref, hist_ref):
    R = v_ref.shape[0]
    HIST = hist_ref.shape[0]

    y = v_ref[...] * jax.nn.sigmoid(glu_ref[...])
    ye = jnp.concatenate([hist_ref[...], y], axis=0)
    acc = db_ref[...]
    for r in range(SUBLANES):
        yr = ye if r == 0 else pltpu.roll(ye, r, 0)
        for q in range((CONF_KERNEL - 1 - r) // SUBLANES + 1):
            kk = CONF_KERNEL - 1 - (SUBLANES * q + r)
            start = HIST - SUBLANES * q
            acc = acc + dw_ref[kk:kk + 1, :] * yr[start:start + R]
    hist_ref[...] = y[R - HIST:R]

    mu = jnp.mean(acc, axis=-1, keepdims=True)
    xc = acc - mu
    var = jnp.mean(xc * xc, axis=-1, keepdims=True)
    yn = xc * lax.rsqrt(var + EPS) * lg_ref[...] + lb_ref[...]
    o = _dot(_silu(yn).astype(BF16), pw_ref[...]) + pb_ref[...]
    return _head_norm_gate(o, z_ref[...], g_ref[...])


def _cmp_kernel(kc_ref, vc_ref, posk_ref, w1k_ref, w2k_ref, posv_ref, w1v_ref, w2v_ref, ko_ref, vo_ref):
    S = kc_ref.shape[0]
    NB = S // CMP_STRIDE
    HALF = CMP_LEN // 2

    def compress(z_ref, pos_ref, w1_ref, w2_ref):
        lo = jnp.zeros((NB, w1_ref.shape[2]), F32)
        hi = jnp.zeros((NB, w1_ref.shape[2]), F32)
        for l in range(HALF):
            grp = z_ref[pl.ds(l, NB, stride=CMP_STRIDE), :]
            lo = lo + _dot((grp + pos_ref[l:l + 1, :]).astype(BF16), w1_ref[l])
            hi = hi + _dot((grp + pos_ref[HALF + l:HALF + l + 1, :]).astype(BF16), w1_ref[HALF + l])
        hid = lo + pltpu.roll(hi, NB - 1, 0)
        blk = lax.broadcasted_iota(jnp.int32, hid.shape, 0)
        hid = jnp.where(blk < NB - 1, hid, 0.0)
        return _dot(_silu(hid).astype(BF16), w2_ref[...])

    ko_ref[...] = compress(kc_ref, posk_ref, w1k_ref, w2k_ref)
    vo_ref[...] = compress(vc_ref, posv_ref, w1v_ref, w2v_ref)


def _nsa_compress(proj, posk, w1k, w2k, posv, w1v, w2v, l, B, S):
    NB = S // CMP_STRIDE

    def seg(name):
        off, wp = _DST[name]
        return pl.BlockSpec((S, wp), lambda b: (b, off // wp))

    out = jax.ShapeDtypeStruct((B, NB, HEAD_DIM), F32)
    ospec = pl.BlockSpec((None, NB, HEAD_DIM), lambda b: (b, 0, 0))
    return pl.pallas_call(
        _cmp_kernel,
        out_shape=(out, out),
        grid=(B,),
        in_specs=[seg('nsa_kc'), seg('nsa_vc'), _layer_spec(posk, l), _layer_spec(w1k, l),
                  _layer_spec(w2k, l), _layer_spec(posv, l), _layer_spec(w1v, l), _layer_spec(w2v, l)],
        out_specs=(ospec, ospec),
        compiler_params=_params("parallel"),
        name="nsa_compress",
    )(proj, proj, posk, w1k, w2k, posv, w1v, w2v)


def _nsa_kernel(q_ref, gt_ref, z_ref, kc_ref, vc_ref, ks_ref, vs_ref, kw_ref, vw_ref, g_ref, o_ref,
                m_ref, acc_ref):
    TQ = q_ref.shape[0]
    S = ks_ref.shape[0]
    NB = kc_ref.shape[0]
    H, Dh = N_HEADS, HEAD_DIM
    scale = Dh ** -0.5
    i = pl.program_id(1)
    t0 = i * TQ

    q = q_ref[...]
    q4 = jnp.concatenate([q[:, h * Dh:(h + 1) * Dh] for h in range(H)], axis=0).astype(BF16)

    t_c = t0 + lax.broadcasted_iota(jnp.int32, (TQ, NB), 0)
    n_c = lax.broadcasted_iota(jnp.int32, (TQ, NB), 1)
    cmask = t_c >= n_c * CMP_STRIDE + (CMP_LEN - 1)
    cmask4 = jnp.concatenate([cmask] * H, axis=0)
    p4 = _masked_softmax(_dot_t(q4, kc_ref[...].astype(BF16)) * scale, cmask4)
    p4b = p4.astype(BF16)
    o_cmp4 = _dot(p4b, vc_ref[...].astype(BF16))

    n_sel = S // SLC_LEN
    jj = lax.broadcasted_iota(jnp.int32, (n_sel, NB), 0)
    nn = lax.broadcasted_iota(jnp.int32, (n_sel, NB), 1)
    ov = ((nn * CMP_STRIDE < (jj + 1) * SLC_LEN) & (nn * CMP_STRIDE + CMP_LEN > jj * SLC_LEN)
          & (nn < NB - 1))
    po = _dot_t(jnp.where(ov, 1.0, 0.0).astype(BF16), p4b)
    imp = po[:, 0:TQ]
    for h in range(1, H):
        imp = imp + po[:, h * TQ:(h + 1) * TQ]

    j = lax.broadcasted_iota(jnp.int32, (n_sel, TQ), 0)
    cur = (t0 + lax.broadcasted_iota(jnp.int32, (n_sel, TQ), 1)) // SLC_LEN
    forced = (j == 0) | (j == cur) | (j == cur - 1)
    val = jnp.where(j > cur, -jnp.inf, jnp.where(forced, FORCED_SCORE, imp))
    rank = jnp.zeros((n_sel, TQ), F32)
    for ii in range(n_sel):
        vi = val[ii:ii + 1, :]
        beats = jnp.where(vi > val, 1.0, jnp.where(vi == val, jnp.where(j > ii, 1.0, 0.0), 0.0))
        rank = rank + beats
    sel_t = jnp.where(rank < SLC_TOPN, 1.0, 0.0)
    sel = jnp.concatenate([sel_t, jnp.zeros((LANES - n_sel, TQ), F32)], axis=0).T.astype(BF16)

    t_k = t0 + lax.broadcasted_iota(jnp.int32, (TQ, TQ), 0)
    k_i = lax.broadcasted_iota(jnp.int32, (TQ, TQ), 1)
    ones_half = jnp.ones((TQ, Dh), BF16)
    neg_inf = jnp.full((TQ, TQ), -jnp.inf, F32)
    scale_log2e = scale * LOG2_E

    def reset():
        m_ref[...] = jnp.full(m_ref.shape, -jnp.inf, F32)
        acc_ref[...] = jnp.zeros(acc_ref.shape, F32)

    def attend_chunk(k_ref, v_ref, c, bias):
        k0 = pl.multiple_of(c * TQ, TQ)
        kb = k_ref[pl.ds(k0, TQ), :].astype(BF16)
        vb = jnp.concatenate([v_ref[pl.ds(k0, TQ), :].astype(BF16), ones_half], axis=-1)
        s4 = _dot_t(q4, kb)
        for h in range(H):
            rs = slice(h * TQ, (h + 1) * TQ)
            s = s4[rs] if bias is None else s4[rs] + bias
            m_old = m_ref[rs]
            m_new = jnp.maximum(m_old, jnp.max(s, axis=-1, keepdims=True))
            m_safe = jnp.where(m_new == -jnp.inf, 0.0, m_new)
            p = jnp.exp2((s - jnp.concatenate([m_safe] * (TQ // LANES), axis=-1)) * scale_log2e)
            alpha = jnp.exp2((m_old - m_safe) * scale_log2e)
            acc_ref[rs] = jnp.concatenate([alpha, alpha], axis=-1) * acc_ref[rs] + _dot(p.astype(BF16), vb)
            m_ref[rs] = m_new

    def finish():
        acc = acc_ref[...]
        return acc[:, :Dh] / jnp.maximum(acc[:, Dh:], 1e-30)

    eb = lax.broadcasted_iota(jnp.int32, (LANES, TQ), 0)
    ek = lax.broadcasted_iota(jnp.int32, (LANES, TQ), 1) // SLC_LEN

    def sel_keys(c):
        expand = jnp.where(eb == ek + c * (TQ // SLC_LEN), 1.0, 0.0).astype(BF16)
        return _dot(sel, expand) > 0.5

    reset()

    def slc_chunk(c):
        attend_chunk(ks_ref, vs_ref, c, jnp.where(sel_keys(c), 0.0, neg_inf))

    def slc_diagonal():
        attend_chunk(ks_ref, vs_ref, i, jnp.where(sel_keys(i) & (t0 + k_i <= t_k), 0.0, neg_inf))

    def slc_group(cg, carry):
        for u in range(SLC_GROUP):
            slc_chunk(SLC_GROUP * cg + u)
        return carry

    lax.fori_loop(0, i // SLC_GROUP, slc_group, 0)
    for rem in range(SLC_GROUP):
        def slc_tail(rem=rem):
            for u in range(rem, 0, -1):
                slc_chunk(i - u)
            slc_diagonal()
        pl.when(i % SLC_GROUP == rem)(slc_tail)
    o_slc4 = finish()

    reset()
    n_back = WIN // TQ
    def win_chunk(back):
        c = i - back
        if 0 < back < n_back:
            bias = None
        else:
            dist = t_k - (c * TQ + k_i)
            bias = jnp.where((dist >= 0) & (dist < WIN), 0.0, neg_inf)
        attend_chunk(kw_ref, vw_ref, c, bias)

    for first in range(n_back, -1, -1):
        def win_chunks(first=first):
            for back in range(first, -1, -1):
                win_chunk(back)
        pl.when(jnp.minimum(i, n_back) == first)(win_chunks)
    o_win4 = finish()

    gs = jax.nn.sigmoid(gt_ref[...])
    outs = []
    for h in range(H):
        rs = slice(h * TQ, (h + 1) * TQ)
        outs.append(gs[:, 3 * h:3 * h + 1] * o_cmp4[rs] + gs[:, 3 * h + 1:3 * h + 2] * o_slc4[rs]
                    + gs[:, 3 * h + 2:3 * h + 3] * o_win4[rs])
    o_ref[...] = _head_norm_gate(jnp.concatenate(outs, axis=-1), z_ref[...], g_ref[...])


def _nsa(proj, k_cmp, v_cmp, bg, l, B, S, TQ=256):
    assert WIN % TQ == 0 and TQ % SLC_LEN == 0 and TQ % LANES == 0
    nt = S // TQ
    T = B * S
    NB = k_cmp.shape[1]

    def seq(name):
        off, wp = _DST[name]
        return pl.BlockSpec((S, wp), lambda b, i: (b, off // wp))

    cspec = pl.BlockSpec((None, NB, HEAD_DIM), lambda b, i: (b, 0, 0))
    return pl.pallas_call(
        _nsa_kernel,
        out_shape=jax.ShapeDtypeStruct((T, W_BR), BF16),
        grid=(B, nt),
        in_specs=[_seg_spec('nsa_q', TQ, nt), _seg_spec('nsa_g', TQ, nt), _seg_spec('nsa_z', TQ, nt),
                  cspec, cspec, seq('nsa_ks'), seq('nsa_vs'), seq('nsa_kw'), seq('nsa_vw'),
                  _layer_spec(bg, l, 2)],
        out_specs=pl.BlockSpec((TQ, W_BR), lambda b, i: (b * nt + i, 0)),
        scratch_shapes=[pltpu.VMEM((N_HEADS * TQ, LANES), F32),
                        pltpu.VMEM((N_HEADS * TQ, 2 * HEAD_DIM), F32)],
        compiler_params=_params("parallel", "arbitrary"),
        name="nsa_attention",
    )(proj, proj, proj, k_cmp, v_cmp, proj, proj, proj, proj, bg)


def kernel(x, pre_norm_g, w_in, gla_w_fg2, gla_b_fg2, lru_conv_w, lru_conv_b, lru_w_a, lru_b_a, lru_w_x,
           lru_b_x, lru_lambda, nsa_cmp_pos_k, nsa_cmp_w1_k, nsa_cmp_w2_k, nsa_cmp_pos_v, nsa_cmp_w1_v,
           nsa_cmp_w2_v, conf_dw_w, conf_dw_b, conf_ln_g, conf_ln_b, conf_pw_w, conf_pw_b, branch_norm_g,
           w_out, post_norm_g):
    B, S, D = x.shape
    L = w_in.shape[0]
    T = B * S
    x2 = x.reshape(T, D)

    def rows(a):
        return a.reshape(L, 1, -1)

    w_in_t = _relayout_w_in(w_in)
    lowrank = gla_w_fg2.shape[1]
    w1k = nsa_cmp_w1_k.reshape(L, CMP_LEN, HEAD_DIM, -1).astype(BF16)
    w1v = nsa_cmp_w1_v.reshape(L, CMP_LEN, HEAD_DIM, -1).astype(BF16)
    w2k = nsa_cmp_w2_k.astype(BF16)
    w2v = nsa_cmp_w2_v.astype(BF16)
    bg = branch_norm_g.reshape(L, N_MIXERS, 1, W_BR)
    p = dict(
        bg=bg, pre_g=rows(pre_norm_g), post_g=rows(post_norm_g), w_out=w_out.astype(BF16),
        wfg=jnp.pad(gla_w_fg2, ((0, 0), (0, LANES - lowrank), (0, 0))).astype(BF16), bfg=rows(gla_b_fg2),
        lru_cw=lru_conv_w, lru_cb=rows(lru_conv_b), wa=lru_w_a.astype(BF16), lru_ba=rows(lru_b_a),
        wx=lru_w_x.astype(BF16), lru_bx=rows(lru_b_x), lru_lam=rows(lru_lambda),
        conf_dw=conf_dw_w, conf_db=rows(conf_dw_b), conf_lg=rows(conf_ln_g), conf_lb=rows(conf_ln_b),
        pw=conf_pw_w.astype(BF16), conf_pb=rows(conf_pw_b))

    hn = _prenorm(x2, p['pre_g'], 0)
    for l in range(L):
        proj = _in_proj(hn, w_in_t, l)
        k_cmp, v_cmp = _nsa_compress(proj, nsa_cmp_pos_k, w1k, w2k, nsa_cmp_pos_v, w1v, w2v, l, B, S)
        m_c = _nsa(proj, k_cmp, v_cmp, bg, l, B, S)
        if l + 1 < L:
            x2, hn = _mix_out(proj, m_c, x2, p, l, B, S, emit_hn=True)
        else:
            x2 = _mix_out(proj, m_c, x2, p, l, B, S, emit_hn=False)
    return x2.reshape(B, S, D)
```

```python
import functools

import jax
import jax.numpy as jnp
from jax import lax
from jax.experimental import pallas as pl
from jax.experimental.pallas import tpu as pltpu

F32 = jnp.float32
BF16 = jnp.bfloat16

D_MODEL = 2048
N_HEADS = 4
HEAD_DIM = 128
W_BR = N_HEADS * HEAD_DIM
N_MIXERS = 4
GLA_HEAD_K = 64
GLA_TAU = 16.0
GLA_CHUNK = 64
LRU_C = 8.0
LRU_CONV = 4
CMP_LEN = 32
CMP_STRIDE = 16
SLC_LEN = 64
SLC_TOPN = 16
SLC_GROUP = 4
WIN = 512
FORCED_SCORE = 1e3
CONF_KERNEL = 31
CONF_HIST = 32
EPS = 1e-6
LOG2_E = 1.4426950408889634
SUBLANES = 8
LANES = 128
VMEM_LIMIT = 48 * 1024 * 1024

_SEGS = (('gla_q', 256), ('gla_k', 256), ('gla_v', 512), ('gla_fg', 16), ('gla_z', 512),
         ('lru_x', 512), ('lru_z', 512),
         ('nsa_q', 512), ('nsa_kc', 128), ('nsa_vc', 128), ('nsa_ks', 128), ('nsa_vs', 128),
         ('nsa_kw', 128), ('nsa_vw', 128), ('nsa_g', 12), ('nsa_z', 512),
         ('conv_v', 512), ('conv_glu', 512), ('conv_z', 512))
_ORDER = ('gla_v', 'gla_z', 'lru_x', 'lru_z', 'nsa_q', 'nsa_z', 'conv_v', 'conv_glu', 'conv_z',
          'gla_q', 'gla_k', 'gla_fg', 'nsa_kc', 'nsa_vc', 'nsa_ks', 'nsa_vs', 'nsa_kw', 'nsa_vw',
          'nsa_g')


def _layout():
    src, off = {}, 0
    for name, w in _SEGS:
        src[name] = (off, w)
        off += w
    dst, off = {}, 0
    for name in _ORDER:
        w = src[name][1]
        wp = -(-w // LANES) * LANES
        assert off % wp == 0
        dst[name] = (off, wp)
        off += wp
    return src, dst, off


_SRC, _DST, D_PROJ_PAD = _layout()


def _relayout_w_in(w_in):
    wt = jnp.transpose(w_in, (2, 0, 1))
    parts = []
    for name in _ORDER:
        o, w = _SRC[name]
        wp = _DST[name][1]
        p = wt[o:o + w]
        if wp != w:
            p = jnp.pad(p, ((0, wp - w), (0, 0), (0, 0)))
        parts.append(p)
    return jnp.transpose(jnp.concatenate(parts, axis=0).astype(BF16), (1, 0, 2))


def _dot(a, b):
    return jnp.dot(a, b, preferred_element_type=F32)


def _dot_t(a, b):
    return lax.dot_general(a, b, (((1,), (1,)), ((), ())), preferred_element_type=F32)


def _softplus(x):
    return jnp.maximum(x, 0.0) + jnp.log1p(jnp.exp(-jnp.abs(x)))


def _expm1(x):
    return jnp.tanh(0.5 * x) * (jnp.exp(x) + 1.0)


def _silu(x):
    return x * jax.nn.sigmoid(x)


def _rmsnorm(x, g):
    return x * lax.rsqrt(jnp.mean(x * x, axis=-1, keepdims=True) + EPS) * g


def _masked_softmax(s, mask):
    s = jnp.where(mask, s, -jnp.inf)
    m = jnp.max(s, axis=-1, keepdims=True)
    m = jnp.where(jnp.isfinite(m), m, 0.0)
    p = jnp.exp(s - m)
    return p / jnp.maximum(jnp.sum(p, axis=-1, keepdims=True), 1e-30)


def _head_norm_gate(o, z, g):
    outs = []
    for h in range(N_HEADS):
        oh = o[:, h * HEAD_DIM:(h + 1) * HEAD_DIM]
        outs.append(oh * lax.rsqrt(jnp.mean(oh * oh, axis=-1, keepdims=True) + EPS))
    on = jnp.concatenate(outs, axis=-1) * g
    return (on * _silu(z)).astype(BF16)


def _layer_spec(arr, *lead):
    rest = arr.shape[len(lead):]
    idx = tuple(lead) + (0,) * len(rest)
    return pl.BlockSpec((None,) * len(lead) + rest, lambda *_: idx)


def _seg_spec(name, rows, nt):
    off, wp = _DST[name]
    cb = off // wp
    return pl.BlockSpec((rows, wp), lambda b, i: (b * nt + i, cb))


def _params(*sem):
    return pltpu.CompilerParams(dimension_semantics=sem, vmem_limit_bytes=VMEM_LIMIT)


def _prenorm_kernel(x_ref, g_ref, o_ref):
    o_ref[...] = _rmsnorm(x_ref[...], g_ref[...]).astype(BF16)


def _prenorm(x2, g, l, tm=512):
    T, D = x2.shape
    return pl.pallas_call(
        _prenorm_kernel,
        out_shape=jax.ShapeDtypeStruct((T, D), BF16),
        grid=(T // tm,),
        in_specs=[pl.BlockSpec((tm, D), lambda i: (i, 0)), _layer_spec(g, l)],
        out_specs=pl.BlockSpec((tm, D), lambda i: (i, 0)),
        compiler_params=_params("parallel"),
        name="prenorm",
    )(x2, g)


def _in_proj_kernel(h_ref, wt_ref, o_ref):
    o_ref[...] = _dot_t(h_ref[...], wt_ref[...])


def _in_proj(hn, wt, l, tm=1024, tn=2048):
    T, D = hn.shape
    NP = wt.shape[1]
    return pl.pallas_call(
        _in_proj_kernel,
        out_shape=jax.ShapeDtypeStruct((T, NP), F32),
        grid=(T // tm, NP // tn),
        in_specs=[pl.BlockSpec((tm, D), lambda i, j: (i, 0)),
                  pl.BlockSpec((None, tn, D), lambda i, j: (l, j, 0))],
        out_specs=pl.BlockSpec((tm, tn), lambda i, j: (i, j)),
        compiler_params=_params("parallel", "arbitrary"),
        name="in_proj",
    )(hn, wt)


N_GLA_IN, N_LRU_IN, N_CONF_IN = 8, 10, 10


def _mix_out_kernel(*refs, emit_hn):
    refs = list(refs)
    gla_in = [refs.pop(0) for _ in range(N_GLA_IN)]
    lru_in = [refs.pop(0) for _ in range(N_LRU_IN)]
    conf_in = [refs.pop(0) for _ in range(N_CONF_IN)]
    mc_ref, w_ref, x_ref, pg_ref = [refs.pop(0) for _ in range(4)]
    gn_ref = refs.pop(0) if emit_hn else None
    o_ref = refs.pop(0)
    hn_ref = refs.pop(0) if emit_hn else None
    gla_state, gla_acc, lru_hist, lru_h, lru_acc, conf_hist = refs

    @pl.when(pl.program_id(1) == 0)
    def _():
        for ref in (gla_state, lru_hist, lru_h, conf_hist):
            ref[...] = jnp.zeros(ref.shape, F32)

    gla_stages, gla_finish = _gla_stages(*gla_in, gla_state, gla_acc)

    def next_gla_stage():
        if gla_stages:
            gla_stages.pop(0)()

    m_b = _lru_mix(*lru_in, lru_hist, lru_h, lru_acc)
    m_d = _conf_mix(*conf_in, conf_hist, next_gla_stage)
    while gla_stages:
        next_gla_stage()
    mixed = jnp.concatenate([gla_finish(), m_b, mc_ref[...], m_d], axis=-1)
    x_new = x_ref[...] + _rmsnorm(_dot(mixed, w_ref[...]), pg_ref[...])
    o_ref[...] = x_new
    if emit_hn:
        hn_ref[...] = _rmsnorm(x_new, gn_ref[...]).astype(BF16)


def _mix_out(proj, m_c, x2, p, l, B, S, emit_hn, R=256):
    nt = S // R
    T, D = x2.shape
    bg = p['bg']

    def tile(width):
        return pl.BlockSpec((R, width), lambda b, i: (b * nt + i, 0))

    def seg(*names):
        return [_seg_spec(n, R, nt) for n in names]

    def layer(*names):
        return [_layer_spec(p[n], l) for n in names]

    gla = seg('gla_q', 'gla_k', 'gla_v', 'gla_fg', 'gla_z') + layer('wfg', 'bfg') + [_layer_spec(bg, l, 0)]
    lru = (seg('lru_x', 'lru_z') + layer('lru_cw', 'lru_cb', 'wa', 'lru_ba', 'wx', 'lru_bx', 'lru_lam')
           + [_layer_spec(bg, l, 1)])
    conf = (seg('conv_v', 'conv_glu', 'conv_z') + layer('conf_dw', 'conf_db', 'conf_lg', 'conf_lb', 'pw', 'conf_pb')
            + [_layer_spec(bg, l, 3)])
    assert (len(gla), len(lru), len(conf)) == (N_GLA_IN, N_LRU_IN, N_CONF_IN)
    in_specs = gla + lru + conf + [tile(W_BR), _layer_spec(p['w_out'], l), tile(D), _layer_spec(p['post_g'], l)]
    args = ([proj] * 5 + [p['wfg'], p['bfg'], bg]
            + [proj] * 2 + [p[n] for n in ('lru_cw', 'lru_cb', 'wa', 'lru_ba', 'wx', 'lru_bx', 'lru_lam')] + [bg]
            + [proj] * 3 + [p[n] for n in ('conf_dw', 'conf_db', 'conf_lg', 'conf_lb', 'pw', 'conf_pb')] + [bg]
            + [m_c, p['w_out'], x2, p['post_g']])
    out_shape = jax.ShapeDtypeStruct((T, D), F32)
    out_specs = tile(D)
    if emit_hn:
        in_specs.append(_layer_spec(p['pre_g'], l + 1))
        args.append(p['pre_g'])
        out_shape = (out_shape, jax.ShapeDtypeStruct((T, D), BF16))
        out_specs = (tile(D), tile(D))
    return pl.pallas_call(
        functools.partial(_mix_out_kernel, emit_hn=emit_hn),
        out_shape=out_shape,
        grid=(B, nt),
        in_specs=in_specs,
        out_specs=out_specs,
        scratch_shapes=[pltpu.VMEM((N_HEADS, HEAD_DIM, LANES), F32), pltpu.VMEM((R, W_BR), F32),
                        pltpu.VMEM((SUBLANES, W_BR), F32), pltpu.VMEM((1, W_BR), F32),
                        pltpu.VMEM((R, W_BR), F32), pltpu.VMEM((CONF_HIST, W_BR), F32)],
        compiler_params=_params("parallel", "arbitrary"),
        name="mix_out",
    )(*args)


def _gla_stages(q_ref, k_ref, v_ref, fg_ref, z_ref, wfg_ref, bfg_ref, g_ref, state_ref, acc_ref):
    R = q_ref.shape[0]
    C = GLA_CHUNK

    pre = _dot(fg_ref[...].astype(BF16), wfg_ref[...]) + bfg_ref[...]
    log_f = -_softplus(-pre) * (1.0 / GLA_TAU)
    row = lax.broadcasted_iota(jnp.int32, log_f.shape, 0) % C
    bcum = log_f
    d = 1
    while d < C:
        bcum = bcum + jnp.where(row >= d, pltpu.roll(bcum, d, 0), 0.0)
        d *= 2

    q = q_ref[...] * (GLA_HEAD_K ** -0.5)
    k = k_ref[...]
    v = v_ref[...]
    lane = lax.broadcasted_iota(jnp.int32, (C, LANES), 1)
    causal = (lax.broadcasted_iota(jnp.int32, (C, C), 0) >= lax.broadcasted_iota(jnp.int32, (C, C), 1))
    head_lanes = (lane < GLA_HEAD_K, lane >= GLA_HEAD_K)

    def stage(c, p):
        rs = slice(c * C, (c + 1) * C)
        ls = slice(p * LANES, (p + 1) * LANES)
        b = bcum[rs, ls]
        b_last = b[C - 1:C, :]
        kk = k[rs, ls]
        q_dec = q[rs, ls] * jnp.exp(b)
        k_dec = kk * jnp.exp(-b)
        k_last = kk * jnp.exp(b_last - b)
        decay = jnp.exp(b_last)
        for hh in range(2):
            h = 2 * p + hh
            m = head_lanes[hh]
            qd = jnp.where(m, q_dec, 0.0).astype(BF16)
            kd = jnp.where(m, k_dec, 0.0).astype(BF16)
            kl = jnp.where(m, k_last, 0.0).astype(BF16)
            vh = v[rs, h * HEAD_DIM:(h + 1) * HEAD_DIM]
            attn = jnp.where(causal, _dot_t(qd, kd), 0.0)
            st = state_ref[h]
            o = _dot(attn.astype(BF16), vh.astype(BF16)) + _dot_t(qd, st.astype(BF16))
            acc_ref[rs, h * HEAD_DIM:(h + 1) * HEAD_DIM] = o
            state_ref[h] = decay * st + _dot(vh.T.astype(BF16), kl)

    def finish():
        return _head_norm_gate(acc_ref[...], z_ref[...], g_ref[...])

    stages = [functools.partial(stage, c, p) for c in range(R // C) for p in range(2)]
    return stages, finish


def _lru_mix(x_ref, z_ref, cw_ref, cb_ref, wa_ref, ba_ref, wx_ref, bx_ref, lam_ref, g_ref,
             hist_ref, h_ref, acc_ref):
    R = x_ref.shape[0]
    HIST = hist_ref.shape[0]

    x = x_ref[...]
    xe = jnp.concatenate([hist_ref[...], x], axis=0)
    xc = cb_ref[...] + cw_ref[LRU_CONV - 1:LRU_CONV, :] * x
    for kk in range(LRU_CONV - 1):
        sh = LRU_CONV - 1 - kk
        xc = xc + cw_ref[kk:kk + 1, :] * pltpu.roll(xe, sh, 0)[HIST:HIST + R]
    hist_ref[...] = x[R - HIST:R]

    xb = xc.astype(BF16)
    ra, rx = [], []
    for h in range(N_HEADS):
        xh = xb[:, h * HEAD_DIM:(h + 1) * HEAD_DIM]
        ra.append(_dot(xh, wa_ref[h]))
        rx.append(_dot(xh, wx_ref[h]))
    r = jax.nn.sigmoid(jnp.concatenate(ra, axis=-1) + ba_ref[...])
    ig = jax.nn.sigmoid(jnp.concatenate(rx, axis=-1) + bx_ref[...])
    log_a = -LRU_C * r * _softplus(-lam_ref[...])
    a = jnp.exp(log_a)
    u = jnp.sqrt(-_expm1(2.0 * log_a)) * (ig * xc)

    G = R // SUBLANES
    a3 = a.reshape(G, SUBLANES, W_BR)
    u3 = u.reshape(G, SUBLANES, W_BR)
    row = lax.broadcasted_iota(jnp.int32, a3.shape, 1)
    d = 1
    while d < SUBLANES:
        keep = row >= d
        u3 = jnp.where(keep, a3 * pltpu.roll(u3, d, 1) + u3, u3)
        a3 = jnp.where(keep, a3 * pltpu.roll(a3, d, 1), a3)
        d *= 2
    carry = h_ref[...]
    for gi in range(G):
        hg = a3[gi] * carry + u3[gi]
        acc_ref[gi * SUBLANES:(gi + 1) * SUBLANES, :] = hg
        carry = hg[SUBLANES - 1:SUBLANES, :]
    h_ref[...] = carry
    return _head_norm_gate(acc_ref[...], z_ref[...], g_ref[...])


def _conf_mix(v_ref, glu_ref, z_ref, dw_ref, db_ref, lg_ref, lb_ref, pw_ref, pb_ref, g_ref, hist_ref, beside):
    R = v_ref.shape[0]
    HIST = hist_ref.shape[0]

    y = v_ref[...] * jax.nn.sigmoid(glu_ref[...])
    ye = jnp.concatenate([hist_ref[...], y], axis=0)
    acc = db_ref[...]
    for r in range(SUBLANES):
        yr = ye if r == 0 else pltpu.roll(ye, r, 0)
        for q in range((CONF_KERNEL - 1 - r) // SUBLANES + 1):
            kk = CONF_KERNEL - 1 - (SUBLANES * q + r)
            start = HIST - SUBLANES * q
            acc = acc + dw_ref[kk:kk + 1, :] * yr[start:start + R]
        beside()
    hist_ref[...] = y[R - HIST:R]

    mu = jnp.mean(acc, axis=-1, keepdims=True)
    xc = acc - mu
    var = jnp.mean(xc * xc, axis=-1, keepdims=True)
    yn = xc * lax.rsqrt(var + EPS) * lg_ref[...] + lb_ref[...]
    o = _dot(_silu(yn).astype(BF16), pw_ref[...]) + pb_ref[...]
    return _head_norm_gate(o, z_ref[...], g_ref[...])


def _cmp_kernel(kc_ref, vc_ref, posk_ref, w1k_ref, w2k_ref, posv_ref, w1v_ref, w2v_ref, ko_ref, vo_ref):
    S = kc_ref.shape[0]
    NB = S // CMP_STRIDE
    HALF = CMP_LEN // 2

    def compress(z_ref, pos_ref, w1_ref, w2_ref):
        lo = jnp.zeros((NB, w1_ref.shape[2]), F32)
        hi = jnp.zeros((NB, w1_ref.shape[2]), F32)
        for l in range(HALF):
            grp = z_ref[pl.ds(l, NB, stride=CMP_STRIDE), :]
            lo = lo + _dot((grp + pos_ref[l:l + 1, :]).astype(BF16), w1_ref[l])
            hi = hi + _dot((grp + pos_ref[HALF + l:HALF + l + 1, :]).astype(BF16), w1_ref[HALF + l])
        hid = lo + pltpu.roll(hi, NB - 1, 0)
        blk = lax.broadcasted_iota(jnp.int32, hid.shape, 0)
        hid = jnp.where(blk < NB - 1, hid, 0.0)
        return _dot(_silu(hid).astype(BF16), w2_ref[...])

    ko_ref[...] = compress(kc_ref, posk_ref, w1k_ref, w2k_ref)
    vo_ref[...] = compress(vc_ref, posv_ref, w1v_ref, w2v_ref)


def _nsa_compress(proj, posk, w1k, w2k, posv, w1v, w2v, l, B, S):
    NB = S // CMP_STRIDE

    def seg(name):
        off, wp = _DST[name]
        return pl.BlockSpec((S, wp), lambda b: (b, off // wp))

    out = jax.ShapeDtypeStruct((B, NB, HEAD_DIM), F32)
    ospec = pl.BlockSpec((None, NB, HEAD_DIM), lambda b: (b, 0, 0))
    return pl.pallas_call(
        _cmp_kernel,
        out_shape=(out, out),
        grid=(B,),
        in_specs=[seg('nsa_kc'), seg('nsa_vc'), _layer_spec(posk, l), _layer_spec(w1k, l),
                  _layer_spec(w2k, l), _layer_spec(posv, l), _layer_spec(w1v, l), _layer_spec(w2v, l)],
        out_specs=(ospec, ospec),
        compiler_params=_params("parallel"),
        name="nsa_compress",
    )(proj, proj, posk, w1k, w2k, posv, w1v, w2v)


def _nsa_kernel(q_ref, gt_ref, z_ref, kc_ref, vc_ref, ks_ref, vs_ref, kw_ref, vw_ref, g_ref, o_ref,
                m_ref, acc_ref, ocmp_ref, sel_ref):
    TQ = q_ref.shape[0]
    S = ks_ref.shape[0]
    NB = kc_ref.shape[0]
    H, Dh = N_HEADS, HEAD_DIM
    scale = Dh ** -0.5
    i = pl.program_id(1)
    t0 = i * TQ

    q = q_ref[...]
    q4 = jnp.concatenate([q[:, h * Dh:(h + 1) * Dh] for h in range(H)], axis=0).astype(BF16)

    n_sel = S // SLC_LEN
    RANK_STEP = 8

    def selection_stages():
        st = {}

        def compressed():
            t_c = t0 + lax.broadcasted_iota(jnp.int32, (TQ, NB), 0)
            n_c = lax.broadcasted_iota(jnp.int32, (TQ, NB), 1)
            cmask = t_c >= n_c * CMP_STRIDE + (CMP_LEN - 1)
            cmask4 = jnp.concatenate([cmask] * H, axis=0)
            p4 = _masked_softmax(_dot_t(q4, kc_ref[...].astype(BF16)) * scale, cmask4)
            st['p4b'] = p4.astype(BF16)
            ocmp_ref[...] = _dot(st['p4b'], vc_ref[...].astype(BF16))

        def importance():
            jj = lax.broadcasted_iota(jnp.int32, (n_sel, NB), 0)
            nn = lax.broadcasted_iota(jnp.int32, (n_sel, NB), 1)
            ov = ((nn * CMP_STRIDE < (jj + 1) * SLC_LEN) & (nn * CMP_STRIDE + CMP_LEN > jj * SLC_LEN)
                  & (nn < NB - 1))
            po = _dot_t(jnp.where(ov, 1.0, 0.0).astype(BF16), st['p4b'])
            imp = po[:, 0:TQ]
            for h in range(1, H):
                imp = imp + po[:, h * TQ:(h + 1) * TQ]
            j = lax.broadcasted_iota(jnp.int32, (n_sel, TQ), 0)
            cur = (t0 + lax.broadcasted_iota(jnp.int32, (n_sel, TQ), 1)) // SLC_LEN
            forced = (j == 0) | (j == cur) | (j == cur - 1)
            st['j'] = j
            st['val'] = jnp.where(j > cur, -jnp.inf, jnp.where(forced, FORCED_SCORE, imp))
            st['rank'] = jnp.zeros((n_sel, TQ), F32)

        def rank_part(lo):
            val, j = st['val'], st['j']
            for ii in range(lo, lo + RANK_STEP):
                vi = val[ii:ii + 1, :]
                beats = jnp.where(vi > val, 1.0, jnp.where(vi == val, jnp.where(j > ii, 1.0, 0.0), 0.0))
                st['rank'] = st['rank'] + beats

        def select():
            sel_t = jnp.where(st['rank'] < SLC_TOPN, 1.0, 0.0)
            sel_ref[...] = jnp.concatenate([sel_t, jnp.zeros((LANES - n_sel, TQ), F32)], axis=0).T.astype(BF16)

        return ([compressed, importance] + [functools.partial(rank_part, lo) for lo in range(0, n_sel, RANK_STEP)]
                + [select])

    t_k = t0 + lax.broadcasted_iota(jnp.int32, (TQ, TQ), 0)
    k_i = lax.broadcasted_iota(jnp.int32, (TQ, TQ), 1)
    ones_half = jnp.ones((TQ, Dh), BF16)
    neg_inf = jnp.full((TQ, TQ), -jnp.inf, F32)
    scale_log2e = scale * LOG2_E

    def reset():
        m_ref[...] = jnp.full(m_ref.shape, -jnp.inf, F32)
        acc_ref[...] = jnp.zeros(acc_ref.shape, F32)

    def attend_chunk(k_ref, v_ref, c, bias, beside=lambda: None):
        k0 = pl.multiple_of(c * TQ, TQ)
        kb = k_ref[pl.ds(k0, TQ), :].astype(BF16)
        vb = jnp.concatenate([v_ref[pl.ds(k0, TQ), :].astype(BF16), ones_half], axis=-1)
        s4 = _dot_t(q4, kb)
        for h in range(H):
            rs = slice(h * TQ, (h + 1) * TQ)
            s = s4[rs] if bias is None else s4[rs] + bias
            m_old = m_ref[rs]
            m_new = jnp.maximum(m_old, jnp.max(s, axis=-1, keepdims=True))
            m_safe = jnp.where(m_new == -jnp.inf, 0.0, m_new)
            p = jnp.exp2((s - jnp.concatenate([m_safe] * (TQ // LANES), axis=-1)) * scale_log2e)
            alpha = jnp.exp2((m_old - m_safe) * scale_log2e)
            acc_ref[rs] = jnp.concatenate([alpha, alpha], axis=-1) * acc_ref[rs] + _dot(p.astype(BF16), vb)
            m_ref[rs] = m_new
            beside()

    def finish():
        acc = acc_ref[...]
        return acc[:, :Dh] / jnp.maximum(acc[:, Dh:], 1e-30)

    reset()
    n_back = WIN // TQ

    def win_chunk(back, beside):
        c = i - back
        if 0 < back < n_back:
            bias = None
        else:
            dist = t_k - (c * TQ + k_i)
            bias = jnp.where((dist >= 0) & (dist < WIN), 0.0, neg_inf)
        attend_chunk(kw_ref, vw_ref, c, bias, beside)

    for first in range(n_back, -1, -1):
        def win_chunks(first=first):
            stages = selection_stages()

            def next_stage():
                if stages:
                    stages.pop(0)()

            for back in range(first, -1, -1):
                win_chunk(back, next_stage)
            while stages:
                next_stage()
        pl.when(jnp.minimum(i, n_back) == first)(win_chunks)
    o_win4 = finish()
    sel = sel_ref[...]

    eb = lax.broadcasted_iota(jnp.int32, (LANES, TQ), 0)
    ek = lax.broadcasted_iota(jnp.int32, (LANES, TQ), 1) // SLC_LEN

    def sel_keys(c):
        expand = jnp.where(eb == ek + c * (TQ // SLC_LEN), 1.0, 0.0).astype(BF16)
        return _dot(sel, expand) > 0.5

    reset()

    def slc_chunk(c):
        attend_chunk(ks_ref, vs_ref, c, jnp.where(sel_keys(c), 0.0, neg_inf))

    def slc_diagonal():
        attend_chunk(ks_ref, vs_ref, i, jnp.where(sel_keys(i) & (t0 + k_i <= t_k), 0.0, neg_inf))

    def slc_group(cg, carry):
        for u in range(SLC_GROUP):
            slc_chunk(SLC_GROUP * cg + u)
        return carry

    lax.fori_loop(0, i // SLC_GROUP, slc_group, 0)
    for rem in range(SLC_GROUP):
        def slc_tail(rem=rem):
            for u in range(rem, 0, -1):
                slc_chunk(i - u)
            slc_diagonal()
        pl.when(i % SLC_GROUP == rem)(slc_tail)
    o_slc4 = finish()

    o_cmp4 = ocmp_ref[...]
    gs = jax.nn.sigmoid(gt_ref[...])
    outs = []
    for h in range(H):
        rs = slice(h * TQ, (h + 1) * TQ)
        outs.append(gs[:, 3 * h:3 * h + 1] * o_cmp4[rs] + gs[:, 3 * h + 1:3 * h + 2] * o_slc4[rs]
                    + gs[:, 3 * h + 2:3 * h + 3] * o_win4[rs])
    o_ref[...] = _head_norm_gate(jnp.concatenate(outs, axis=-1), z_ref[...], g_ref[...])


def _nsa(proj, k_cmp, v_cmp, bg, l, B, S, TQ=256):
    assert WIN % TQ == 0 and TQ % SLC_LEN == 0 and TQ % LANES == 0
    nt = S // TQ
    T = B * S
    NB = k_cmp.shape[1]

    def seq(name):
        off, wp = _DST[name]
        return pl.BlockSpec((S, wp), lambda b, i: (b, off // wp))

    cspec = pl.BlockSpec((None, NB, HEAD_DIM), lambda b, i: (b, 0, 0))
    return pl.pallas_call(
        _nsa_kernel,
        out_shape=jax.ShapeDtypeStruct((T, W_BR), BF16),
        grid=(B, nt),
        in_specs=[_seg_spec('nsa_q', TQ, nt), _seg_spec('nsa_g', TQ, nt), _seg_spec('nsa_z', TQ, nt),
                  cspec, cspec, seq('nsa_ks'), seq('nsa_vs'), seq('nsa_kw'), seq('nsa_vw'),
                  _layer_spec(bg, l, 2)],
        out_specs=pl.BlockSpec((TQ, W_BR), lambda b, i: (b * nt + i, 0)),
        scratch_shapes=[pltpu.VMEM((N_HEADS * TQ, LANES), F32),
                        pltpu.VMEM((N_HEADS * TQ, 2 * HEAD_DIM), F32),
                        pltpu.VMEM((N_HEADS * TQ, HEAD_DIM), F32), pltpu.VMEM((TQ, LANES), BF16)],
        compiler_params=_params("parallel", "arbitrary"),
        name="nsa_attention",
    )(proj, proj, proj, k_cmp, v_cmp, proj, proj, proj, proj, bg)


def kernel(x, pre_norm_g, w_in, gla_w_fg2, gla_b_fg2, lru_conv_w, lru_conv_b, lru_w_a, lru_b_a, lru_w_x,
           lru_b_x, lru_lambda, nsa_cmp_pos_k, nsa_cmp_w1_k, nsa_cmp_w2_k, nsa_cmp_pos_v, nsa_cmp_w1_v,
           nsa_cmp_w2_v, conf_dw_w, conf_dw_b, conf_ln_g, conf_ln_b, conf_pw_w, conf_pw_b, branch_norm_g,
           w_out, post_norm_g):
    B, S, D = x.shape
    L = w_in.shape[0]
    T = B * S
    x2 = x.reshape(T, D)

    def rows(a):
        return a.reshape(L, 1, -1)

    w_in_t = _relayout_w_in(w_in)
    lowrank = gla_w_fg2.shape[1]
    w1k = nsa_cmp_w1_k.reshape(L, CMP_LEN, HEAD_DIM, -1).astype(BF16)
    w1v = nsa_cmp_w1_v.reshape(L, CMP_LEN, HEAD_DIM, -1).astype(BF16)
    w2k = nsa_cmp_w2_k.astype(BF16)
    w2v = nsa_cmp_w2_v.astype(BF16)
    bg = branch_norm_g.reshape(L, N_MIXERS, 1, W_BR)
    p = dict(
        bg=bg, pre_g=rows(pre_norm_g), post_g=rows(post_norm_g), w_out=w_out.astype(BF16),
        wfg=jnp.pad(gla_w_fg2, ((0, 0), (0, LANES - lowrank), (0, 0))).astype(BF16), bfg=rows(gla_b_fg2),
        lru_cw=lru_conv_w, lru_cb=rows(lru_conv_b), wa=lru_w_a.astype(BF16), lru_ba=rows(lru_b_a),
        wx=lru_w_x.astype(BF16), lru_bx=rows(lru_b_x), lru_lam=rows(lru_lambda),
        conf_dw=conf_dw_w, conf_db=rows(conf_dw_b), conf_lg=rows(conf_ln_g), conf_lb=rows(conf_ln_b),
        pw=conf_pw_w.astype(BF16), conf_pb=rows(conf_pw_b))

    hn = _prenorm(x2, p['pre_g'], 0)
    for l in range(L):
        proj = _in_proj(hn, w_in_t, l)
        k_cmp, v_cmp = _nsa_compress(proj, nsa_cmp_pos_k, w1k, w2k, nsa_cmp_pos_v, w1v, w2v, l, B, S)
        m_c = _nsa(proj, k_cmp, v_cmp, bg, l, B, S)
        if l + 1 < L:
            x2, hn = _mix_out(proj, m_c, x2, p, l, B, S, emit_hn=True)
        else:
            x2 = _mix_out(proj, m_c, x2, p, l, B, S, emit_hn=False)
    return x2.reshape(B, S, D)
```

```python
import functools

import jax
import jax.numpy as jnp
from jax import lax
from jax.experimental import pallas as pl
from jax.experimental.pallas import tpu as pltpu

F32 = jnp.float32
BF16 = jnp.bfloat16

D_MODEL = 2048
N_HEADS = 4
HEAD_DIM = 128
W_BR = N_HEADS * HEAD_DIM
N_MIXERS = 4
GLA_HEAD_K = 64
GLA_TAU = 16.0
GLA_CHUNK = 64
LRU_C = 8.0
LRU_CONV = 4
CMP_LEN = 32
CMP_STRIDE = 16
SLC_LEN = 64
SLC_TOPN = 16
SLC_GROUP = 4
WIN = 512
FORCED_SCORE = 1e3
CONF_KERNEL = 31
CONF_HIST = 32
EPS = 1e-6
LOG2_E = 1.4426950408889634
SUBLANES = 8
LANES = 128
VMEM_LIMIT = 48 * 1024 * 1024

_SEGS = (('gla_q', 256), ('gla_k', 256), ('gla_v', 512), ('gla_fg', 16), ('gla_z', 512),
         ('lru_x', 512), ('lru_z', 512),
         ('nsa_q', 512), ('nsa_kc', 128), ('nsa_vc', 128), ('nsa_ks', 128), ('nsa_vs', 128),
         ('nsa_kw', 128), ('nsa_vw', 128), ('nsa_g', 12), ('nsa_z', 512),
         ('conv_v', 512), ('conv_glu', 512), ('conv_z', 512))
_ORDER = ('gla_v', 'gla_z', 'lru_x', 'lru_z', 'nsa_q', 'nsa_z', 'conv_v', 'conv_glu', 'conv_z',
          'gla_q', 'gla_k', 'gla_fg', 'nsa_kc', 'nsa_vc', 'nsa_ks', 'nsa_vs', 'nsa_kw', 'nsa_vw',
          'nsa_g')


def _layout():
    src, off = {}, 0
    for name, w in _SEGS:
        src[name] = (off, w)
        off += w
    dst, off = {}, 0
    for name in _ORDER:
        w = src[name][1]
        wp = -(-w // LANES) * LANES
        assert off % wp == 0
        dst[name] = (off, wp)
        off += wp
    return src, dst, off


_SRC, _DST, D_PROJ_PAD = _layout()


def _relayout_kernel(off_ref, valid_ref, w_ref, o_ref):
    del off_ref
    rows = lax.broadcasted_iota(jnp.int32, o_ref.shape[1:], 0)
    keep = rows < valid_ref[pl.program_id(0)]
    for l in range(o_ref.shape[0]):
        o_ref[l] = jnp.where(keep, w_ref[:, l, :], 0.0).astype(BF16)


def _relayout_w_in(w_in):
    L, D, NP = w_in.shape
    wt = jnp.transpose(w_in, (2, 0, 1))
    offs, valid = [], []
    for name in _ORDER:
        o, w = _SRC[name]
        for b in range(_DST[name][1] // LANES):
            offs.append(o + b * LANES)
            valid.append(min(LANES, w - b * LANES))
    assert max(offs) + LANES <= NP and min(valid) > 0
    grid_spec = pltpu.PrefetchScalarGridSpec(
        num_scalar_prefetch=2,
        grid=(len(offs),),
        in_specs=[pl.BlockSpec((pl.Element(LANES), pl.Element(L), pl.Element(D)),
                               lambda j, off, valid: (off[j], 0, 0))],
        out_specs=pl.BlockSpec((L, LANES, D), lambda j, off, valid: (0, j, 0)))
    return pl.pallas_call(
        _relayout_kernel,
        out_shape=jax.ShapeDtypeStruct((L, D_PROJ_PAD, D), BF16),
        grid_spec=grid_spec,
        compiler_params=_params("parallel"),
        name="relayout_w_in",
    )(jnp.array(offs, jnp.int32), jnp.array(valid, jnp.int32), wt)


def _dot(a, b):
    return jnp.dot(a, b, preferred_element_type=F32)


def _dot_t(a, b):
    return lax.dot_general(a, b, (((1,), (1,)), ((), ())), preferred_element_type=F32)


def _softplus(x):
    return jnp.maximum(x, 0.0) + jnp.log1p(jnp.exp(-jnp.abs(x)))


def _expm1(x):
    return jnp.tanh(0.5 * x) * (jnp.exp(x) + 1.0)


def _silu(x):
    return x * jax.nn.sigmoid(x)


def _rmsnorm(x, g):
    return x * lax.rsqrt(jnp.mean(x * x, axis=-1, keepdims=True) + EPS) * g


def _masked_softmax(s, mask):
    s = jnp.where(mask, s, -jnp.inf)
    m = jnp.max(s, axis=-1, keepdims=True)
    m = jnp.where(jnp.isfinite(m), m, 0.0)
    p = jnp.exp(s - m)
    return p / jnp.maximum(jnp.sum(p, axis=-1, keepdims=True), 1e-30)


def _head_norm_gate(o, z, g):
    outs = []
    for h in range(N_HEADS):
        oh = o[:, h * HEAD_DIM:(h + 1) * HEAD_DIM]
        outs.append(oh * lax.rsqrt(jnp.mean(oh * oh, axis=-1, keepdims=True) + EPS))
    on = jnp.concatenate(outs, axis=-1) * g
    return (on * _silu(z)).astype(BF16)


def _layer_spec(arr, *lead):
    rest = arr.shape[len(lead):]
    idx = tuple(lead) + (0,) * len(rest)
    return pl.BlockSpec((None,) * len(lead) + rest, lambda *_: idx)


def _seg_spec(name, rows, nt):
    off, wp = _DST[name]
    cb = off // wp
    return pl.BlockSpec((rows, wp), lambda b, i: (b * nt + i, cb))


def _params(*sem):
    return pltpu.CompilerParams(dimension_semantics=sem, vmem_limit_bytes=VMEM_LIMIT)


def _prenorm_kernel(x_ref, g_ref, o_ref):
    o_ref[...] = _rmsnorm(x_ref[...], g_ref[...]).astype(BF16)


def _prenorm(x2, g, l, tm=512):
    T, D = x2.shape
    return pl.pallas_call(
        _prenorm_kernel,
        out_shape=jax.ShapeDtypeStruct((T, D), BF16),
        grid=(T // tm,),
        in_specs=[pl.BlockSpec((tm, D), lambda i: (i, 0)), _layer_spec(g, l)],
        out_specs=pl.BlockSpec((tm, D), lambda i: (i, 0)),
        compiler_params=_params("parallel"),
        name="prenorm",
    )(x2, g)


def _in_proj_kernel(h_ref, wt_ref, o_ref):
    o_ref[...] = _dot_t(h_ref[...], wt_ref[...])


def _in_proj(hn, wt, l, tm=1024, tn=2048):
    T, D = hn.shape
    NP = wt.shape[1]
    return pl.pallas_call(
        _in_proj_kernel,
        out_shape=jax.ShapeDtypeStruct((T, NP), F32),
        grid=(T // tm, NP // tn),
        in_specs=[pl.BlockSpec((tm, D), lambda i, j: (i, 0)),
                  pl.BlockSpec((None, tn, D), lambda i, j: (l, j, 0))],
        out_specs=pl.BlockSpec((tm, tn), lambda i, j: (i, j)),
        compiler_params=_params("parallel", "arbitrary"),
        name="in_proj",
    )(hn, wt)


N_GLA_IN, N_LRU_IN, N_CONF_IN = 8, 10, 10


def _mix_out_kernel(*refs, emit_hn):
    refs = list(refs)
    gla_in = [refs.pop(0) for _ in range(N_GLA_IN)]
    lru_in = [refs.pop(0) for _ in range(N_LRU_IN)]
    conf_in = [refs.pop(0) for _ in range(N_CONF_IN)]
    mc_ref, w_ref, x_ref, pg_ref = [refs.pop(0) for _ in range(4)]
    gn_ref = refs.pop(0) if emit_hn else None
    o_ref = refs.pop(0)
    hn_ref = refs.pop(0) if emit_hn else None
    gla_state, gla_acc, lru_hist, lru_h, lru_acc, conf_hist = refs

    @pl.when(pl.program_id(1) == 0)
    def _():
        for ref in (gla_state, lru_hist, lru_h, conf_hist):
            ref[...] = jnp.zeros(ref.shape, F32)

    gla_stages, gla_finish = _gla_stages(*gla_in, gla_state, gla_acc)

    def next_gla_stage():
        if gla_stages:
            gla_stages.pop(0)()

    m_b = _lru_mix(*lru_in, lru_hist, lru_h, lru_acc)
    m_d = _conf_mix(*conf_in, conf_hist, next_gla_stage)
    while gla_stages:
        next_gla_stage()
    mixed = jnp.concatenate([gla_finish(), m_b, mc_ref[...], m_d], axis=-1)
    x_new = x_ref[...] + _rmsnorm(_dot(mixed, w_ref[...]), pg_ref[...])
    o_ref[...] = x_new
    if emit_hn:
        hn_ref[...] = _rmsnorm(x_new, gn_ref[...]).astype(BF16)


def _mix_out(proj, m_c, x2, p, l, B, S, emit_hn, R=256):
    nt = S // R
    T, D = x2.shape
    bg = p['bg']

    def tile(width):
        return pl.BlockSpec((R, width), lambda b, i: (b * nt + i, 0))

    def seg(*names):
        return [_seg_spec(n, R, nt) for n in names]

    def layer(*names):
        return [_layer_spec(p[n], l) for n in names]

    gla = seg('gla_q', 'gla_k', 'gla_v', 'gla_fg', 'gla_z') + layer('wfg', 'bfg') + [_layer_spec(bg, l, 0)]
    lru = (seg('lru_x', 'lru_z') + layer('lru_cw', 'lru_cb', 'wa', 'lru_ba', 'wx', 'lru_bx', 'lru_lam')
           + [_layer_spec(bg, l, 1)])
    conf = (seg('conv_v', 'conv_glu', 'conv_z') + layer('conf_dw', 'conf_db', 'conf_lg', 'conf_lb', 'pw', 'conf_pb')
            + [_layer_spec(bg, l, 3)])
    assert (len(gla), len(lru), len(conf)) == (N_GLA_IN, N_LRU_IN, N_CONF_IN)
    in_specs = gla + lru + conf + [tile(W_BR), _layer_spec(p['w_out'], l), tile(D), _layer_spec(p['post_g'], l)]
    args = ([proj] * 5 + [p['wfg'], p['bfg'], bg]
            + [proj] * 2 + [p[n] for n in ('lru_cw', 'lru_cb', 'wa', 'lru_ba', 'wx', 'lru_bx', 'lru_lam')] + [bg]
            + [proj] * 3 + [p[n] for n in ('conf_dw', 'conf_db', 'conf_lg', 'conf_lb', 'pw', 'conf_pb')] + [bg]
            + [m_c, p['w_out'], x2, p['post_g']])
    out_shape = jax.ShapeDtypeStruct((T, D), F32)
    out_specs = tile(D)
    if emit_hn:
        in_specs.append(_layer_spec(p['pre_g'], l + 1))
        args.append(p['pre_g'])
        out_shape = (out_shape, jax.ShapeDtypeStruct((T, D), BF16))
        out_specs = (tile(D), tile(D))
    return pl.pallas_call(
        functools.partial(_mix_out_kernel, emit_hn=emit_hn),
        out_shape=out_shape,
        grid=(B, nt),
        in_specs=in_specs,
        out_specs=out_specs,
        scratch_shapes=[pltpu.VMEM((N_HEADS, HEAD_DIM, LANES), F32), pltpu.VMEM((R, W_BR), F32),
                        pltpu.VMEM((SUBLANES, W_BR), F32), pltpu.VMEM((1, W_BR), F32),
                        pltpu.VMEM((R, W_BR), F32), pltpu.VMEM((CONF_HIST, W_BR), F32)],
        compiler_params=_params("parallel", "arbitrary"),
        name="mix_out",
    )(*args)


def _gla_stages(q_ref, k_ref, v_ref, fg_ref, z_ref, wfg_ref, bfg_ref, g_ref, state_ref, acc_ref):
    R = q_ref.shape[0]
    C = GLA_CHUNK

    pre = _dot(fg_ref[...].astype(BF16), wfg_ref[...]) + bfg_ref[...]
    log_f = -_softplus(-pre) * (1.0 / GLA_TAU)
    row = lax.broadcasted_iota(jnp.int32, log_f.shape, 0) % C
    bcum = log_f
    d = 1
    while d < C:
        bcum = bcum + jnp.where(row >= d, pltpu.roll(bcum, d, 0), 0.0)
        d *= 2

    q = q_ref[...] * (GLA_HEAD_K ** -0.5)
    k = k_ref[...]
    v = v_ref[...]
    lane = lax.broadcasted_iota(jnp.int32, (C, LANES), 1)
    causal = (lax.broadcasted_iota(jnp.int32, (C, C), 0) >= lax.broadcasted_iota(jnp.int32, (C, C), 1))
    head_lanes = (lane < GLA_HEAD_K, lane >= GLA_HEAD_K)

    def stage(c, p):
        rs = slice(c * C, (c + 1) * C)
        ls = slice(p * LANES, (p + 1) * LANES)
        b = bcum[rs, ls]
        b_last = b[C - 1:C, :]
        kk = k[rs, ls]
        q_dec = q[rs, ls] * jnp.exp(b)
        k_dec = kk * jnp.exp(-b)
        k_last = kk * jnp.exp(b_last - b)
        decay = jnp.exp(b_last)
        for hh in range(2):
            h = 2 * p + hh
            m = head_lanes[hh]
            qd = jnp.where(m, q_dec, 0.0).astype(BF16)
            kd = jnp.where(m, k_dec, 0.0).astype(BF16)
            kl = jnp.where(m, k_last, 0.0).astype(BF16)
            vh = v[rs, h * HEAD_DIM:(h + 1) * HEAD_DIM]
            attn = jnp.where(causal, _dot_t(qd, kd), 0.0)
            st = state_ref[h]
            o = _dot(attn.astype(BF16), vh.astype(BF16)) + _dot_t(qd, st.astype(BF16))
            acc_ref[rs, h * HEAD_DIM:(h + 1) * HEAD_DIM] = o
            state_ref[h] = decay * st + _dot(vh.T.astype(BF16), kl)

    def finish():
        return _head_norm_gate(acc_ref[...], z_ref[...], g_ref[...])

    stages = [functools.partial(stage, c, p) for c in range(R // C) for p in range(2)]
    return stages, finish


def _lru_mix(x_ref, z_ref, cw_ref, cb_ref, wa_ref, ba_ref, wx_ref, bx_ref, lam_ref, g_ref,
             hist_ref, h_ref, acc_ref):
    R = x_ref.shape[0]
    HIST = hist_ref.shape[0]

    x = x_ref[...]
    xe = jnp.concatenate([hist_ref[...], x], axis=0)
    xc = cb_ref[...] + cw_ref[LRU_CONV - 1:LRU_CONV, :] * x
    for kk in range(LRU_CONV - 1):
        sh = LRU_CONV - 1 - kk
        xc = xc + cw_ref[kk:kk + 1, :] * pltpu.roll(xe, sh, 0)[HIST:HIST + R]
    hist_ref[...] = x[R - HIST:R]

    xb = xc.astype(BF16)
    ra, rx = [], []
    for h in range(N_HEADS):
        xh = xb[:, h * HEAD_DIM:(h + 1) * HEAD_DIM]
        ra.append(_dot(xh, wa_ref[h]))
        rx.append(_dot(xh, wx_ref[h]))
    r = jax.nn.sigmoid(jnp.concatenate(ra, axis=-1) + ba_ref[...])
    ig = jax.nn.sigmoid(jnp.concatenate(rx, axis=-1) + bx_ref[...])
    log_a = -LRU_C * r * _softplus(-lam_ref[...])
    a = jnp.exp(log_a)
    u = jnp.sqrt(-_expm1(2.0 * log_a)) * (ig * xc)

    G = R // SUBLANES
    a3 = a.reshape(G, SUBLANES, W_BR)
    u3 = u.reshape(G, SUBLANES, W_BR)
    row = lax.broadcasted_iota(jnp.int32, a3.shape, 1)
    d = 1
    while d < SUBLANES:
        keep = row >= d
        u3 = jnp.where(keep, a3 * pltpu.roll(u3, d, 1) + u3, u3)
        a3 = jnp.where(keep, a3 * pltpu.roll(a3, d, 1), a3)
        d *= 2
    carry = h_ref[...]
    for gi in range(G):
        hg = a3[gi] * carry + u3[gi]
        acc_ref[gi * SUBLANES:(gi + 1) * SUBLANES, :] = hg
        carry = hg[SUBLANES - 1:SUBLANES, :]
    h_ref[...] = carry
    return _head_norm_gate(acc_ref[...], z_ref[...], g_ref[...])


def _conf_mix(v_ref, glu_ref, z_ref, dw_ref, db_ref, lg_ref, lb_ref, pw_ref, pb_ref, g_ref, hist_ref, beside):
    R = v_ref.shape[0]
    HIST = hist_ref.shape[0]

    y = v_ref[...] * jax.nn.sigmoid(glu_ref[...])
    ye = jnp.concatenate([hist_ref[...], y], axis=0)
    acc = db_ref[...]
    for r in range(SUBLANES):
        yr = ye if r == 0 else pltpu.roll(ye, r, 0)
        for q in range((CONF_KERNEL - 1 - r) // SUBLANES + 1):
            kk = CONF_KERNEL - 1 - (SUBLANES * q + r)
            start = HIST - SUBLANES * q
            acc = acc + dw_ref[kk:kk + 1, :] * yr[start:start + R]
        beside()
    hist_ref[...] = y[R - HIST:R]

    mu = jnp.mean(acc, axis=-1, keepdims=True)
    xc = acc - mu
    var = jnp.mean(xc * xc, axis=-1, keepdims=True)
    yn = xc * lax.rsqrt(var + EPS) * lg_ref[...] + lb_ref[...]
    o = _dot(_silu(yn).astype(BF16), pw_ref[...]) + pb_ref[...]
    return _head_norm_gate(o, z_ref[...], g_ref[...])


def _cmp_kernel(kc_ref, vc_ref, posk_ref, w1k_ref, w2k_ref, posv_ref, w1v_ref, w2v_ref, ko_ref, vo_ref):
    S = kc_ref.shape[0]
    NB = S // CMP_STRIDE
    HALF = CMP_LEN // 2

    def compress(z_ref, pos_ref, w1_ref, w2_ref):
        lo = jnp.zeros((NB, w1_ref.shape[2]), F32)
        hi = jnp.zeros((NB, w1_ref.shape[2]), F32)
        for l in range(HALF):
            grp = z_ref[pl.ds(l, NB, stride=CMP_STRIDE), :]
            lo = lo + _dot((grp + pos_ref[l:l + 1, :]).astype(BF16), w1_ref[l])
            hi = hi + _dot((grp + pos_ref[HALF + l:HALF + l + 1, :]).astype(BF16), w1_ref[HALF + l])
        hid = lo + pltpu.roll(hi, NB - 1, 0)
        blk = lax.broadcasted_iota(jnp.int32, hid.shape, 0)
        hid = jnp.where(blk < NB - 1, hid, 0.0)
        return _dot(_silu(hid).astype(BF16), w2_ref[...])

    ko_ref[...] = compress(kc_ref, posk_ref, w1k_ref, w2k_ref)
    vo_ref[...] = compress(vc_ref, posv_ref, w1v_ref, w2v_ref)


def _nsa_compress(proj, posk, w1k, w2k, posv, w1v, w2v, l, B, S):
    NB = S // CMP_STRIDE

    def seg(name):
        off, wp = _DST[name]
        return pl.BlockSpec((S, wp), lambda b: (b, off // wp))

    out = jax.ShapeDtypeStruct((B, NB, HEAD_DIM), F32)
    ospec = pl.BlockSpec((None, NB, HEAD_DIM), lambda b: (b, 0, 0))
    return pl.pallas_call(
        _cmp_kernel,
        out_shape=(out, out),
        grid=(B,),
        in_specs=[seg('nsa_kc'), seg('nsa_vc'), _layer_spec(posk, l), _layer_spec(w1k, l),
                  _layer_spec(w2k, l), _layer_spec(posv, l), _layer_spec(w1v, l), _layer_spec(w2v, l)],
        out_specs=(ospec, ospec),
        compiler_params=_params("parallel"),
        name="nsa_compress",
    )(proj, proj, posk, w1k, w2k, posv, w1v, w2v)


def _nsa_kernel(q_ref, gt_ref, z_ref, kc_ref, vc_ref, ks_ref, vs_ref, kw_ref, vw_ref, g_ref, o_ref,
                m_ref, acc_ref, ocmp_ref, sel_ref):
    TQ = q_ref.shape[0]
    S = ks_ref.shape[0]
    NB = kc_ref.shape[0]
    H, Dh = N_HEADS, HEAD_DIM
    scale = Dh ** -0.5
    i = pl.program_id(1)
    t0 = i * TQ

    q = q_ref[...]
    q4 = jnp.concatenate([q[:, h * Dh:(h + 1) * Dh] for h in range(H)], axis=0).astype(BF16)

    n_sel = S // SLC_LEN
    RANK_STEP = 8

    def selection_stages():
        st = {}

        def compressed():
            t_c = t0 + lax.broadcasted_iota(jnp.int32, (TQ, NB), 0)
            n_c = lax.broadcasted_iota(jnp.int32, (TQ, NB), 1)
            cmask = t_c >= n_c * CMP_STRIDE + (CMP_LEN - 1)
            cmask4 = jnp.concatenate([cmask] * H, axis=0)
            p4 = _masked_softmax(_dot_t(q4, kc_ref[...].astype(BF16)) * scale, cmask4)
            st['p4b'] = p4.astype(BF16)
            ocmp_ref[...] = _dot(st['p4b'], vc_ref[...].astype(BF16))

        def importance():
            jj = lax.broadcasted_iota(jnp.int32, (n_sel, NB), 0)
            nn = lax.broadcasted_iota(jnp.int32, (n_sel, NB), 1)
            ov = ((nn * CMP_STRIDE < (jj + 1) * SLC_LEN) & (nn * CMP_STRIDE + CMP_LEN > jj * SLC_LEN)
                  & (nn < NB - 1))
            po = _dot_t(jnp.where(ov, 1.0, 0.0).astype(BF16), st['p4b'])
            imp = po[:, 0:TQ]
            for h in range(1, H):
                imp = imp + po[:, h * TQ:(h + 1) * TQ]
            j = lax.broadcasted_iota(jnp.int32, (n_sel, TQ), 0)
            cur = (t0 + lax.broadcasted_iota(jnp.int32, (n_sel, TQ), 1)) // SLC_LEN
            forced = (j == 0) | (j == cur) | (j == cur - 1)
            st['j'] = j
            st['val'] = jnp.where(j > cur, -jnp.inf, jnp.where(forced, FORCED_SCORE, imp))
            st['rank'] = jnp.zeros((n_sel, TQ), F32)

        def rank_part(lo):
            val, j = st['val'], st['j']
            for ii in range(lo, lo + RANK_STEP):
                vi = val[ii:ii + 1, :]
                beats = jnp.where(vi > val, 1.0, jnp.where(vi == val, jnp.where(j > ii, 1.0, 0.0), 0.0))
                st['rank'] = st['rank'] + beats

        def select():
            sel_t = jnp.where(st['rank'] < SLC_TOPN, 1.0, 0.0)
            sel_ref[...] = jnp.concatenate([sel_t, jnp.zeros((LANES - n_sel, TQ), F32)], axis=0).T.astype(BF16)

        return ([compressed, importance] + [functools.partial(rank_part, lo) for lo in range(0, n_sel, RANK_STEP)]
                + [select])

    t_k = t0 + lax.broadcasted_iota(jnp.int32, (TQ, TQ), 0)
    k_i = lax.broadcasted_iota(jnp.int32, (TQ, TQ), 1)
    ones_half = jnp.ones((TQ, Dh), BF16)
    neg_inf = jnp.full((TQ, TQ), -jnp.inf, F32)
    scale_log2e = scale * LOG2_E

    def reset():
        m_ref[...] = jnp.full(m_ref.shape, -jnp.inf, F32)
        acc_ref[...] = jnp.zeros(acc_ref.shape, F32)

    def attend_chunk(k_ref, v_ref, c, bias, beside=lambda: None):
        k0 = pl.multiple_of(c * TQ, TQ)
        kb = k_ref[pl.ds(k0, TQ), :].astype(BF16)
        vb = jnp.concatenate([v_ref[pl.ds(k0, TQ), :].astype(BF16), ones_half], axis=-1)
        s4 = _dot_t(q4, kb)
        for h in range(H):
            rs = slice(h * TQ, (h + 1) * TQ)
            s = s4[rs] if bias is None else s4[rs] + bias
            m_old = m_ref[rs]
            m_new = jnp.maximum(m_old, jnp.max(s, axis=-1, keepdims=True))
            m_safe = jnp.where(m_new == -jnp.inf, 0.0, m_new)
            p = jnp.exp2((s - jnp.concatenate([m_safe] * (TQ // LANES), axis=-1)) * scale_log2e)
            alpha = jnp.exp2((m_old - m_safe) * scale_log2e)
            acc_ref[rs] = jnp.concatenate([alpha, alpha], axis=-1) * acc_ref[rs] + _dot(p.astype(BF16), vb)
            m_ref[rs] = m_new
            beside()

    def finish():
        acc = acc_ref[...]
        return acc[:, :Dh] / jnp.maximum(acc[:, Dh:], 1e-30)

    reset()
    n_back = WIN // TQ

    def win_chunk(back, beside):
        c = i - back
        if 0 < back < n_back:
            bias = None
        else:
            dist = t_k - (c * TQ + k_i)
            bias = jnp.where((dist >= 0) & (dist < WIN), 0.0, neg_inf)
        attend_chunk(kw_ref, vw_ref, c, bias, beside)

    for first in range(n_back, -1, -1):
        def win_chunks(first=first):
            stages = selection_stages()

            def next_stage():
                if stages:
                    stages.pop(0)()

            for back in range(first, -1, -1):
                win_chunk(back, next_stage)
            while stages:
                next_stage()
        pl.when(jnp.minimum(i, n_back) == first)(win_chunks)
    o_win4 = finish()
    sel = sel_ref[...]

    eb = lax.broadcasted_iota(jnp.int32, (LANES, TQ), 0)
    ek = lax.broadcasted_iota(jnp.int32, (LANES, TQ), 1) // SLC_LEN

    def sel_keys(c):
        expand = jnp.where(eb == ek + c * (TQ // SLC_LEN), 1.0, 0.0).astype(BF16)
        return _dot(sel, expand) > 0.5

    reset()

    def slc_chunk(c):
        attend_chunk(ks_ref, vs_ref, c, jnp.where(sel_keys(c), 0.0, neg_inf))

    def slc_diagonal():
        attend_chunk(ks_ref, vs_ref, i, jnp.where(sel_keys(i) & (t0 + k_i <= t_k), 0.0, neg_inf))

    def slc_group(cg, carry):
        for u in range(SLC_GROUP):
            slc_chunk(SLC_GROUP * cg + u)
        return carry

    lax.fori_loop(0, i // SLC_GROUP, slc_group, 0)
    for rem in range(SLC_GROUP):
        def slc_tail(rem=rem):
            for u in range(rem, 0, -1):
                slc_chunk(i - u)
            slc_diagonal()
        pl.when(i % SLC_GROUP == rem)(slc_tail)
    o_slc4 = finish()

    o_cmp4 = ocmp_ref[...]
    gs = jax.nn.sigmoid(gt_ref[...])
    outs = []
    for h in range(H):
        rs = slice(h * TQ, (h + 1) * TQ)
        outs.append(gs[:, 3 * h:3 * h + 1] * o_cmp4[rs] + gs[:, 3 * h + 1:3 * h + 2] * o_slc4[rs]
                    + gs[:, 3 * h + 2:3 * h + 3] * o_win4[rs])
    o_ref[...] = _head_norm_gate(jnp.concatenate(outs, axis=-1), z_ref[...], g_ref[...])


def _nsa(proj, k_cmp, v_cmp, bg, l, B, S, TQ=256):
    assert WIN % TQ == 0 and TQ % SLC_LEN == 0 and TQ % LANES == 0
    nt = S // TQ
    T = B * S
    NB = k_cmp.shape[1]

    def seq(name):
        off, wp = _DST[name]
        return pl.BlockSpec((S, wp), lambda b, i: (b, off // wp))

    cspec = pl.BlockSpec((None, NB, HEAD_DIM), lambda b, i: (b, 0, 0))
    return pl.pallas_call(
        _nsa_kernel,
        out_shape=jax.ShapeDtypeStruct((T, W_BR), BF16),
        grid=(B, nt),
        in_specs=[_seg_spec('nsa_q', TQ, nt), _seg_spec('nsa_g', TQ, nt), _seg_spec('nsa_z', TQ, nt),
                  cspec, cspec, seq('nsa_ks'), seq('nsa_vs'), seq('nsa_kw'), seq('nsa_vw'),
                  _layer_spec(bg, l, 2)],
        out_specs=pl.BlockSpec((TQ, W_BR), lambda b, i: (b * nt + i, 0)),
        scratch_shapes=[pltpu.VMEM((N_HEADS * TQ, LANES), F32),
                        pltpu.VMEM((N_HEADS * TQ, 2 * HEAD_DIM), F32),
                        pltpu.VMEM((N_HEADS * TQ, HEAD_DIM), F32), pltpu.VMEM((TQ, LANES), BF16)],
        compiler_params=_params("parallel", "arbitrary"),
        name="nsa_attention",
    )(proj, proj, proj, k_cmp, v_cmp, proj, proj, proj, proj, bg)


def kernel(x, pre_norm_g, w_in, gla_w_fg2, gla_b_fg2, lru_conv_w, lru_conv_b, lru_w_a, lru_b_a, lru_w_x,
           lru_b_x, lru_lambda, nsa_cmp_pos_k, nsa_cmp_w1_k, nsa_cmp_w2_k, nsa_cmp_pos_v, nsa_cmp_w1_v,
           nsa_cmp_w2_v, conf_dw_w, conf_dw_b, conf_ln_g, conf_ln_b, conf_pw_w, conf_pw_b, branch_norm_g,
           w_out, post_norm_g):
    B, S, D = x.shape
    L = w_in.shape[0]
    T = B * S
    x2 = x.reshape(T, D)

    def rows(a):
        return a.reshape(L, 1, -1)

    w_in_t = _relayout_w_in(w_in)
    lowrank = gla_w_fg2.shape[1]
    w1k = nsa_cmp_w1_k.reshape(L, CMP_LEN, HEAD_DIM, -1).astype(BF16)
    w1v = nsa_cmp_w1_v.reshape(L, CMP_LEN, HEAD_DIM, -1).astype(BF16)
    w2k = nsa_cmp_w2_k.astype(BF16)
    w2v = nsa_cmp_w2_v.astype(BF16)
    bg = branch_norm_g.reshape(L, N_MIXERS, 1, W_BR)
    p = dict(
        bg=bg, pre_g=rows(pre_norm_g), post_g=rows(post_norm_g), w_out=w_out.astype(BF16),
        wfg=jnp.pad(gla_w_fg2, ((0, 0), (0, LANES - lowrank), (0, 0))).astype(BF16), bfg=rows(gla_b_fg2),
        lru_cw=lru_conv_w, lru_cb=rows(lru_conv_b), wa=lru_w_a.astype(BF16), lru_ba=rows(lru_b_a),
        wx=lru_w_x.astype(BF16), lru_bx=rows(lru_b_x), lru_lam=rows(lru_lambda),
        conf_dw=conf_dw_w, conf_db=rows(conf_dw_b), conf_lg=rows(conf_ln_g), conf_lb=rows(conf_ln_b),
        pw=conf_pw_w.astype(BF16), conf_pb=rows(conf_pw_b))

    hn = _prenorm(x2, p['pre_g'], 0)
    for l in range(L):
        proj = _in_proj(hn, w_in_t, l)
        k_cmp, v_cmp = _nsa_compress(proj, nsa_cmp_pos_k, w1k, w2k, nsa_cmp_pos_v, w1v, w2v, l, B, S)
        m_c = _nsa(proj, k_cmp, v_cmp, bg, l, B, S)
        if l + 1 < L:
            x2, hn = _mix_out(proj, m_c, x2, p, l, B, S, emit_hn=True)
        else:
            x2 = _mix_out(proj, m_c, x2, p, l, B, S, emit_hn=False)
    return x2.reshape(B, S, D)
```

```python
import functools

import jax
import jax.numpy as jnp
from jax import lax
from jax.experimental import pallas as pl
from jax.experimental.pallas import tpu as pltpu

F32 = jnp.float32
BF16 = jnp.bfloat16

D_MODEL = 2048
N_HEADS = 4
HEAD_DIM = 128
W_BR = N_HEADS * HEAD_DIM
N_MIXERS = 4
GLA_HEAD_K = 64
GLA_TAU = 16.0
GLA_CHUNK = 64
LRU_C = 8.0
LRU_CONV = 4
CMP_LEN = 32
CMP_STRIDE = 16
SLC_LEN = 64
SLC_TOPN = 16
SLC_GROUP = 8
WIN = 512
FORCED_SCORE = 1e3
CONF_KERNEL = 31
CONF_HIST = 32
EPS = 1e-6
LOG2_E = 1.4426950408889634
SUBLANES = 8
LANES = 128
VMEM_LIMIT = 48 * 1024 * 1024

_SEGS = (('gla_q', 256), ('gla_k', 256), ('gla_v', 512), ('gla_fg', 16), ('gla_z', 512),
         ('lru_x', 512), ('lru_z', 512),
         ('nsa_q', 512), ('nsa_kc', 128), ('nsa_vc', 128), ('nsa_ks', 128), ('nsa_vs', 128),
         ('nsa_kw', 128), ('nsa_vw', 128), ('nsa_g', 12), ('nsa_z', 512),
         ('conv_v', 512), ('conv_glu', 512), ('conv_z', 512))
_ORDER = ('gla_v', 'gla_z', 'lru_x', 'lru_z', 'nsa_q', 'nsa_z', 'conv_v', 'conv_glu', 'conv_z',
          'gla_q', 'gla_k', 'gla_fg', 'nsa_kc', 'nsa_vc', 'nsa_ks', 'nsa_vs', 'nsa_kw', 'nsa_vw',
          'nsa_g')


def _layout():
    src, off = {}, 0
    for name, w in _SEGS:
        src[name] = (off, w)
        off += w
    dst, off = {}, 0
    for name in _ORDER:
        w = src[name][1]
        wp = -(-w // LANES) * LANES
        assert off % wp == 0
        dst[name] = (off, wp)
        off += wp
    return src, dst, off


_SRC, _DST, D_PROJ_PAD = _layout()


def _relayout_kernel(off_ref, valid_ref, w_ref, o_ref):
    del off_ref
    rows = lax.broadcasted_iota(jnp.int32, o_ref.shape[1:], 0)
    keep = rows < valid_ref[pl.program_id(0)]
    for l in range(o_ref.shape[0]):
        o_ref[l] = jnp.where(keep, w_ref[:, l, :], 0.0).astype(BF16)


def _relayout_w_in(w_in):
    L, D, NP = w_in.shape
    wt = jnp.transpose(w_in, (2, 0, 1))
    offs, valid = [], []
    for name in _ORDER:
        o, w = _SRC[name]
        for b in range(_DST[name][1] // LANES):
            offs.append(o + b * LANES)
            valid.append(min(LANES, w - b * LANES))
    assert max(offs) + LANES <= NP and min(valid) > 0
    grid_spec = pltpu.PrefetchScalarGridSpec(
        num_scalar_prefetch=2,
        grid=(len(offs),),
        in_specs=[pl.BlockSpec((pl.Element(LANES), pl.Element(L), pl.Element(D)),
                               lambda j, off, valid: (off[j], 0, 0))],
        out_specs=pl.BlockSpec((L, LANES, D), lambda j, off, valid: (0, j, 0)))
    return pl.pallas_call(
        _relayout_kernel,
        out_shape=jax.ShapeDtypeStruct((L, D_PROJ_PAD, D), BF16),
        grid_spec=grid_spec,
        compiler_params=_params("parallel"),
        name="relayout_w_in",
    )(jnp.array(offs, jnp.int32), jnp.array(valid, jnp.int32), wt)


def _dot(a, b):
    return jnp.dot(a, b, preferred_element_type=F32)


def _dot_t(a, b):
    return lax.dot_general(a, b, (((1,), (1,)), ((), ())), preferred_element_type=F32)


def _softplus(x):
    return jnp.maximum(x, 0.0) + jnp.log1p(jnp.exp(-jnp.abs(x)))


def _expm1(x):
    return jnp.tanh(0.5 * x) * (jnp.exp(x) + 1.0)


def _silu(x):
    return x * jax.nn.sigmoid(x)


def _rmsnorm(x, g):
    return x * lax.rsqrt(jnp.mean(x * x, axis=-1, keepdims=True) + EPS) * g


def _masked_softmax(s, mask):
    s = jnp.where(mask, s, -jnp.inf)
    m = jnp.max(s, axis=-1, keepdims=True)
    m = jnp.where(jnp.isfinite(m), m, 0.0)
    p = jnp.exp(s - m)
    return p / jnp.maximum(jnp.sum(p, axis=-1, keepdims=True), 1e-30)


def _head_norm_gate(o, z, g):
    outs = []
    for h in range(N_HEADS):
        oh = o[:, h * HEAD_DIM:(h + 1) * HEAD_DIM]
        outs.append(oh * lax.rsqrt(jnp.mean(oh * oh, axis=-1, keepdims=True) + EPS))
    on = jnp.concatenate(outs, axis=-1) * g
    return (on * _silu(z)).astype(BF16)


def _layer_spec(arr, *lead):
    rest = arr.shape[len(lead):]
    idx = tuple(lead) + (0,) * len(rest)
    return pl.BlockSpec((None,) * len(lead) + rest, lambda *_: idx)


def _seg_spec(name, rows, nt):
    off, wp = _DST[name]
    cb = off // wp
    return pl.BlockSpec((rows, wp), lambda b, i: (b * nt + i, cb))


def _params(*sem):
    return pltpu.CompilerParams(dimension_semantics=sem, vmem_limit_bytes=VMEM_LIMIT)


def _prenorm_kernel(x_ref, g_ref, o_ref):
    o_ref[...] = _rmsnorm(x_ref[...], g_ref[...]).astype(BF16)


def _prenorm(x2, g, l, tm=512):
    T, D = x2.shape
    return pl.pallas_call(
        _prenorm_kernel,
        out_shape=jax.ShapeDtypeStruct((T, D), BF16),
        grid=(T // tm,),
        in_specs=[pl.BlockSpec((tm, D), lambda i: (i, 0)), _layer_spec(g, l)],
        out_specs=pl.BlockSpec((tm, D), lambda i: (i, 0)),
        compiler_params=_params("parallel"),
        name="prenorm",
    )(x2, g)


def _in_proj_kernel(h_ref, wt_ref, o_ref):
    o_ref[...] = _dot_t(h_ref[...], wt_ref[...])


def _in_proj(hn, wt, l, tm=1024, tn=2048):
    T, D = hn.shape
    NP = wt.shape[1]
    return pl.pallas_call(
        _in_proj_kernel,
        out_shape=jax.ShapeDtypeStruct((T, NP), F32),
        grid=(T // tm, NP // tn),
        in_specs=[pl.BlockSpec((tm, D), lambda i, j: (i, 0)),
                  pl.BlockSpec((None, tn, D), lambda i, j: (l, j, 0))],
        out_specs=pl.BlockSpec((tm, tn), lambda i, j: (i, j)),
        compiler_params=_params("parallel", "arbitrary"),
        name="in_proj",
    )(hn, wt)


N_GLA_IN, N_LRU_IN, N_CONF_IN = 8, 10, 10


def _mix_out_kernel(*refs, emit_hn):
    refs = list(refs)
    gla_in = [refs.pop(0) for _ in range(N_GLA_IN)]
    lru_in = [refs.pop(0) for _ in range(N_LRU_IN)]
    conf_in = [refs.pop(0) for _ in range(N_CONF_IN)]
    mc_ref, w_ref, x_ref, pg_ref = [refs.pop(0) for _ in range(4)]
    gn_ref = refs.pop(0) if emit_hn else None
    o_ref = refs.pop(0)
    hn_ref = refs.pop(0) if emit_hn else None
    gla_state, gla_acc, lru_hist, lru_h, lru_acc, conf_hist = refs

    @pl.when(pl.program_id(1) == 0)
    def _():
        for ref in (gla_state, lru_hist, lru_h, conf_hist):
            ref[...] = jnp.zeros(ref.shape, F32)

    gla_stages, gla_finish = _gla_stages(*gla_in, gla_state, gla_acc)

    def next_gla_stage():
        if gla_stages:
            gla_stages.pop(0)()

    m_b = _lru_mix(*lru_in, lru_hist, lru_h, lru_acc)
    m_d = _conf_mix(*conf_in, conf_hist, next_gla_stage)
    while gla_stages:
        next_gla_stage()
    mixed = jnp.concatenate([gla_finish(), m_b, mc_ref[...], m_d], axis=-1)
    x_new = x_ref[...] + _rmsnorm(_dot(mixed, w_ref[...]), pg_ref[...])
    o_ref[...] = x_new
    if emit_hn:
        hn_ref[...] = _rmsnorm(x_new, gn_ref[...]).astype(BF16)


def _mix_out(proj, m_c, x2, p, l, B, S, emit_hn, R=256):
    nt = S // R
    T, D = x2.shape
    bg = p['bg']

    def tile(width):
        return pl.BlockSpec((R, width), lambda b, i: (b * nt + i, 0))

    def seg(*names):
        return [_seg_spec(n, R, nt) for n in names]

    def layer(*names):
        return [_layer_spec(p[n], l) for n in names]

    gla = seg('gla_q', 'gla_k', 'gla_v', 'gla_fg', 'gla_z') + layer('wfg', 'bfg') + [_layer_spec(bg, l, 0)]
    lru = (seg('lru_x', 'lru_z') + layer('lru_cw', 'lru_cb', 'wa', 'lru_ba', 'wx', 'lru_bx', 'lru_lam')
           + [_layer_spec(bg, l, 1)])
    conf = (seg('conv_v', 'conv_glu', 'conv_z') + layer('conf_dw', 'conf_db', 'conf_lg', 'conf_lb', 'pw', 'conf_pb')
            + [_layer_spec(bg, l, 3)])
    assert (len(gla), len(lru), len(conf)) == (N_GLA_IN, N_LRU_IN, N_CONF_IN)
    in_specs = gla + lru + conf + [tile(W_BR), _layer_spec(p['w_out'], l), tile(D), _layer_spec(p['post_g'], l)]
    args = ([proj] * 5 + [p['wfg'], p['bfg'], bg]
            + [proj] * 2 + [p[n] for n in ('lru_cw', 'lru_cb', 'wa', 'lru_ba', 'wx', 'lru_bx', 'lru_lam')] + [bg]
            + [proj] * 3 + [p[n] for n in ('conf_dw', 'conf_db', 'conf_lg', 'conf_lb', 'pw', 'conf_pb')] + [bg]
            + [m_c, p['w_out'], x2, p['post_g']])
    out_shape = jax.ShapeDtypeStruct((T, D), F32)
    out_specs = tile(D)
    if emit_hn:
        in_specs.append(_layer_spec(p['pre_g'], l + 1))
        args.append(p['pre_g'])
        out_shape = (out_shape, jax.ShapeDtypeStruct((T, D), BF16))
        out_specs = (tile(D), tile(D))
    return pl.pallas_call(
        functools.partial(_mix_out_kernel, emit_hn=emit_hn),
        out_shape=out_shape,
        grid=(B, nt),
        in_specs=in_specs,
        out_specs=out_specs,
        scratch_shapes=[pltpu.VMEM((N_HEADS, HEAD_DIM, LANES), F32), pltpu.VMEM((R, W_BR), F32),
                        pltpu.VMEM((SUBLANES, W_BR), F32), pltpu.VMEM((1, W_BR), F32),
                        pltpu.VMEM((R, W_BR), F32), pltpu.VMEM((CONF_HIST, W_BR), F32)],
        compiler_params=_params("parallel", "arbitrary"),
        name="mix_out",
    )(*args)


def _gla_stages(q_ref, k_ref, v_ref, fg_ref, z_ref, wfg_ref, bfg_ref, g_ref, state_ref, acc_ref):
    R = q_ref.shape[0]
    C = GLA_CHUNK

    pre = _dot(fg_ref[...].astype(BF16), wfg_ref[...]) + bfg_ref[...]
    log_f = -_softplus(-pre) * (1.0 / GLA_TAU)
    row = lax.broadcasted_iota(jnp.int32, log_f.shape, 0) % C
    bcum = log_f
    d = 1
    while d < C:
        bcum = bcum + jnp.where(row >= d, pltpu.roll(bcum, d, 0), 0.0)
        d *= 2

    q = q_ref[...] * (GLA_HEAD_K ** -0.5)
    k = k_ref[...]
    v = v_ref[...]
    lane = lax.broadcasted_iota(jnp.int32, (C, LANES), 1)
    causal = (lax.broadcasted_iota(jnp.int32, (C, C), 0) >= lax.broadcasted_iota(jnp.int32, (C, C), 1))
    head_lanes = (lane < GLA_HEAD_K, lane >= GLA_HEAD_K)

    def stage(c, p):
        rs = slice(c * C, (c + 1) * C)
        ls = slice(p * LANES, (p + 1) * LANES)
        b = bcum[rs, ls]
        b_last = b[C - 1:C, :]
        kk = k[rs, ls]
        q_dec = q[rs, ls] * jnp.exp(b)
        k_dec = kk * jnp.exp(-b)
        k_last = kk * jnp.exp(b_last - b)
        decay = jnp.exp(b_last)
        for hh in range(2):
            h = 2 * p + hh
            m = head_lanes[hh]
            qd = jnp.where(m, q_dec, 0.0).astype(BF16)
            kd = jnp.where(m, k_dec, 0.0).astype(BF16)
            kl = jnp.where(m, k_last, 0.0).astype(BF16)
            vh = v[rs, h * HEAD_DIM:(h + 1) * HEAD_DIM]
            attn = jnp.where(causal, _dot_t(qd, kd), 0.0)
            st = state_ref[h]
            o = _dot(attn.astype(BF16), vh.astype(BF16)) + _dot_t(qd, st.astype(BF16))
            acc_ref[rs, h * HEAD_DIM:(h + 1) * HEAD_DIM] = o
            state_ref[h] = decay * st + _dot(vh.T.astype(BF16), kl)

    def finish():
        return _head_norm_gate(acc_ref[...], z_ref[...], g_ref[...])

    stages = [functools.partial(stage, c, p) for c in range(R // C) for p in range(2)]
    return stages, finish


def _lru_mix(x_ref, z_ref, cw_ref, cb_ref, wa_ref, ba_ref, wx_ref, bx_ref, lam_ref, g_ref,
             hist_ref, h_ref, acc_ref):
    R = x_ref.shape[0]
    HIST = hist_ref.shape[0]

    x = x_ref[...]
    xe = jnp.concatenate([hist_ref[...], x], axis=0)
    xc = cb_ref[...] + cw_ref[LRU_CONV - 1:LRU_CONV, :] * x
    for kk in range(LRU_CONV - 1):
        sh = LRU_CONV - 1 - kk
        xc = xc + cw_ref[kk:kk + 1, :] * pltpu.roll(xe, sh, 0)[HIST:HIST + R]
    hist_ref[...] = x[R - HIST:R]

    xb = xc.astype(BF16)
    ra, rx = [], []
    for h in range(N_HEADS):
        xh = xb[:, h * HEAD_DIM:(h + 1) * HEAD_DIM]
        ra.append(_dot(xh, wa_ref[h]))
        rx.append(_dot(xh, wx_ref[h]))
    r = jax.nn.sigmoid(jnp.concatenate(ra, axis=-1) + ba_ref[...])
    ig = jax.nn.sigmoid(jnp.concatenate(rx, axis=-1) + bx_ref[...])
    log_a = -LRU_C * r * _softplus(-lam_ref[...])
    a = jnp.exp(log_a)
    u = jnp.sqrt(-_expm1(2.0 * log_a)) * (ig * xc)

    G = R // SUBLANES
    a3 = a.reshape(G, SUBLANES, W_BR)
    u3 = u.reshape(G, SUBLANES, W_BR)
    row = lax.broadcasted_iota(jnp.int32, a3.shape, 1)
    d = 1
    while d < SUBLANES:
        keep = row >= d
        u3 = jnp.where(keep, a3 * pltpu.roll(u3, d, 1) + u3, u3)
        a3 = jnp.where(keep, a3 * pltpu.roll(a3, d, 1), a3)
        d *= 2
    carry = h_ref[...]
    for gi in range(G):
        hg = a3[gi] * carry + u3[gi]
        acc_ref[gi * SUBLANES:(gi + 1) * SUBLANES, :] = hg
        carry = hg[SUBLANES - 1:SUBLANES, :]
    h_ref[...] = carry
    return _head_norm_gate(acc_ref[...], z_ref[...], g_ref[...])


def _conf_mix(v_ref, glu_ref, z_ref, dw_ref, db_ref, lg_ref, lb_ref, pw_ref, pb_ref, g_ref, hist_ref, beside):
    R = v_ref.shape[0]
    HIST = hist_ref.shape[0]

    y = v_ref[...] * jax.nn.sigmoid(glu_ref[...])
    ye = jnp.concatenate([hist_ref[...], y], axis=0)
    acc = db_ref[...]
    for r in range(SUBLANES):
        yr = ye if r == 0 else pltpu.roll(ye, r, 0)
        for q in range((CONF_KERNEL - 1 - r) // SUBLANES + 1):
            kk = CONF_KERNEL - 1 - (SUBLANES * q + r)
            start = HIST - SUBLANES * q
            acc = acc + dw_ref[kk:kk + 1, :] * yr[start:start + R]
        beside()
    hist_ref[...] = y[R - HIST:R]

    mu = jnp.mean(acc, axis=-1, keepdims=True)
    xc = acc - mu
    var = jnp.mean(xc * xc, axis=-1, keepdims=True)
    yn = xc * lax.rsqrt(var + EPS) * lg_ref[...] + lb_ref[...]
    o = _dot(_silu(yn).astype(BF16), pw_ref[...]) + pb_ref[...]
    return _head_norm_gate(o, z_ref[...], g_ref[...])


def _cmp_kernel(kc_ref, vc_ref, posk_ref, w1k_ref, w2k_ref, posv_ref, w1v_ref, w2v_ref, ko_ref, vo_ref):
    S = kc_ref.shape[0]
    NB = S // CMP_STRIDE
    HALF = CMP_LEN // 2

    def compress(z_ref, pos_ref, w1_ref, w2_ref):
        lo = jnp.zeros((NB, w1_ref.shape[2]), F32)
        hi = jnp.zeros((NB, w1_ref.shape[2]), F32)
        for l in range(HALF):
            grp = z_ref[pl.ds(l, NB, stride=CMP_STRIDE), :]
            lo = lo + _dot((grp + pos_ref[l:l + 1, :]).astype(BF16), w1_ref[l])
            hi = hi + _dot((grp + pos_ref[HALF + l:HALF + l + 1, :]).astype(BF16), w1_ref[HALF + l])
        hid = lo + pltpu.roll(hi, NB - 1, 0)
        blk = lax.broadcasted_iota(jnp.int32, hid.shape, 0)
        hid = jnp.where(blk < NB - 1, hid, 0.0)
        return _dot(_silu(hid).astype(BF16), w2_ref[...])

    ko_ref[...] = compress(kc_ref, posk_ref, w1k_ref, w2k_ref)
    vo_ref[...] = compress(vc_ref, posv_ref, w1v_ref, w2v_ref)


def _nsa_compress(proj, posk, w1k, w2k, posv, w1v, w2v, l, B, S):
    NB = S // CMP_STRIDE

    def seg(name):
        off, wp = _DST[name]
        return pl.BlockSpec((S, wp), lambda b: (b, off // wp))

    out = jax.ShapeDtypeStruct((B, NB, HEAD_DIM), F32)
    ospec = pl.BlockSpec((None, NB, HEAD_DIM), lambda b: (b, 0, 0))
    return pl.pallas_call(
        _cmp_kernel,
        out_shape=(out, out),
        grid=(B,),
        in_specs=[seg('nsa_kc'), seg('nsa_vc'), _layer_spec(posk, l), _layer_spec(w1k, l),
                  _layer_spec(w2k, l), _layer_spec(posv, l), _layer_spec(w1v, l), _layer_spec(w2v, l)],
        out_specs=(ospec, ospec),
        compiler_params=_params("parallel"),
        name="nsa_compress",
    )(proj, proj, posk, w1k, w2k, posv, w1v, w2v)


def _nsa_kernel(q_ref, gt_ref, z_ref, kc_ref, vc_ref, ks_ref, vs_ref, kw_ref, vw_ref, g_ref, o_ref,
                m_ref, acc_ref, ocmp_ref, sel_ref):
    TQ = q_ref.shape[0]
    S = ks_ref.shape[0]
    NB = kc_ref.shape[0]
    H, Dh = N_HEADS, HEAD_DIM
    scale = Dh ** -0.5
    i = pl.program_id(1)
    t0 = i * TQ

    q = q_ref[...]
    q4 = jnp.concatenate([q[:, h * Dh:(h + 1) * Dh] for h in range(H)], axis=0).astype(BF16)

    n_sel = S // SLC_LEN
    RANK_STEP = 8

    def selection_stages():
        st = {}

        def compressed():
            t_c = t0 + lax.broadcasted_iota(jnp.int32, (TQ, NB), 0)
            n_c = lax.broadcasted_iota(jnp.int32, (TQ, NB), 1)
            cmask = t_c >= n_c * CMP_STRIDE + (CMP_LEN - 1)
            cmask4 = jnp.concatenate([cmask] * H, axis=0)
            p4 = _masked_softmax(_dot_t(q4, kc_ref[...].astype(BF16)) * scale, cmask4)
            st['p4b'] = p4.astype(BF16)
            ocmp_ref[...] = _dot(st['p4b'], vc_ref[...].astype(BF16))

        def importance():
            jj = lax.broadcasted_iota(jnp.int32, (n_sel, NB), 0)
            nn = lax.broadcasted_iota(jnp.int32, (n_sel, NB), 1)
            ov = ((nn * CMP_STRIDE < (jj + 1) * SLC_LEN) & (nn * CMP_STRIDE + CMP_LEN > jj * SLC_LEN)
                  & (nn < NB - 1))
            po = _dot_t(jnp.where(ov, 1.0, 0.0).astype(BF16), st['p4b'])
            imp = po[:, 0:TQ]
            for h in range(1, H):
                imp = imp + po[:, h * TQ:(h + 1) * TQ]
            j = lax.broadcasted_iota(jnp.int32, (n_sel, TQ), 0)
            cur = (t0 + lax.broadcasted_iota(jnp.int32, (n_sel, TQ), 1)) // SLC_LEN
            forced = (j == 0) | (j == cur) | (j == cur - 1)
            st['j'] = j
            st['val'] = jnp.where(j > cur, -jnp.inf, jnp.where(forced, FORCED_SCORE, imp))
            st['rank'] = jnp.zeros((n_sel, TQ), F32)

        def rank_part(lo):
            val, j = st['val'], st['j']
            for ii in range(lo, lo + RANK_STEP):
                vi = val[ii:ii + 1, :]
                beats = jnp.where(vi > val, 1.0, jnp.where(vi == val, jnp.where(j > ii, 1.0, 0.0), 0.0))
                st['rank'] = st['rank'] + beats

        def select():
            sel_t = jnp.where(st['rank'] < SLC_TOPN, 1.0, 0.0)
            sel_ref[...] = jnp.concatenate([sel_t, jnp.zeros((LANES - n_sel, TQ), F32)], axis=0).T.astype(BF16)

        return ([compressed, importance] + [functools.partial(rank_part, lo) for lo in range(0, n_sel, RANK_STEP)]
                + [select])

    t_k = t0 + lax.broadcasted_iota(jnp.int32, (TQ, TQ), 0)
    k_i = lax.broadcasted_iota(jnp.int32, (TQ, TQ), 1)
    ones_half = jnp.ones((TQ, Dh), BF16)
    neg_inf = jnp.full((TQ, TQ), -jnp.inf, F32)
    scale_log2e = scale * LOG2_E

    def reset():
        m_ref[...] = jnp.full(m_ref.shape, -jnp.inf, F32)
        acc_ref[...] = jnp.zeros(acc_ref.shape, F32)

    def attend_chunk(k_ref, v_ref, c, bias, beside=lambda: None):
        k0 = pl.multiple_of(c * TQ, TQ)
        kb = k_ref[pl.ds(k0, TQ), :].astype(BF16)
        vb = jnp.concatenate([v_ref[pl.ds(k0, TQ), :].astype(BF16), ones_half], axis=-1)
        s4 = _dot_t(q4, kb)
        for h in range(H):
            rs = slice(h * TQ, (h + 1) * TQ)
            s = s4[rs] if bias is None else s4[rs] + bias
            m_old = m_ref[rs]
            m_new = jnp.maximum(m_old, jnp.max(s, axis=-1, keepdims=True))
            m_safe = jnp.where(m_new == -jnp.inf, 0.0, m_new)
            p = jnp.exp2((s - jnp.concatenate([m_safe] * (TQ // LANES), axis=-1)) * scale_log2e)
            alpha = jnp.exp2((m_old - m_safe) * scale_log2e)
            acc_ref[rs] = jnp.concatenate([alpha, alpha], axis=-1) * acc_ref[rs] + _dot(p.astype(BF16), vb)
            m_ref[rs] = m_new
            beside()

    def attend_window_ends(k_ref, v_ref, beside):
        old = k_i > (t_k - t0)
        k_old, k_new = pl.multiple_of((i - n_back) * TQ, TQ), pl.multiple_of(i * TQ, TQ)
        s4_old = _dot_t(q4, k_ref[pl.ds(k_old, TQ), :].astype(BF16))
        s4_new = _dot_t(q4, k_ref[pl.ds(k_new, TQ), :].astype(BF16))
        vb_old = jnp.concatenate([v_ref[pl.ds(k_old, TQ), :].astype(BF16), ones_half], axis=-1)
        vb_new = jnp.concatenate([v_ref[pl.ds(k_new, TQ), :].astype(BF16), ones_half], axis=-1)
        zero = jnp.zeros((TQ, TQ), BF16)
        for h in range(H):
            rs = slice(h * TQ, (h + 1) * TQ)
            s = jnp.where(old, s4_old[rs], s4_new[rs])
            m_old = m_ref[rs]
            m_new = jnp.maximum(m_old, jnp.max(s, axis=-1, keepdims=True))
            p = jnp.exp2((s - jnp.concatenate([m_new] * (TQ // LANES), axis=-1)) * scale_log2e).astype(BF16)
            alpha = jnp.exp2((m_old - m_new) * scale_log2e)
            acc_ref[rs] = (jnp.concatenate([alpha, alpha], axis=-1) * acc_ref[rs]
                           + _dot(jnp.where(old, p, zero), vb_old) + _dot(jnp.where(old, zero, p), vb_new))
            m_ref[rs] = m_new
            beside()

    def finish():
        acc = acc_ref[...]
        return acc[:, :Dh] / jnp.maximum(acc[:, Dh:], 1e-30)

    reset()
    n_back = WIN // TQ

    def win_chunk(back, beside):
        c = i - back
        if 0 < back < n_back:
            bias = None
        else:
            dist = t_k - (c * TQ + k_i)
            bias = jnp.where((dist >= 0) & (dist < WIN), 0.0, neg_inf)
        attend_chunk(kw_ref, vw_ref, c, bias, beside)

    for first in range(n_back, -1, -1):
        def win_chunks(first=first):
            stages = selection_stages()

            def next_stage():
                if stages:
                    stages.pop(0)()

            if first == n_back:
                for back in range(n_back - 1, 0, -1):
                    win_chunk(back, next_stage)
                attend_window_ends(kw_ref, vw_ref, next_stage)
            else:
                for back in range(first, -1, -1):
                    win_chunk(back, next_stage)
            while stages:
                next_stage()
        pl.when(jnp.minimum(i, n_back) == first)(win_chunks)
    o_win4 = finish()
    sel = sel_ref[...]

    eb = lax.broadcasted_iota(jnp.int32, (LANES, TQ), 0)
    ek = lax.broadcasted_iota(jnp.int32, (LANES, TQ), 1) // SLC_LEN

    def sel_keys(c):
        expand = jnp.where(eb == ek + c * (TQ // SLC_LEN), 1.0, 0.0).astype(BF16)
        return _dot(sel, expand) > 0.5

    reset()

    def slc_chunk(c):
        attend_chunk(ks_ref, vs_ref, c, jnp.where(sel_keys(c), 0.0, neg_inf))

    def slc_diagonal():
        attend_chunk(ks_ref, vs_ref, i, jnp.where(sel_keys(i) & (t0 + k_i <= t_k), 0.0, neg_inf))

    def slc_group(cg, carry):
        for u in range(SLC_GROUP):
            slc_chunk(SLC_GROUP * cg + u)
        return carry

    lax.fori_loop(0, i // SLC_GROUP, slc_group, 0)
    for rem in range(SLC_GROUP):
        def slc_tail(rem=rem):
            for u in range(rem, 0, -1):
                slc_chunk(i - u)
            slc_diagonal()
        pl.when(i % SLC_GROUP == rem)(slc_tail)
    o_slc4 = finish()

    o_cmp4 = ocmp_ref[...]
    gs = jax.nn.sigmoid(gt_ref[...])
    outs = []
    for h in range(H):
        rs = slice(h * TQ, (h + 1) * TQ)
        outs.append(gs[:, 3 * h:3 * h + 1] * o_cmp4[rs] + gs[:, 3 * h + 1:3 * h + 2] * o_slc4[rs]
                    + gs[:, 3 * h + 2:3 * h + 3] * o_win4[rs])
    o_ref[...] = _head_norm_gate(jnp.concatenate(outs, axis=-1), z_ref[...], g_ref[...])


def _nsa(proj, k_cmp, v_cmp, bg, l, B, S, TQ=256):
    assert WIN % TQ == 0 and TQ % SLC_LEN == 0 and TQ % LANES == 0
    nt = S // TQ
    T = B * S
    NB = k_cmp.shape[1]

    def seq(name):
        off, wp = _DST[name]
        return pl.BlockSpec((S, wp), lambda b, i: (b, off // wp))

    cspec = pl.BlockSpec((None, NB, HEAD_DIM), lambda b, i: (b, 0, 0))
    return pl.pallas_call(
        _nsa_kernel,
        out_shape=jax.ShapeDtypeStruct((T, W_BR), BF16),
        grid=(B, nt),
        in_specs=[_seg_spec('nsa_q', TQ, nt), _seg_spec('nsa_g', TQ, nt), _seg_spec('nsa_z', TQ, nt),
                  cspec, cspec, seq('nsa_ks'), seq('nsa_vs'), seq('nsa_kw'), seq('nsa_vw'),
                  _layer_spec(bg, l, 2)],
        out_specs=pl.BlockSpec((TQ, W_BR), lambda b, i: (b * nt + i, 0)),
        scratch_shapes=[pltpu.VMEM((N_HEADS * TQ, LANES), F32),
                        pltpu.VMEM((N_HEADS * TQ, 2 * HEAD_DIM), F32),
                        pltpu.VMEM((N_HEADS * TQ, HEAD_DIM), F32), pltpu.VMEM((TQ, LANES), BF16)],
        compiler_params=_params("parallel", "arbitrary"),
        name="nsa_attention",
    )(proj, proj, proj, k_cmp, v_cmp, proj, proj, proj, proj, bg)


def kernel(x, pre_norm_g, w_in, gla_w_fg2, gla_b_fg2, lru_conv_w, lru_conv_b, lru_w_a, lru_b_a, lru_w_x,
           lru_b_x, lru_lambda, nsa_cmp_pos_k, nsa_cmp_w1_k, nsa_cmp_w2_k, nsa_cmp_pos_v, nsa_cmp_w1_v,
           nsa_cmp_w2_v, conf_dw_w, conf_dw_b, conf_ln_g, conf_ln_b, conf_pw_w, conf_pw_b, branch_norm_g,
           w_out, post_norm_g):
    B, S, D = x.shape
    L = w_in.shape[0]
    T = B * S
    x2 = x.reshape(T, D)

    def rows(a):
        return a.reshape(L, 1, -1)

    w_in_t = _relayout_w_in(w_in)
    lowrank = gla_w_fg2.shape[1]
    w1k = nsa_cmp_w1_k.reshape(L, CMP_LEN, HEAD_DIM, -1).astype(BF16)
    w1v = nsa_cmp_w1_v.reshape(L, CMP_LEN, HEAD_DIM, -1).astype(BF16)
    w2k = nsa_cmp_w2_k.astype(BF16)
    w2v = nsa_cmp_w2_v.astype(BF16)
    bg = branch_norm_g.reshape(L, N_MIXERS, 1, W_BR)
    p = dict(
        bg=bg, pre_g=rows(pre_norm_g), post_g=rows(post_norm_g), w_out=w_out.astype(BF16),
        wfg=jnp.pad(gla_w_fg2, ((0, 0), (0, LANES - lowrank), (0, 0))).astype(BF16), bfg=rows(gla_b_fg2),
        lru_cw=lru_conv_w, lru_cb=rows(lru_conv_b), wa=lru_w_a.astype(BF16), lru_ba=rows(lru_b_a),
        wx=lru_w_x.astype(BF16), lru_bx=rows(lru_b_x), lru_lam=rows(lru_lambda),
        conf_dw=conf_dw_w, conf_db=rows(conf_dw_b), conf_lg=rows(conf_ln_g), conf_lb=rows(conf_ln_b),
        pw=conf_pw_w.astype(BF16), conf_pb=rows(conf_pw_b))

    hn = _prenorm(x2, p['pre_g'], 0)
    for l in range(L):
        proj = _in_proj(hn, w_in_t, l)
        k_cmp, v_cmp = _nsa_compress(proj, nsa_cmp_pos_k, w1k, w2k, nsa_cmp_pos_v, w1v, w2v, l, B, S)
        m_c = _nsa(proj, k_cmp, v_cmp, bg, l, B, S)
        if l + 1 < L:
            x2, hn = _mix_out(proj, m_c, x2, p, l, B, S, emit_hn=True)
        else:
            x2 = _mix_out(proj, m_c, x2, p, l, B, S, emit_hn=False)
    return x2.reshape(B, S, D)
```

```python
import functools

import jax
import jax.numpy as jnp
from jax import lax
from jax.experimental import pallas as pl
from jax.experimental.pallas import tpu as pltpu

F32 = jnp.float32
BF16 = jnp.bfloat16

D_MODEL = 2048
N_HEADS = 4
HEAD_DIM = 128
W_BR = N_HEADS * HEAD_DIM
N_MIXERS = 4
GLA_HEAD_K = 64
GLA_TAU = 16.0
GLA_CHUNK = 64
LRU_C = 8.0
LRU_CONV = 4
CMP_LEN = 32
CMP_STRIDE = 16
SLC_LEN = 64
SLC_TOPN = 16
SLC_GROUP = 4
WIN = 512
FORCED_SCORE = 1e3
CONF_KERNEL = 31
CONF_HIST = 32
EPS = 1e-6
LOG2_E = 1.4426950408889634
SUBLANES = 8
LANES = 128
VMEM_LIMIT = 48 * 1024 * 1024

_SEGS = (('gla_q', 256), ('gla_k', 256), ('gla_v', 512), ('gla_fg', 16), ('gla_z', 512),
         ('lru_x', 512), ('lru_z', 512),
         ('nsa_q', 512), ('nsa_kc', 128), ('nsa_vc', 128), ('nsa_ks', 128), ('nsa_vs', 128),
         ('nsa_kw', 128), ('nsa_vw', 128), ('nsa_g', 12), ('nsa_z', 512),
         ('conv_v', 512), ('conv_glu', 512), ('conv_z', 512))
_ORDER = ('gla_v', 'gla_z', 'lru_x', 'lru_z', 'nsa_q', 'nsa_z', 'conv_v', 'conv_glu', 'conv_z',
          'gla_q', 'gla_k', 'gla_fg', 'nsa_kc', 'nsa_vc', 'nsa_ks', 'nsa_vs', 'nsa_kw', 'nsa_vw',
          'nsa_g')


def _layout():
    src, off = {}, 0
    for name, w in _SEGS:
        src[name] = (off, w)
        off += w
    dst, off = {}, 0
    for name in _ORDER:
        w = src[name][1]
        wp = -(-w // LANES) * LANES
        assert off % wp == 0
        dst[name] = (off, wp)
        off += wp
    return src, dst, off


_SRC, _DST, D_PROJ_PAD = _layout()


def _relayout_kernel(off_ref, valid_ref, w_ref, o_ref):
    del off_ref
    rows = lax.broadcasted_iota(jnp.int32, o_ref.shape[1:], 0)
    keep = rows < valid_ref[pl.program_id(0)]
    for l in range(o_ref.shape[0]):
        o_ref[l] = jnp.where(keep, w_ref[:, l, :], 0.0).astype(BF16)


def _relayout_w_in(w_in):
    L, D, NP = w_in.shape
    wt = jnp.transpose(w_in, (2, 0, 1))
    offs, valid = [], []
    for name in _ORDER:
        o, w = _SRC[name]
        for b in range(_DST[name][1] // LANES):
            offs.append(o + b * LANES)
            valid.append(min(LANES, w - b * LANES))
    assert max(offs) + LANES <= NP and min(valid) > 0
    grid_spec = pltpu.PrefetchScalarGridSpec(
        num_scalar_prefetch=2,
        grid=(len(offs),),
        in_specs=[pl.BlockSpec((pl.Element(LANES), pl.Element(L), pl.Element(D)),
                               lambda j, off, valid: (off[j], 0, 0))],
        out_specs=pl.BlockSpec((L, LANES, D), lambda j, off, valid: (0, j, 0)))
    return pl.pallas_call(
        _relayout_kernel,
        out_shape=jax.ShapeDtypeStruct((L, D_PROJ_PAD, D), BF16),
        grid_spec=grid_spec,
        compiler_params=_params("parallel"),
        name="relayout_w_in",
    )(jnp.array(offs, jnp.int32), jnp.array(valid, jnp.int32), wt)


def _dot(a, b):
    return jnp.dot(a, b, preferred_element_type=F32)


def _dot_t(a, b):
    return lax.dot_general(a, b, (((1,), (1,)), ((), ())), preferred_element_type=F32)


def _softplus(x):
    return jnp.maximum(x, 0.0) + jnp.log1p(jnp.exp(-jnp.abs(x)))


def _expm1(x):
    return jnp.tanh(0.5 * x) * (jnp.exp(x) + 1.0)


def _silu(x):
    return x * jax.nn.sigmoid(x)


def _rmsnorm(x, g):
    return x * lax.rsqrt(jnp.mean(x * x, axis=-1, keepdims=True) + EPS) * g


def _masked_softmax(s, mask):
    s = jnp.where(mask, s, -jnp.inf)
    m = jnp.max(s, axis=-1, keepdims=True)
    m = jnp.where(jnp.isfinite(m), m, 0.0)
    p = jnp.exp(s - m)
    return p / jnp.maximum(jnp.sum(p, axis=-1, keepdims=True), 1e-30)


def _head_norm_gate(o, z, g):
    outs = []
    for h in range(N_HEADS):
        oh = o[:, h * HEAD_DIM:(h + 1) * HEAD_DIM]
        outs.append(oh * lax.rsqrt(jnp.mean(oh * oh, axis=-1, keepdims=True) + EPS))
    on = jnp.concatenate(outs, axis=-1) * g
    return (on * _silu(z)).astype(BF16)


def _layer_spec(arr, *lead):
    rest = arr.shape[len(lead):]
    idx = tuple(lead) + (0,) * len(rest)
    return pl.BlockSpec((None,) * len(lead) + rest, lambda *_: idx)


def _seg_spec(name, rows, nt):
    off, wp = _DST[name]
    cb = off // wp
    return pl.BlockSpec((rows, wp), lambda b, i: (b * nt + i, cb))


def _params(*sem):
    return pltpu.CompilerParams(dimension_semantics=sem, vmem_limit_bytes=VMEM_LIMIT)


def _prenorm_kernel(x_ref, g_ref, o_ref):
    o_ref[...] = _rmsnorm(x_ref[...], g_ref[...]).astype(BF16)


def _prenorm(x2, g, l, tm=512):
    T, D = x2.shape
    return pl.pallas_call(
        _prenorm_kernel,
        out_shape=jax.ShapeDtypeStruct((T, D), BF16),
        grid=(T // tm,),
        in_specs=[pl.BlockSpec((tm, D), lambda i: (i, 0)), _layer_spec(g, l)],
        out_specs=pl.BlockSpec((tm, D), lambda i: (i, 0)),
        compiler_params=_params("parallel"),
        name="prenorm",
    )(x2, g)


def _in_proj_kernel(h_ref, wt_ref, o_ref):
    o_ref[...] = _dot_t(h_ref[...], wt_ref[...])


def _in_proj(hn, wt, l, tm=1024, tn=2048):
    T, D = hn.shape
    NP = wt.shape[1]
    return pl.pallas_call(
        _in_proj_kernel,
        out_shape=jax.ShapeDtypeStruct((T, NP), F32),
        grid=(T // tm, NP // tn),
        in_specs=[pl.BlockSpec((tm, D), lambda i, j: (i, 0)),
                  pl.BlockSpec((None, tn, D), lambda i, j: (l, j, 0))],
        out_specs=pl.BlockSpec((tm, tn), lambda i, j: (i, j)),
        compiler_params=_params("parallel", "arbitrary"),
        name="in_proj",
    )(hn, wt)


N_GLA_IN, N_LRU_IN, N_CONF_IN = 8, 10, 10


def _mix_out_kernel(*refs, emit_hn):
    refs = list(refs)
    gla_in = [refs.pop(0) for _ in range(N_GLA_IN)]
    lru_in = [refs.pop(0) for _ in range(N_LRU_IN)]
    conf_in = [refs.pop(0) for _ in range(N_CONF_IN)]
    mc_ref, w_ref, x_ref, pg_ref = [refs.pop(0) for _ in range(4)]
    gn_ref = refs.pop(0) if emit_hn else None
    o_ref = refs.pop(0)
    hn_ref = refs.pop(0) if emit_hn else None
    gla_state, gla_acc, lru_hist, lru_h, lru_acc, conf_hist = refs

    @pl.when(pl.program_id(1) == 0)
    def _():
        for ref in (gla_state, lru_hist, lru_h, conf_hist):
            ref[...] = jnp.zeros(ref.shape, F32)

    gla_stages, gla_finish = _gla_stages(*gla_in, gla_state, gla_acc)

    def next_gla_stage():
        if gla_stages:
            gla_stages.pop(0)()

    m_b = _lru_mix(*lru_in, lru_hist, lru_h, lru_acc)
    m_d = _conf_mix(*conf_in, conf_hist, next_gla_stage)
    while gla_stages:
        next_gla_stage()
    mixed = jnp.concatenate([gla_finish(), m_b, mc_ref[...], m_d], axis=-1)
    x_new = x_ref[...] + _rmsnorm(_dot(mixed, w_ref[...]), pg_ref[...])
    o_ref[...] = x_new
    if emit_hn:
        hn_ref[...] = _rmsnorm(x_new, gn_ref[...]).astype(BF16)


def _mix_out(proj, m_c, x2, p, l, B, S, emit_hn, R=256):
    nt = S // R
    T, D = x2.shape
    bg = p['bg']

    def tile(width):
        return pl.BlockSpec((R, width), lambda b, i: (b * nt + i, 0))

    def seg(*names):
        return [_seg_spec(n, R, nt) for n in names]

    def layer(*names):
        return [_layer_spec(p[n], l) for n in names]

    gla = seg('gla_q', 'gla_k', 'gla_v', 'gla_fg', 'gla_z') + layer('wfg', 'bfg') + [_layer_spec(bg, l, 0)]
    lru = (seg('lru_x', 'lru_z') + layer('lru_cw', 'lru_cb', 'wa', 'lru_ba', 'wx', 'lru_bx', 'lru_lam')
           + [_layer_spec(bg, l, 1)])
    conf = (seg('conv_v', 'conv_glu', 'conv_z') + layer('conf_dw', 'conf_db', 'conf_lg', 'conf_lb', 'pw', 'conf_pb')
            + [_layer_spec(bg, l, 3)])
    assert (len(gla), len(lru), len(conf)) == (N_GLA_IN, N_LRU_IN, N_CONF_IN)
    in_specs = gla + lru + conf + [tile(W_BR), _layer_spec(p['w_out'], l), tile(D), _layer_spec(p['post_g'], l)]
    args = ([proj] * 5 + [p['wfg'], p['bfg'], bg]
            + [proj] * 2 + [p[n] for n in ('lru_cw', 'lru_cb', 'wa', 'lru_ba', 'wx', 'lru_bx', 'lru_lam')] + [bg]
            + [proj] * 3 + [p[n] for n in ('conf_dw', 'conf_db', 'conf_lg', 'conf_lb', 'pw', 'conf_pb')] + [bg]
            + [m_c, p['w_out'], x2, p['post_g']])
    out_shape = jax.ShapeDtypeStruct((T, D), F32)
    out_specs = tile(D)
    if emit_hn:
        in_specs.append(_layer_spec(p['pre_g'], l + 1))
        args.append(p['pre_g'])
        out_shape = (out_shape, jax.ShapeDtypeStruct((T, D), BF16))
        out_specs = (tile(D), tile(D))
    return pl.pallas_call(
        functools.partial(_mix_out_kernel, emit_hn=emit_hn),
        out_shape=out_shape,
        grid=(B, nt),
        in_specs=in_specs,
        out_specs=out_specs,
        scratch_shapes=[pltpu.VMEM((N_HEADS, HEAD_DIM, LANES), F32), pltpu.VMEM((R, W_BR), F32),
                        pltpu.VMEM((SUBLANES, W_BR), F32), pltpu.VMEM((1, W_BR), F32),
                        pltpu.VMEM((R, W_BR), F32), pltpu.VMEM((CONF_HIST, W_BR), F32)],
        compiler_params=_params("parallel", "arbitrary"),
        name="mix_out",
    )(*args)


def _gla_stages(q_ref, k_ref, v_ref, fg_ref, z_ref, wfg_ref, bfg_ref, g_ref, state_ref, acc_ref):
    R = q_ref.shape[0]
    C = GLA_CHUNK

    pre = _dot(fg_ref[...].astype(BF16), wfg_ref[...]) + bfg_ref[...]
    log_f = -_softplus(-pre) * (1.0 / GLA_TAU)
    row = lax.broadcasted_iota(jnp.int32, log_f.shape, 0) % C
    bcum = log_f
    d = 1
    while d < C:
        bcum = bcum + jnp.where(row >= d, pltpu.roll(bcum, d, 0), 0.0)
        d *= 2

    q = q_ref[...] * (GLA_HEAD_K ** -0.5)
    k = k_ref[...]
    v = v_ref[...]
    lane = lax.broadcasted_iota(jnp.int32, (C, LANES), 1)
    causal = (lax.broadcasted_iota(jnp.int32, (C, C), 0) >= lax.broadcasted_iota(jnp.int32, (C, C), 1))
    head_lanes = (lane < GLA_HEAD_K, lane >= GLA_HEAD_K)

    def stage(c, p):
        rs = slice(c * C, (c + 1) * C)
        ls = slice(p * LANES, (p + 1) * LANES)
        b = bcum[rs, ls]
        b_last = b[C - 1:C, :]
        kk = k[rs, ls]
        q_dec = q[rs, ls] * jnp.exp(b)
        k_dec = kk * jnp.exp(-b)
        k_last = kk * jnp.exp(b_last - b)
        decay = jnp.exp(b_last)
        for hh in range(2):
            h = 2 * p + hh
            m = head_lanes[hh]
            qd = jnp.where(m, q_dec, 0.0).astype(BF16)
            kd = jnp.where(m, k_dec, 0.0).astype(BF16)
            kl = jnp.where(m, k_last, 0.0).astype(BF16)
            vh = v[rs, h * HEAD_DIM:(h + 1) * HEAD_DIM]
            attn = jnp.where(causal, _dot_t(qd, kd), 0.0)
            st = state_ref[h]
            o = _dot(attn.astype(BF16), vh.astype(BF16)) + _dot_t(qd, st.astype(BF16))
            acc_ref[rs, h * HEAD_DIM:(h + 1) * HEAD_DIM] = o
            state_ref[h] = decay * st + _dot(vh.T.astype(BF16), kl)

    def finish():
        return _head_norm_gate(acc_ref[...], z_ref[...], g_ref[...])

    stages = [functools.partial(stage, c, p) for c in range(R // C) for p in range(2)]
    return stages, finish


def _lru_mix(x_ref, z_ref, cw_ref, cb_ref, wa_ref, ba_ref, wx_ref, bx_ref, lam_ref, g_ref,
             hist_ref, h_ref, acc_ref):
    R = x_ref.shape[0]
    HIST = hist_ref.shape[0]

    x = x_ref[...]
    xe = jnp.concatenate([hist_ref[...], x], axis=0)
    xc = cb_ref[...] + cw_ref[LRU_CONV - 1:LRU_CONV, :] * x
    for kk in range(LRU_CONV - 1):
        sh = LRU_CONV - 1 - kk
        xc = xc + cw_ref[kk:kk + 1, :] * pltpu.roll(xe, sh, 0)[HIST:HIST + R]
    hist_ref[...] = x[R - HIST:R]

    xb = xc.astype(BF16)
    ra, rx = [], []
    for h in range(N_HEADS):
        xh = xb[:, h * HEAD_DIM:(h + 1) * HEAD_DIM]
        ra.append(_dot(xh, wa_ref[h]))
        rx.append(_dot(xh, wx_ref[h]))
    r = jax.nn.sigmoid(jnp.concatenate(ra, axis=-1) + ba_ref[...])
    ig = jax.nn.sigmoid(jnp.concatenate(rx, axis=-1) + bx_ref[...])
    log_a = -LRU_C * r * _softplus(-lam_ref[...])
    a = jnp.exp(log_a)
    u = jnp.sqrt(-_expm1(2.0 * log_a)) * (ig * xc)

    G = R // SUBLANES
    a3 = a.reshape(G, SUBLANES, W_BR)
    u3 = u.reshape(G, SUBLANES, W_BR)
    row = lax.broadcasted_iota(jnp.int32, a3.shape, 1)
    d = 1
    while d < SUBLANES:
        keep = row >= d
        u3 = jnp.where(keep, a3 * pltpu.roll(u3, d, 1) + u3, u3)
        a3 = jnp.where(keep, a3 * pltpu.roll(a3, d, 1), a3)
        d *= 2
    carry = h_ref[...]
    for gi in range(G):
        hg = a3[gi] * carry + u3[gi]
        acc_ref[gi * SUBLANES:(gi + 1) * SUBLANES, :] = hg
        carry = hg[SUBLANES - 1:SUBLANES, :]
    h_ref[...] = carry
    return _head_norm_gate(acc_ref[...], z_ref[...], g_ref[...])


def _conf_mix(v_ref, glu_ref, z_ref, dw_ref, db_ref, lg_ref, lb_ref, pw_ref, pb_ref, g_ref, hist_ref, beside):
    R = v_ref.shape[0]
    HIST = hist_ref.shape[0]

    y = v_ref[...] * jax.nn.sigmoid(glu_ref[...])
    ye = jnp.concatenate([hist_ref[...], y], axis=0)
    acc = db_ref[...]
    for r in range(SUBLANES):
        yr = ye if r == 0 else pltpu.roll(ye, r, 0)
        for q in range((CONF_KERNEL - 1 - r) // SUBLANES + 1):
            kk = CONF_KERNEL - 1 - (SUBLANES * q + r)
            start = HIST - SUBLANES * q
            acc = acc + dw_ref[kk:kk + 1, :] * yr[start:start + R]
        beside()
    hist_ref[...] = y[R - HIST:R]

    mu = jnp.mean(acc, axis=-1, keepdims=True)
    xc = acc - mu
    var = jnp.mean(xc * xc, axis=-1, keepdims=True)
    yn = xc * lax.rsqrt(var + EPS) * lg_ref[...] + lb_ref[...]
    o = _dot(_silu(yn).astype(BF16), pw_ref[...]) + pb_ref[...]
    return _head_norm_gate(o, z_ref[...], g_ref[...])


def _cmp_kernel(kc_ref, vc_ref, posk_ref, w1k_ref, w2k_ref, posv_ref, w1v_ref, w2v_ref, ko_ref, vo_ref):
    S = kc_ref.shape[0]
    NB = S // CMP_STRIDE
    HALF = CMP_LEN // 2

    def compress(z_ref, pos_ref, w1_ref, w2_ref):
        lo = jnp.zeros((NB, w1_ref.shape[2]), F32)
        hi = jnp.zeros((NB, w1_ref.shape[2]), F32)
        for l in range(HALF):
            grp = z_ref[pl.ds(l, NB, stride=CMP_STRIDE), :]
            lo = lo + _dot((grp + pos_ref[l:l + 1, :]).astype(BF16), w1_ref[l])
            hi = hi + _dot((grp + pos_ref[HALF + l:HALF + l + 1, :]).astype(BF16), w1_ref[HALF + l])
        hid = lo + pltpu.roll(hi, NB - 1, 0)
        blk = lax.broadcasted_iota(jnp.int32, hid.shape, 0)
        hid = jnp.where(blk < NB - 1, hid, 0.0)
        return _dot(_silu(hid).astype(BF16), w2_ref[...])

    ko_ref[...] = compress(kc_ref, posk_ref, w1k_ref, w2k_ref)
    vo_ref[...] = compress(vc_ref, posv_ref, w1v_ref, w2v_ref)


def _nsa_compress(proj, posk, w1k, w2k, posv, w1v, w2v, l, B, S):
    NB = S // CMP_STRIDE

    def seg(name):
        off, wp = _DST[name]
        return pl.BlockSpec((S, wp), lambda b: (b, off // wp))

    out = jax.ShapeDtypeStruct((B, NB, HEAD_DIM), F32)
    ospec = pl.BlockSpec((None, NB, HEAD_DIM), lambda b: (b, 0, 0))
    return pl.pallas_call(
        _cmp_kernel,
        out_shape=(out, out),
        grid=(B,),
        in_specs=[seg('nsa_kc'), seg('nsa_vc'), _layer_spec(posk, l), _layer_spec(w1k, l),
                  _layer_spec(w2k, l), _layer_spec(posv, l), _layer_spec(w1v, l), _layer_spec(w2v, l)],
        out_specs=(ospec, ospec),
        compiler_params=_params("parallel"),
        name="nsa_compress",
    )(proj, proj, posk, w1k, w2k, posv, w1v, w2v)


def _nsa_kernel(q_ref, gt_ref, z_ref, kc_ref, vc_ref, ks_ref, vs_ref, kw_ref, vw_ref, g_ref, o_ref,
                m_ref, acc_ref, ocmp_ref, sel_ref):
    TQ = q_ref.shape[0]
    S = ks_ref.shape[0]
    NB = kc_ref.shape[0]
    H, Dh = N_HEADS, HEAD_DIM
    scale = Dh ** -0.5
    i = pl.program_id(1)
    t0 = i * TQ

    q = q_ref[...]
    q4 = jnp.concatenate([q[:, h * Dh:(h + 1) * Dh] for h in range(H)], axis=0).astype(BF16)

    n_sel = S // SLC_LEN
    RANK_STEP = 8

    def selection_stages():
        st = {}

        def compressed():
            t_c = t0 + lax.broadcasted_iota(jnp.int32, (TQ, NB), 0)
            n_c = lax.broadcasted_iota(jnp.int32, (TQ, NB), 1)
            cmask = t_c >= n_c * CMP_STRIDE + (CMP_LEN - 1)
            cmask4 = jnp.concatenate([cmask] * H, axis=0)
            p4 = _masked_softmax(_dot_t(q4, kc_ref[...].astype(BF16)) * scale, cmask4)
            st['p4b'] = p4.astype(BF16)
            ocmp_ref[...] = _dot(st['p4b'], vc_ref[...].astype(BF16))

        def importance():
            jj = lax.broadcasted_iota(jnp.int32, (n_sel, NB), 0)
            nn = lax.broadcasted_iota(jnp.int32, (n_sel, NB), 1)
            ov = ((nn * CMP_STRIDE < (jj + 1) * SLC_LEN) & (nn * CMP_STRIDE + CMP_LEN > jj * SLC_LEN)
                  & (nn < NB - 1))
            po = _dot_t(jnp.where(ov, 1.0, 0.0).astype(BF16), st['p4b'])
            imp = po[:, 0:TQ]
            for h in range(1, H):
                imp = imp + po[:, h * TQ:(h + 1) * TQ]
            j = lax.broadcasted_iota(jnp.int32, (n_sel, TQ), 0)
            cur = (t0 + lax.broadcasted_iota(jnp.int32, (n_sel, TQ), 1)) // SLC_LEN
            forced = (j == 0) | (j == cur) | (j == cur - 1)
            st['j'] = j
            st['val'] = jnp.where(j > cur, -jnp.inf, jnp.where(forced, FORCED_SCORE, imp))
            st['rank'] = jnp.zeros((n_sel, TQ), F32)

        def rank_part(lo):
            val, j = st['val'], st['j']
            for ii in range(lo, lo + RANK_STEP):
                vi = val[ii:ii + 1, :]
                beats = jnp.where(vi > val, 1.0, jnp.where(vi == val, jnp.where(j > ii, 1.0, 0.0), 0.0))
                st['rank'] = st['rank'] + beats

        def select():
            sel_t = jnp.where(st['rank'] < SLC_TOPN, 1.0, 0.0)
            sel_ref[...] = jnp.concatenate([sel_t, jnp.zeros((LANES - n_sel, TQ), F32)], axis=0).T.astype(BF16)

        return ([compressed, importance] + [functools.partial(rank_part, lo) for lo in range(0, n_sel, RANK_STEP)]
                + [select])

    t_k = t0 + lax.broadcasted_iota(jnp.int32, (TQ, TQ), 0)
    k_i = lax.broadcasted_iota(jnp.int32, (TQ, TQ), 1)
    ones_half = jnp.ones((TQ, Dh), BF16)
    neg_inf = jnp.full((TQ, TQ), -jnp.inf, F32)
    scale_log2e = scale * LOG2_E

    def reset():
        m_ref[...] = jnp.full(m_ref.shape, -jnp.inf, F32)
        acc_ref[...] = jnp.zeros(acc_ref.shape, F32)

    def attend_chunk(k_ref, v_ref, c, bias, beside=lambda: None):
        k0 = pl.multiple_of(c * TQ, TQ)
        kb = k_ref[pl.ds(k0, TQ), :].astype(BF16)
        vb = jnp.concatenate([v_ref[pl.ds(k0, TQ), :].astype(BF16), ones_half], axis=-1)
        s4 = _dot_t(q4, kb)
        for h in range(H):
            rs = slice(h * TQ, (h + 1) * TQ)
            s = s4[rs] if bias is None else s4[rs] + bias
            m_old = m_ref[rs]
            m_new = jnp.maximum(m_old, jnp.max(s, axis=-1, keepdims=True))
            m_safe = jnp.where(m_new == -jnp.inf, 0.0, m_new)
            p = jnp.exp2((s - jnp.concatenate([m_safe] * (TQ // LANES), axis=-1)) * scale_log2e)
            alpha = jnp.exp2((m_old - m_safe) * scale_log2e)
            acc_ref[rs] = jnp.concatenate([alpha, alpha], axis=-1) * acc_ref[rs] + _dot(p.astype(BF16), vb)
            m_ref[rs] = m_new
            beside()

    def attend_window_ends(k_ref, v_ref, beside):
        old = k_i > (t_k - t0)
        k_old, k_new = pl.multiple_of((i - n_back) * TQ, TQ), pl.multiple_of(i * TQ, TQ)
        s4_old = _dot_t(q4, k_ref[pl.ds(k_old, TQ), :].astype(BF16))
        s4_new = _dot_t(q4, k_ref[pl.ds(k_new, TQ), :].astype(BF16))
        vb_old = jnp.concatenate([v_ref[pl.ds(k_old, TQ), :].astype(BF16), ones_half], axis=-1)
        vb_new = jnp.concatenate([v_ref[pl.ds(k_new, TQ), :].astype(BF16), ones_half], axis=-1)
        zero = jnp.zeros((TQ, TQ), BF16)
        for h in range(H):
            rs = slice(h * TQ, (h + 1) * TQ)
            s = jnp.where(old, s4_old[rs], s4_new[rs])
            m_old = m_ref[rs]
            m_new = jnp.maximum(m_old, jnp.max(s, axis=-1, keepdims=True))
            p = jnp.exp2((s - jnp.concatenate([m_new] * (TQ // LANES), axis=-1)) * scale_log2e).astype(BF16)
            alpha = jnp.exp2((m_old - m_new) * scale_log2e)
            acc_ref[rs] = (jnp.concatenate([alpha, alpha], axis=-1) * acc_ref[rs]
                           + _dot(jnp.where(old, p, zero), vb_old) + _dot(jnp.where(old, zero, p), vb_new))
            m_ref[rs] = m_new
            beside()

    def finish():
        acc = acc_ref[...]
        return acc[:, :Dh] / jnp.maximum(acc[:, Dh:], 1e-30)

    reset()
    n_back = WIN // TQ

    def win_chunk(back, beside):
        c = i - back
        if 0 < back < n_back:
            bias = None
        else:
            dist = t_k - (c * TQ + k_i)
            bias = jnp.where((dist >= 0) & (dist < WIN), 0.0, neg_inf)
        attend_chunk(kw_ref, vw_ref, c, bias, beside)

    for first in range(n_back, -1, -1):
        def win_chunks(first=first):
            stages = selection_stages()

            def next_stage():
                if stages:
                    stages.pop(0)()

            if first == n_back:
                for back in range(n_back - 1, 0, -1):
                    win_chunk(back, next_stage)
                attend_window_ends(kw_ref, vw_ref, next_stage)
            else:
                for back in range(first, -1, -1):
                    win_chunk(back, next_stage)
            while stages:
                next_stage()
        pl.when(jnp.minimum(i, n_back) == first)(win_chunks)
    o_win4 = finish()
    sel = sel_ref[...]

    eb = lax.broadcasted_iota(jnp.int32, (LANES, TQ), 0)
    ek = lax.broadcasted_iota(jnp.int32, (LANES, TQ), 1) // SLC_LEN

    def sel_keys(c):
        expand = jnp.where(eb == ek + c * (TQ // SLC_LEN), 1.0, 0.0).astype(BF16)
        return _dot(sel, expand) > 0.5

    reset()

    def slc_chunk(c):
        attend_chunk(ks_ref, vs_ref, c, jnp.where(sel_keys(c), 0.0, neg_inf))

    def slc_diagonal():
        attend_chunk(ks_ref, vs_ref, i, jnp.where(sel_keys(i) & (t0 + k_i <= t_k), 0.0, neg_inf))

    def slc_group(cg, carry):
        for u in range(SLC_GROUP):
            slc_chunk(SLC_GROUP * cg + u)
        return carry

    lax.fori_loop(0, i // SLC_GROUP, slc_group, 0)
    for rem in range(SLC_GROUP):
        def slc_tail(rem=rem):
            for u in range(rem, 0, -1):
                slc_chunk(i - u)
            slc_diagonal()
        pl.when(i % SLC_GROUP == rem)(slc_tail)
    o_slc4 = finish()

    o_cmp4 = ocmp_ref[...]
    gs = jax.nn.sigmoid(gt_ref[...])
    outs = []
    for h in range(H):
        rs = slice(h * TQ, (h + 1) * TQ)
        outs.append(gs[:, 3 * h:3 * h + 1] * o_cmp4[rs] + gs[:, 3 * h + 1:3 * h + 2] * o_slc4[rs]
                    + gs[:, 3 * h + 2:3 * h + 3] * o_win4[rs])
    o_ref[...] = _head_norm_gate(jnp.concatenate(outs, axis=-1), z_ref[...], g_ref[...])


def _nsa(proj, k_cmp, v_cmp, bg, l, B, S, TQ=256):
    assert WIN % TQ == 0 and TQ % SLC_LEN == 0 and TQ % LANES == 0
    nt = S // TQ
    T = B * S
    NB = k_cmp.shape[1]

    def seq(name):
        off, wp = _DST[name]
        return pl.BlockSpec((S, wp), lambda b, i: (b, off // wp))

    cspec = pl.BlockSpec((None, NB, HEAD_DIM), lambda b, i: (b, 0, 0))
    return pl.pallas_call(
        _nsa_kernel,
        out_shape=jax.ShapeDtypeStruct((T, W_BR), BF16),
        grid=(B, nt),
        in_specs=[_seg_spec('nsa_q', TQ, nt), _seg_spec('nsa_g', TQ, nt), _seg_spec('nsa_z', TQ, nt),
                  cspec, cspec, seq('nsa_ks'), seq('nsa_vs'), seq('nsa_kw'), seq('nsa_vw'),
                  _layer_spec(bg, l, 2)],
        out_specs=pl.BlockSpec((TQ, W_BR), lambda b, i: (b * nt + i, 0)),
        scratch_shapes=[pltpu.VMEM((N_HEADS * TQ, LANES), F32),
                        pltpu.VMEM((N_HEADS * TQ, 2 * HEAD_DIM), F32),
                        pltpu.VMEM((N_HEADS * TQ, HEAD_DIM), F32), pltpu.VMEM((TQ, LANES), BF16)],
        compiler_params=_params("parallel", "arbitrary"),
        name="nsa_attention",
    )(proj, proj, proj, k_cmp, v_cmp, proj, proj, proj, proj, bg)


def kernel(x, pre_norm_g, w_in, gla_w_fg2, gla_b_fg2, lru_conv_w, lru_conv_b, lru_w_a, lru_b_a, lru_w_x,
           lru_b_x, lru_lambda, nsa_cmp_pos_k, nsa_cmp_w1_k, nsa_cmp_w2_k, nsa_cmp_pos_v, nsa_cmp_w1_v,
           nsa_cmp_w2_v, conf_dw_w, conf_dw_b, conf_ln_g, conf_ln_b, conf_pw_w, conf_pw_b, branch_norm_g,
           w_out, post_norm_g):
    B, S, D = x.shape
    L = w_in.shape[0]
    T = B * S
    x2 = x.reshape(T, D)

    def rows(a):
        return a.reshape(L, 1, -1)

    w_in_t = _relayout_w_in(w_in)
    lowrank = gla_w_fg2.shape[1]
    w1k = nsa_cmp_w1_k.reshape(L, CMP_LEN, HEAD_DIM, -1).astype(BF16)
    w1v = nsa_cmp_w1_v.reshape(L, CMP_LEN, HEAD_DIM, -1).astype(BF16)
    w2k = nsa_cmp_w2_k.astype(BF16)
    w2v = nsa_cmp_w2_v.astype(BF16)
    bg = branch_norm_g.reshape(L, N_MIXERS, 1, W_BR)
    p = dict(
        bg=bg, pre_g=rows(pre_norm_g), post_g=rows(post_norm_g), w_out=w_out.astype(BF16),
        wfg=jnp.pad(gla_w_fg2, ((0, 0), (0, LANES - lowrank), (0, 0))).astype(BF16), bfg=rows(gla_b_fg2),
        lru_cw=lru_conv_w, lru_cb=rows(lru_conv_b), wa=lru_w_a.astype(BF16), lru_ba=rows(lru_b_a),
        wx=lru_w_x.astype(BF16), lru_bx=rows(lru_b_x), lru_lam=rows(lru_lambda),
        conf_dw=conf_dw_w, conf_db=rows(conf_dw_b), conf_lg=rows(conf_ln_g), conf_lb=rows(conf_ln_b),
        pw=conf_pw_w.astype(BF16), conf_pb=rows(conf_pw_b))

    hn = _prenorm(x2, p['pre_g'], 0)
    for l in range(L):
        proj = _in_proj(hn, w_in_t, l)
        k_cmp, v_cmp = _nsa_compress(proj, nsa_cmp_pos_k, w1k, w2k, nsa_cmp_pos_v, w1v, w2v, l, B, S)
        m_c = _nsa(proj, k_cmp, v_cmp, bg, l, B, S)
        if l + 1 < L:
            x2, hn = _mix_out(proj, m_c, x2, p, l, B, S, emit_hn=True)
        else:
            x2 = _mix_out(proj, m_c, x2, p, l, B, S, emit_hn=False)
    return x2.reshape(B, S, D)
```

```python
import functools

import jax
import jax.numpy as jnp
from jax import lax
from jax.experimental import pallas as pl
from jax.experimental.pallas import tpu as pltpu

F32 = jnp.float32
BF16 = jnp.bfloat16

D_MODEL = 2048
N_HEADS = 4
HEAD_DIM = 128
W_BR = N_HEADS * HEAD_DIM
N_MIXERS = 4
GLA_HEAD_K = 64
GLA_TAU = 16.0
GLA_CHUNK = 64
LRU_C = 8.0
LRU_CONV = 4
CMP_LEN = 32
CMP_STRIDE = 16
SLC_LEN = 64
SLC_TOPN = 16
SLC_GROUP = 4
WIN = 512
FORCED_SCORE = 1e3
CONF_KERNEL = 31
CONF_HIST = 32
EPS = 1e-6
LOG2_E = 1.4426950408889634
SUBLANES = 8
LANES = 128
VMEM_LIMIT = 48 * 1024 * 1024

_SEGS = (('gla_q', 256), ('gla_k', 256), ('gla_v', 512), ('gla_fg', 16), ('gla_z', 512),
         ('lru_x', 512), ('lru_z', 512),
         ('nsa_q', 512), ('nsa_kc', 128), ('nsa_vc', 128), ('nsa_ks', 128), ('nsa_vs', 128),
         ('nsa_kw', 128), ('nsa_vw', 128), ('nsa_g', 12), ('nsa_z', 512),
         ('conv_v', 512), ('conv_glu', 512), ('conv_z', 512))
_ORDER = ('gla_v', 'gla_z', 'lru_x', 'lru_z', 'nsa_q', 'nsa_z', 'conv_v', 'conv_glu', 'conv_z',
          'gla_q', 'gla_k', 'gla_fg', 'nsa_kc', 'nsa_vc', 'nsa_ks', 'nsa_vs', 'nsa_kw', 'nsa_vw',
          'nsa_g')


def _layout():
    src, off = {}, 0
    for name, w in _SEGS:
        src[name] = (off, w)
        off += w
    dst, off = {}, 0
    for name in _ORDER:
        w = src[name][1]
        wp = -(-w // LANES) * LANES
        assert off % wp == 0
        dst[name] = (off, wp)
        off += wp
    return src, dst, off


_SRC, _DST, D_PROJ_PAD = _layout()


def _relayout_kernel(off_ref, valid_ref, w_ref, o_ref):
    del off_ref
    rows = lax.broadcasted_iota(jnp.int32, o_ref.shape[1:], 0)
    keep = rows < valid_ref[pl.program_id(0)]
    for l in range(o_ref.shape[0]):
        o_ref[l] = jnp.where(keep, w_ref[:, l, :], 0.0).astype(BF16)


def _relayout_w_in(w_in):
    L, D, NP = w_in.shape
    wt = jnp.transpose(w_in, (2, 0, 1))
    offs, valid = [], []
    for name in _ORDER:
        o, w = _SRC[name]
        for b in range(_DST[name][1] // LANES):
            offs.append(o + b * LANES)
            valid.append(min(LANES, w - b * LANES))
    assert max(offs) + LANES <= NP and min(valid) > 0
    grid_spec = pltpu.PrefetchScalarGridSpec(
        num_scalar_prefetch=2,
        grid=(len(offs),),
        in_specs=[pl.BlockSpec((pl.Element(LANES), pl.Element(L), pl.Element(D)),
                               lambda j, off, valid: (off[j], 0, 0))],
        out_specs=pl.BlockSpec((L, LANES, D), lambda j, off, valid: (0, j, 0)))
    return pl.pallas_call(
        _relayout_kernel,
        out_shape=jax.ShapeDtypeStruct((L, D_PROJ_PAD, D), BF16),
        grid_spec=grid_spec,
        compiler_params=_params("parallel"),
        name="relayout_w_in",
    )(jnp.array(offs, jnp.int32), jnp.array(valid, jnp.int32), wt)


def _dot(a, b):
    return jnp.dot(a, b, preferred_element_type=F32)


def _dot_t(a, b):
    return lax.dot_general(a, b, (((1,), (1,)), ((), ())), preferred_element_type=F32)


def _softplus(x):
    return jnp.maximum(x, 0.0) + jnp.log1p(jnp.exp(-jnp.abs(x)))


def _expm1(x):
    return jnp.tanh(0.5 * x) * (jnp.exp(x) + 1.0)


def _silu(x):
    return x * jax.nn.sigmoid(x)


def _rmsnorm(x, g):
    return x * lax.rsqrt(jnp.mean(x * x, axis=-1, keepdims=True) + EPS) * g


def _masked_softmax(s, mask):
    s = jnp.where(mask, s, -jnp.inf)
    m = jnp.max(s, axis=-1, keepdims=True)
    m = jnp.where(jnp.isfinite(m), m, 0.0)
    p = jnp.exp(s - m)
    return p / jnp.maximum(jnp.sum(p, axis=-1, keepdims=True), 1e-30)


def _head_norm_gate(o, z, g):
    outs = []
    for h in range(N_HEADS):
        oh = o[:, h * HEAD_DIM:(h + 1) * HEAD_DIM]
        outs.append(oh * lax.rsqrt(jnp.mean(oh * oh, axis=-1, keepdims=True) + EPS))
    on = jnp.concatenate(outs, axis=-1) * g
    return (on * _silu(z)).astype(BF16)


def _layer_spec(arr, *lead):
    rest = arr.shape[len(lead):]
    idx = tuple(lead) + (0,) * len(rest)
    return pl.BlockSpec((None,) * len(lead) + rest, lambda *_: idx)


def _seg_spec(name, rows, nt):
    off, wp = _DST[name]
    cb = off // wp
    return pl.BlockSpec((rows, wp), lambda b, i: (b * nt + i, cb))


def _params(*sem):
    return pltpu.CompilerParams(dimension_semantics=sem, vmem_limit_bytes=VMEM_LIMIT)


def _prenorm_kernel(x_ref, g_ref, o_ref):
    o_ref[...] = _rmsnorm(x_ref[...], g_ref[...]).astype(BF16)


def _prenorm(x2, g, l, tm=512):
    T, D = x2.shape
    return pl.pallas_call(
        _prenorm_kernel,
        out_shape=jax.ShapeDtypeStruct((T, D), BF16),
        grid=(T // tm,),
        in_specs=[pl.BlockSpec((tm, D), lambda i: (i, 0)), _layer_spec(g, l)],
        out_specs=pl.BlockSpec((tm, D), lambda i: (i, 0)),
        compiler_params=_params("parallel"),
        name="prenorm",
    )(x2, g)


def _in_proj_kernel(h_ref, wt_ref, o_ref):
    o_ref[...] = _dot_t(h_ref[...], wt_ref[...])


def _in_proj(hn, wt, l, tm=1024, tn=2048):
    T, D = hn.shape
    NP = wt.shape[1]
    return pl.pallas_call(
        _in_proj_kernel,
        out_shape=jax.ShapeDtypeStruct((T, NP), F32),
        grid=(T // tm, NP // tn),
        in_specs=[pl.BlockSpec((tm, D), lambda i, j: (i, 0)),
                  pl.BlockSpec((None, tn, D), lambda i, j: (l, j, 0))],
        out_specs=pl.BlockSpec((tm, tn), lambda i, j: (i, j)),
        compiler_params=_params("parallel", "arbitrary"),
        name="in_proj",
    )(hn, wt)


N_GLA_IN, N_LRU_IN, N_CONF_IN = 8, 10, 10


def _mix_out_kernel(*refs, emit_hn):
    refs = list(refs)
    gla_in = [refs.pop(0) for _ in range(N_GLA_IN)]
    lru_in = [refs.pop(0) for _ in range(N_LRU_IN)]
    conf_in = [refs.pop(0) for _ in range(N_CONF_IN)]
    mc_ref, w_ref, x_ref, pg_ref = [refs.pop(0) for _ in range(4)]
    gn_ref = refs.pop(0) if emit_hn else None
    o_ref = refs.pop(0)
    hn_ref = refs.pop(0) if emit_hn else None
    gla_state, gla_acc, lru_hist, lru_h, lru_acc, conf_hist = refs

    @pl.when(pl.program_id(1) == 0)
    def _():
        for ref in (gla_state, lru_hist, lru_h, conf_hist):
            ref[...] = jnp.zeros(ref.shape, F32)

    gla_stages, gla_finish = _gla_stages(*gla_in, gla_state, gla_acc)

    def next_gla_stage():
        if gla_stages:
            gla_stages.pop(0)()

    m_b = _lru_mix(*lru_in, lru_hist, lru_h, lru_acc)
    m_d = _conf_mix(*conf_in, conf_hist, next_gla_stage)
    while gla_stages:
        next_gla_stage()
    mixed = jnp.concatenate([gla_finish(), m_b, mc_ref[...], m_d], axis=-1)
    x_new = x_ref[...] + _rmsnorm(_dot(mixed, w_ref[...]), pg_ref[...])
    o_ref[...] = x_new
    if emit_hn:
        hn_ref[...] = _rmsnorm(x_new, gn_ref[...]).astype(BF16)


def _mix_out(proj, m_c, x2, p, l, B, S, emit_hn, R=256):
    nt = S // R
    T, D = x2.shape
    bg = p['bg']

    def tile(width):
        return pl.BlockSpec((R, width), lambda b, i: (b * nt + i, 0))

    def seg(*names):
        return [_seg_spec(n, R, nt) for n in names]

    def layer(*names):
        return [_layer_spec(p[n], l) for n in names]

    gla = seg('gla_q', 'gla_k', 'gla_v', 'gla_fg', 'gla_z') + layer('wfg', 'bfg') + [_layer_spec(bg, l, 0)]
    lru = (seg('lru_x', 'lru_z') + layer('lru_cw', 'lru_cb', 'wa', 'lru_ba', 'wx', 'lru_bx', 'lru_lam')
           + [_layer_spec(bg, l, 1)])
    conf = (seg('conv_v', 'conv_glu', 'conv_z') + layer('conf_dw', 'conf_db', 'conf_lg', 'conf_lb', 'pw', 'conf_pb')
            + [_layer_spec(bg, l, 3)])
    assert (len(gla), len(lru), len(conf)) == (N_GLA_IN, N_LRU_IN, N_CONF_IN)
    in_specs = gla + lru + conf + [tile(W_BR), _layer_spec(p['w_out'], l), tile(D), _layer_spec(p['post_g'], l)]
    args = ([proj] * 5 + [p['wfg'], p['bfg'], bg]
            + [proj] * 2 + [p[n] for n in ('lru_cw', 'lru_cb', 'wa', 'lru_ba', 'wx', 'lru_bx', 'lru_lam')] + [bg]
            + [proj] * 3 + [p[n] for n in ('conf_dw', 'conf_db', 'conf_lg', 'conf_lb', 'pw', 'conf_pb')] + [bg]
            + [m_c, p['w_out'], x2, p['post_g']])
    out_shape = jax.ShapeDtypeStruct((T, D), F32)
    out_specs = tile(D)
    if emit_hn:
        in_specs.append(_layer_spec(p['pre_g'], l + 1))
        args.append(p['pre_g'])
        out_shape = (out_shape, jax.ShapeDtypeStruct((T, D), BF16))
        out_specs = (tile(D), tile(D))
    return pl.pallas_call(
        functools.partial(_mix_out_kernel, emit_hn=emit_hn),
        out_shape=out_shape,
        grid=(B, nt),
        in_specs=in_specs,
        out_specs=out_specs,
        scratch_shapes=[pltpu.VMEM((N_HEADS, HEAD_DIM, LANES), F32), pltpu.VMEM((R, W_BR), F32),
                        pltpu.VMEM((SUBLANES, W_BR), F32), pltpu.VMEM((1, W_BR), F32),
                        pltpu.VMEM((R, W_BR), F32), pltpu.VMEM((CONF_HIST, W_BR), F32)],
        compiler_params=_params("parallel", "arbitrary"),
        name="mix_out",
    )(*args)


def _gla_stages(q_ref, k_ref, v_ref, fg_ref, z_ref, wfg_ref, bfg_ref, g_ref, state_ref, acc_ref):
    R = q_ref.shape[0]
    C = GLA_CHUNK

    pre = _dot(fg_ref[...].astype(BF16), wfg_ref[...]) + bfg_ref[...]
    log_f = -_softplus(-pre) * (1.0 / GLA_TAU)
    row = lax.broadcasted_iota(jnp.int32, log_f.shape, 0) % C
    bcum = log_f
    d = 1
    while d < C:
        bcum = bcum + jnp.where(row >= d, pltpu.roll(bcum, d, 0), 0.0)
        d *= 2

    q = q_ref[...] * (GLA_HEAD_K ** -0.5)
    k = k_ref[...]
    v = v_ref[...]
    lane = lax.broadcasted_iota(jnp.int32, (C, LANES), 1)
    causal = (lax.broadcasted_iota(jnp.int32, (C, C), 0) >= lax.broadcasted_iota(jnp.int32, (C, C), 1))
    head_lanes = (lane < GLA_HEAD_K, lane >= GLA_HEAD_K)

    def stage(c, p):
        rs = slice(c * C, (c + 1) * C)
        ls = slice(p * LANES, (p + 1) * LANES)
        b = bcum[rs, ls]
        b_last = b[C - 1:C, :]
        kk = k[rs, ls]
        q_dec = q[rs, ls] * jnp.exp(b)
        k_dec = kk * jnp.exp(-b)
        k_last = kk * jnp.exp(b_last - b)
        decay = jnp.exp(b_last)
        for hh in range(2):
            h = 2 * p + hh
            m = head_lanes[hh]
            qd = jnp.where(m, q_dec, 0.0).astype(BF16)
            kd = jnp.where(m, k_dec, 0.0).astype(BF16)
            kl = jnp.where(m, k_last, 0.0).astype(BF16)
            vh = v[rs, h * HEAD_DIM:(h + 1) * HEAD_DIM]
            attn = jnp.where(causal, _dot_t(qd, kd), 0.0)
            st = state_ref[h]
            o = _dot(attn.astype(BF16), vh.astype(BF16)) + _dot_t(qd, st.astype(BF16))
            acc_ref[rs, h * HEAD_DIM:(h + 1) * HEAD_DIM] = o
            state_ref[h] = decay * st + _dot(vh.T.astype(BF16), kl)

    def finish():
        return _head_norm_gate(acc_ref[...], z_ref[...], g_ref[...])

    stages = [functools.partial(stage, c, p) for c in range(R // C) for p in range(2)]
    return stages, finish


def _lru_mix(x_ref, z_ref, cw_ref, cb_ref, wa_ref, ba_ref, wx_ref, bx_ref, lam_ref, g_ref,
             hist_ref, h_ref, acc_ref):
    R = x_ref.shape[0]
    HIST = hist_ref.shape[0]

    x = x_ref[...]
    xe = jnp.concatenate([hist_ref[...], x], axis=0)
    xc = cb_ref[...] + cw_ref[LRU_CONV - 1:LRU_CONV, :] * x
    for kk in range(LRU_CONV - 1):
        sh = LRU_CONV - 1 - kk
        xc = xc + cw_ref[kk:kk + 1, :] * pltpu.roll(xe, sh, 0)[HIST:HIST + R]
    hist_ref[...] = x[R - HIST:R]

    xb = xc.astype(BF16)
    ra, rx = [], []
    for h in range(N_HEADS):
        xh = xb[:, h * HEAD_DIM:(h + 1) * HEAD_DIM]
        ra.append(_dot(xh, wa_ref[h]))
        rx.append(_dot(xh, wx_ref[h]))
    r = jax.nn.sigmoid(jnp.concatenate(ra, axis=-1) + ba_ref[...])
    ig = jax.nn.sigmoid(jnp.concatenate(rx, axis=-1) + bx_ref[...])
    log_a = -LRU_C * r * _softplus(-lam_ref[...])
    a = jnp.exp(log_a)
    u = jnp.sqrt(-_expm1(2.0 * log_a)) * (ig * xc)

    G = R // SUBLANES
    a3 = a.reshape(G, SUBLANES, W_BR)
    u3 = u.reshape(G, SUBLANES, W_BR)
    row = lax.broadcasted_iota(jnp.int32, a3.shape, 1)
    d = 1
    while d < SUBLANES:
        keep = row >= d
        u3 = jnp.where(keep, a3 * pltpu.roll(u3, d, 1) + u3, u3)
        a3 = jnp.where(keep, a3 * pltpu.roll(a3, d, 1), a3)
        d *= 2
    carry = h_ref[...]
    for gi in range(G):
        hg = a3[gi] * carry + u3[gi]
        acc_ref[gi * SUBLANES:(gi + 1) * SUBLANES, :] = hg
        carry = hg[SUBLANES - 1:SUBLANES, :]
    h_ref[...] = carry
    return _head_norm_gate(acc_ref[...], z_ref[...], g_ref[...])


def _conf_mix(v_ref, glu_ref, z_ref, dw_ref, db_ref, lg_ref, lb_ref, pw_ref, pb_ref, g_ref, hist_ref, beside):
    R = v_ref.shape[0]
    HIST = hist_ref.shape[0]

    y = v_ref[...] * jax.nn.sigmoid(glu_ref[...])
    ye = jnp.concatenate([hist_ref[...], y], axis=0)
    acc = db_ref[...]
    for r in range(SUBLANES):
        yr = ye if r == 0 else pltpu.roll(ye, r, 0)
        for q in range((CONF_KERNEL - 1 - r) // SUBLANES + 1):
            kk = CONF_KERNEL - 1 - (SUBLANES * q + r)
            start = HIST - SUBLANES * q
            acc = acc + dw_ref[kk:kk + 1, :] * yr[start:start + R]
        beside()
    hist_ref[...] = y[R - HIST:R]

    mu = jnp.mean(acc, axis=-1, keepdims=True)
    xc = acc - mu
    var = jnp.mean(xc * xc, axis=-1, keepdims=True)
    yn = xc * lax.rsqrt(var + EPS) * lg_ref[...] + lb_ref[...]
    o = _dot(_silu(yn).astype(BF16), pw_ref[...]) + pb_ref[...]
    return _head_norm_gate(o, z_ref[...], g_ref[...])


def _compress(z_ref, pos_ref, w1_ref, w2_ref):
    NB = z_ref.shape[0] // CMP_STRIDE
    HALF = CMP_LEN // 2
    lo = jnp.zeros((NB, w1_ref.shape[2]), F32)
    hi = jnp.zeros((NB, w1_ref.shape[2]), F32)
    for l in range(HALF):
        grp = z_ref[pl.ds(l, NB, stride=CMP_STRIDE), :]
        lo = lo + _dot((grp + pos_ref[l:l + 1, :]).astype(BF16), w1_ref[l])
        hi = hi + _dot((grp + pos_ref[HALF + l:HALF + l + 1, :]).astype(BF16), w1_ref[HALF + l])
    hid = lo + pltpu.roll(hi, NB - 1, 0)
    blk = lax.broadcasted_iota(jnp.int32, hid.shape, 0)
    hid = jnp.where(blk < NB - 1, hid, 0.0)
    return _dot(_silu(hid).astype(BF16), w2_ref[...])


def _nsa_kernel(q_ref, gt_ref, z_ref, kraw_ref, vraw_ref, posk_ref, w1k_ref, w2k_ref, posv_ref, w1v_ref, w2v_ref,
                ks_ref, vs_ref, kw_ref, vw_ref, g_ref, o_ref, m_ref, acc_ref, ocmp_ref, sel_ref, kc_ref, vc_ref):
    TQ = q_ref.shape[0]
    S = ks_ref.shape[0]
    NB = kc_ref.shape[0]
    H, Dh = N_HEADS, HEAD_DIM
    scale = Dh ** -0.5
    i = pl.program_id(1)
    t0 = i * TQ

    @pl.when(i == 0)
    def _():
        kc_ref[...] = _compress(kraw_ref, posk_ref, w1k_ref, w2k_ref)
        vc_ref[...] = _compress(vraw_ref, posv_ref, w1v_ref, w2v_ref)

    q = q_ref[...]
    q4 = jnp.concatenate([q[:, h * Dh:(h + 1) * Dh] for h in range(H)], axis=0).astype(BF16)

    n_sel = S // SLC_LEN
    RANK_STEP = 8

    def selection_stages():
        st = {}

        def compressed():
            t_c = t0 + lax.broadcasted_iota(jnp.int32, (TQ, NB), 0)
            n_c = lax.broadcasted_iota(jnp.int32, (TQ, NB), 1)
            cmask = t_c >= n_c * CMP_STRIDE + (CMP_LEN - 1)
            cmask4 = jnp.concatenate([cmask] * H, axis=0)
            p4 = _masked_softmax(_dot_t(q4, kc_ref[...].astype(BF16)) * scale, cmask4)
            st['p4b'] = p4.astype(BF16)
            ocmp_ref[...] = _dot(st['p4b'], vc_ref[...].astype(BF16))

        def importance():
            jj = lax.broadcasted_iota(jnp.int32, (n_sel, NB), 0)
            nn = lax.broadcasted_iota(jnp.int32, (n_sel, NB), 1)
            ov = ((nn * CMP_STRIDE < (jj + 1) * SLC_LEN) & (nn * CMP_STRIDE + CMP_LEN > jj * SLC_LEN)
                  & (nn < NB - 1))
            po = _dot_t(jnp.where(ov, 1.0, 0.0).astype(BF16), st['p4b'])
            imp = po[:, 0:TQ]
            for h in range(1, H):
                imp = imp + po[:, h * TQ:(h + 1) * TQ]
            j = lax.broadcasted_iota(jnp.int32, (n_sel, TQ), 0)
            cur = (t0 + lax.broadcasted_iota(jnp.int32, (n_sel, TQ), 1)) // SLC_LEN
            forced = (j == 0) | (j == cur) | (j == cur - 1)
            st['j'] = j
            st['val'] = jnp.where(j > cur, -jnp.inf, jnp.where(forced, FORCED_SCORE, imp))
            st['rank'] = jnp.zeros((n_sel, TQ), F32)

        def rank_part(lo):
            val, j = st['val'], st['j']
            for ii in range(lo, lo + RANK_STEP):
                vi = val[ii:ii + 1, :]
                beats = jnp.where(vi > val, 1.0, jnp.where(vi == val, jnp.where(j > ii, 1.0, 0.0), 0.0))
                st['rank'] = st['rank'] + beats

        def select():
            sel_t = jnp.where(st['rank'] < SLC_TOPN, 1.0, 0.0)
            sel_ref[...] = jnp.concatenate([sel_t, jnp.zeros((LANES - n_sel, TQ), F32)], axis=0).T.astype(BF16)

        return ([compressed, importance] + [functools.partial(rank_part, lo) for lo in range(0, n_sel, RANK_STEP)]
                + [select])

    t_k = t0 + lax.broadcasted_iota(jnp.int32, (TQ, TQ), 0)
    k_i = lax.broadcasted_iota(jnp.int32, (TQ, TQ), 1)
    ones_half = jnp.ones((TQ, Dh), BF16)
    neg_inf = jnp.full((TQ, TQ), -jnp.inf, F32)
    scale_log2e = scale * LOG2_E

    def reset():
        m_ref[...] = jnp.full(m_ref.shape, -jnp.inf, F32)
        acc_ref[...] = jnp.zeros(acc_ref.shape, F32)

    def attend_chunk(k_ref, v_ref, c, bias, beside=lambda: None):
        k0 = pl.multiple_of(c * TQ, TQ)
        kb = k_ref[pl.ds(k0, TQ), :].astype(BF16)
        vb = jnp.concatenate([v_ref[pl.ds(k0, TQ), :].astype(BF16), ones_half], axis=-1)
        s4 = _dot_t(q4, kb)
        for h in range(H):
            rs = slice(h * TQ, (h + 1) * TQ)
            s = s4[rs] if bias is None else s4[rs] + bias
            m_old = m_ref[rs]
            m_new = jnp.maximum(m_old, jnp.max(s, axis=-1, keepdims=True))
            m_safe = jnp.where(m_new == -jnp.inf, 0.0, m_new)
            p = jnp.exp2((s - jnp.concatenate([m_safe] * (TQ // LANES), axis=-1)) * scale_log2e)
            alpha = jnp.exp2((m_old - m_safe) * scale_log2e)
            acc_ref[rs] = jnp.concatenate([alpha, alpha], axis=-1) * acc_ref[rs] + _dot(p.astype(BF16), vb)
            m_ref[rs] = m_new
            beside()

    def attend_window_ends(k_ref, v_ref, beside):
        old = k_i > (t_k - t0)
        k_old, k_new = pl.multiple_of((i - n_back) * TQ, TQ), pl.multiple_of(i * TQ, TQ)
        s4_old = _dot_t(q4, k_ref[pl.ds(k_old, TQ), :].astype(BF16))
        s4_new = _dot_t(q4, k_ref[pl.ds(k_new, TQ), :].astype(BF16))
        vb_old = jnp.concatenate([v_ref[pl.ds(k_old, TQ), :].astype(BF16), ones_half], axis=-1)
        vb_new = jnp.concatenate([v_ref[pl.ds(k_new, TQ), :].astype(BF16), ones_half], axis=-1)
        zero = jnp.zeros((TQ, TQ), BF16)
        for h in range(H):
            rs = slice(h * TQ, (h + 1) * TQ)
            s = jnp.where(old, s4_old[rs], s4_new[rs])
            m_old = m_ref[rs]
            m_new = jnp.maximum(m_old, jnp.max(s, axis=-1, keepdims=True))
            p = jnp.exp2((s - jnp.concatenate([m_new] * (TQ // LANES), axis=-1)) * scale_log2e).astype(BF16)
            alpha = jnp.exp2((m_old - m_new) * scale_log2e)
            acc_ref[rs] = (jnp.concatenate([alpha, alpha], axis=-1) * acc_ref[rs]
                           + _dot(jnp.where(old, p, zero), vb_old) + _dot(jnp.where(old, zero, p), vb_new))
            m_ref[rs] = m_new
            beside()

    def finish():
        acc = acc_ref[...]
        return acc[:, :Dh] / jnp.maximum(acc[:, Dh:], 1e-30)

    reset()
    n_back = WIN // TQ

    def win_chunk(back, beside):
        c = i - back
        if 0 < back < n_back:
            bias = None
        else:
            dist = t_k - (c * TQ + k_i)
            bias = jnp.where((dist >= 0) & (dist < WIN), 0.0, neg_inf)
        attend_chunk(kw_ref, vw_ref, c, bias, beside)

    for first in range(n_back, -1, -1):
        def win_chunks(first=first):
            stages = selection_stages()

            def next_stage():
                if stages:
                    stages.pop(0)()

            if first == n_back:
                for back in range(n_back - 1, 0, -1):
                    win_chunk(back, next_stage)
                attend_window_ends(kw_ref, vw_ref, next_stage)
            else:
                for back in range(first, -1, -1):
                    win_chunk(back, next_stage)
            while stages:
                next_stage()
        pl.when(jnp.minimum(i, n_back) == first)(win_chunks)
    o_win4 = finish()
    sel = sel_ref[...]

    eb = lax.broadcasted_iota(jnp.int32, (LANES, TQ), 0)
    ek = lax.broadcasted_iota(jnp.int32, (LANES, TQ), 1) // SLC_LEN

    def sel_keys(c):
        expand = jnp.where(eb == ek + c * (TQ // SLC_LEN), 1.0, 0.0).astype(BF16)
        return _dot(sel, expand) > 0.5

    reset()

    def slc_chunk(c):
        attend_chunk(ks_ref, vs_ref, c, jnp.where(sel_keys(c), 0.0, neg_inf))

    def slc_diagonal():
        attend_chunk(ks_ref, vs_ref, i, jnp.where(sel_keys(i) & (t0 + k_i <= t_k), 0.0, neg_inf))

    def slc_group(cg, carry):
        for u in range(SLC_GROUP):
            slc_chunk(SLC_GROUP * cg + u)
        return carry

    lax.fori_loop(0, i // SLC_GROUP, slc_group, 0)
    for rem in range(SLC_GROUP):
        def slc_tail(rem=rem):
            for u in range(rem, 0, -1):
                slc_chunk(i - u)
            slc_diagonal()
        pl.when(i % SLC_GROUP == rem)(slc_tail)
    o_slc4 = finish()

    o_cmp4 = ocmp_ref[...]
    gs = jax.nn.sigmoid(gt_ref[...])
    outs = []
    for h in range(H):
        rs = slice(h * TQ, (h + 1) * TQ)
        outs.append(gs[:, 3 * h:3 * h + 1] * o_cmp4[rs] + gs[:, 3 * h + 1:3 * h + 2] * o_slc4[rs]
                    + gs[:, 3 * h + 2:3 * h + 3] * o_win4[rs])
    o_ref[...] = _head_norm_gate(jnp.concatenate(outs, axis=-1), z_ref[...], g_ref[...])


def _nsa(proj, cmp_w, bg, l, B, S, TQ=256):
    assert WIN % TQ == 0 and TQ % SLC_LEN == 0 and TQ % LANES == 0
    nt = S // TQ
    T = B * S
    NB = S // CMP_STRIDE

    def seq(name):
        off, wp = _DST[name]
        return pl.BlockSpec((S, wp), lambda b, i: (b, off // wp))

    return pl.pallas_call(
        _nsa_kernel,
        out_shape=jax.ShapeDtypeStruct((T, W_BR), BF16),
        grid=(B, nt),
        in_specs=[_seg_spec('nsa_q', TQ, nt), _seg_spec('nsa_g', TQ, nt), _seg_spec('nsa_z', TQ, nt),
                  seq('nsa_kc'), seq('nsa_vc')] + [_layer_spec(w, l) for w in cmp_w]
                 + [seq('nsa_ks'), seq('nsa_vs'), seq('nsa_kw'), seq('nsa_vw'), _layer_spec(bg, l, 2)],
        out_specs=pl.BlockSpec((TQ, W_BR), lambda b, i: (b * nt + i, 0)),
        scratch_shapes=[pltpu.VMEM((N_HEADS * TQ, LANES), F32),
                        pltpu.VMEM((N_HEADS * TQ, 2 * HEAD_DIM), F32),
                        pltpu.VMEM((N_HEADS * TQ, HEAD_DIM), F32), pltpu.VMEM((TQ, LANES), BF16),
                        pltpu.VMEM((NB, HEAD_DIM), F32), pltpu.VMEM((NB, HEAD_DIM), F32)],
        compiler_params=_params("parallel", "arbitrary"),
        name="nsa_attention",
    )(proj, proj, proj, proj, proj, *cmp_w, proj, proj, proj, proj, bg)


def kernel(x, pre_norm_g, w_in, gla_w_fg2, gla_b_fg2, lru_conv_w, lru_conv_b, lru_w_a, lru_b_a, lru_w_x,
           lru_b_x, lru_lambda, nsa_cmp_pos_k, nsa_cmp_w1_k, nsa_cmp_w2_k, nsa_cmp_pos_v, nsa_cmp_w1_v,
           nsa_cmp_w2_v, conf_dw_w, conf_dw_b, conf_ln_g, conf_ln_b, conf_pw_w, conf_pw_b, branch_norm_g,
           w_out, post_norm_g):
    B, S, D = x.shape
    L = w_in.shape[0]
    T = B * S
    x2 = x.reshape(T, D)

    def rows(a):
        return a.reshape(L, 1, -1)

    w_in_t = _relayout_w_in(w_in)
    lowrank = gla_w_fg2.shape[1]
    w1k = nsa_cmp_w1_k.reshape(L, CMP_LEN, HEAD_DIM, -1).astype(BF16)
    w1v = nsa_cmp_w1_v.reshape(L, CMP_LEN, HEAD_DIM, -1).astype(BF16)
    w2k = nsa_cmp_w2_k.astype(BF16)
    w2v = nsa_cmp_w2_v.astype(BF16)
    bg = branch_norm_g.reshape(L, N_MIXERS, 1, W_BR)
    p = dict(
        bg=bg, pre_g=rows(pre_norm_g), post_g=rows(post_norm_g), w_out=w_out.astype(BF16),
        wfg=jnp.pad(gla_w_fg2, ((0, 0), (0, LANES - lowrank), (0, 0))).astype(BF16), bfg=rows(gla_b_fg2),
        lru_cw=lru_conv_w, lru_cb=rows(lru_conv_b), wa=lru_w_a.astype(BF16), lru_ba=rows(lru_b_a),
        wx=lru_w_x.astype(BF16), lru_bx=rows(lru_b_x), lru_lam=rows(lru_lambda),
        conf_dw=conf_dw_w, conf_db=rows(conf_dw_b), conf_lg=rows(conf_ln_g), conf_lb=rows(conf_ln_b),
        pw=conf_pw_w.astype(BF16), conf_pb=rows(conf_pw_b))

    hn = _prenorm(x2, p['pre_g'], 0)
    for l in range(L):
        proj = _in_proj(hn, w_in_t, l)
        m_c = _nsa(proj, (nsa_cmp_pos_k, w1k, w2k, nsa_cmp_pos_v, w1v, w2v), bg, l, B, S)
        if l + 1 < L:
            x2, hn = _mix_out(proj, m_c, x2, p, l, B, S, emit_hn=True)
        else:
            x2 = _mix_out(proj, m_c, x2, p, l, B, S, emit_hn=False)
    return x2.reshape(B, S, D)
```

```python
import functools

import jax
import jax.numpy as jnp
from jax import lax
from jax.experimental import pallas as pl
from jax.experimental.pallas import tpu as pltpu

F32 = jnp.float32
BF16 = jnp.bfloat16

D_MODEL = 2048
N_HEADS = 4
HEAD_DIM = 128
W_BR = N_HEADS * HEAD_DIM
N_MIXERS = 4
GLA_HEAD_K = 64
GLA_TAU = 16.0
GLA_CHUNK = 64
LRU_C = 8.0
LRU_CONV = 4
CMP_LEN = 32
CMP_STRIDE = 16
SLC_LEN = 64
SLC_TOPN = 16
SLC_GROUP = 4
WIN = 512
FORCED_SCORE = 1e3
CONF_KERNEL = 31
CONF_HIST = 32
EPS = 1e-6
LOG2_E = 1.4426950408889634
SUBLANES = 8
LANES = 128
VMEM_LIMIT = 48 * 1024 * 1024
VMEM_LIMIT_RESIDENT = 56 * 1024 * 1024

_SEGS = (('gla_q', 256), ('gla_k', 256), ('gla_v', 512), ('gla_fg', 16), ('gla_z', 512),
         ('lru_x', 512), ('lru_z', 512),
         ('nsa_q', 512), ('nsa_kc', 128), ('nsa_vc', 128), ('nsa_ks', 128), ('nsa_vs', 128),
         ('nsa_kw', 128), ('nsa_vw', 128), ('nsa_g', 12), ('nsa_z', 512),
         ('conv_v', 512), ('conv_glu', 512), ('conv_z', 512))
_ORDER = ('gla_v', 'gla_z', 'lru_x', 'lru_z', 'nsa_q', 'nsa_z', 'conv_v', 'conv_glu', 'conv_z',
          'gla_q', 'gla_k', 'gla_fg', 'nsa_kc', 'nsa_vc', 'nsa_ks', 'nsa_vs', 'nsa_kw', 'nsa_vw',
          'nsa_g')


def _layout():
    src, off = {}, 0
    for name, w in _SEGS:
        src[name] = (off, w)
        off += w
    dst, off = {}, 0
    for name in _ORDER:
        w = src[name][1]
        wp = -(-w // LANES) * LANES
        assert off % wp == 0
        dst[name] = (off, wp)
        off += wp
    return src, dst, off


_SRC, _DST, D_PROJ_PAD = _layout()


def _relayout_kernel(off_ref, valid_ref, w_ref, o_ref):
    del off_ref
    rows = lax.broadcasted_iota(jnp.int32, o_ref.shape[1:], 0)
    keep = rows < valid_ref[pl.program_id(0)]
    for l in range(o_ref.shape[0]):
        o_ref[l] = jnp.where(keep, w_ref[:, l, :], 0.0).astype(BF16)


def _relayout_w_in(w_in):
    L, D, NP = w_in.shape
    wt = jnp.transpose(w_in, (2, 0, 1))
    offs, valid = [], []
    for name in _ORDER:
        o, w = _SRC[name]
        for b in range(_DST[name][1] // LANES):
            offs.append(o + b * LANES)
            valid.append(min(LANES, w - b * LANES))
    assert max(offs) + LANES <= NP and min(valid) > 0
    grid_spec = pltpu.PrefetchScalarGridSpec(
        num_scalar_prefetch=2,
        grid=(len(offs),),
        in_specs=[pl.BlockSpec((pl.Element(LANES), pl.Element(L), pl.Element(D)),
                               lambda j, off, valid: (off[j], 0, 0))],
        out_specs=pl.BlockSpec((L, LANES, D), lambda j, off, valid: (0, j, 0)))
    return pl.pallas_call(
        _relayout_kernel,
        out_shape=jax.ShapeDtypeStruct((L, D_PROJ_PAD, D), BF16),
        grid_spec=grid_spec,
        compiler_params=_params("parallel"),
        name="relayout_w_in",
    )(jnp.array(offs, jnp.int32), jnp.array(valid, jnp.int32), wt)


def _dot(a, b):
    return jnp.dot(a, b, preferred_element_type=F32)


def _dot_t(a, b):
    return lax.dot_general(a, b, (((1,), (1,)), ((), ())), preferred_element_type=F32)


def _softplus(x):
    return jnp.maximum(x, 0.0) + jnp.log1p(jnp.exp(-jnp.abs(x)))


def _expm1(x):
    return jnp.tanh(0.5 * x) * (jnp.exp(x) + 1.0)


def _silu(x):
    return x * jax.nn.sigmoid(x)


def _rmsnorm(x, g):
    return x * lax.rsqrt(jnp.mean(x * x, axis=-1, keepdims=True) + EPS) * g


def _masked_softmax(s, mask):
    s = jnp.where(mask, s, -jnp.inf)
    m = jnp.max(s, axis=-1, keepdims=True)
    m = jnp.where(jnp.isfinite(m), m, 0.0)
    p = jnp.exp(s - m)
    return p / jnp.maximum(jnp.sum(p, axis=-1, keepdims=True), 1e-30)


def _head_norm_gate(o, z, g):
    outs = []
    for h in range(N_HEADS):
        oh = o[:, h * HEAD_DIM:(h + 1) * HEAD_DIM]
        outs.append(oh * lax.rsqrt(jnp.mean(oh * oh, axis=-1, keepdims=True) + EPS))
    on = jnp.concatenate(outs, axis=-1) * g
    return (on * _silu(z)).astype(BF16)


def _layer_spec(arr, *lead):
    rest = arr.shape[len(lead):]
    idx = tuple(lead) + (0,) * len(rest)
    return pl.BlockSpec((None,) * len(lead) + rest, lambda *_: idx)


def _seg_spec(name, rows, nt):
    off, wp = _DST[name]
    cb = off // wp
    return pl.BlockSpec((rows, wp), lambda b, i: (b * nt + i, cb))


def _params(*sem):
    return pltpu.CompilerParams(dimension_semantics=sem, vmem_limit_bytes=VMEM_LIMIT)


def _prenorm_kernel(x_ref, g_ref, o_ref):
    o_ref[...] = _rmsnorm(x_ref[...], g_ref[...]).astype(BF16)


def _prenorm(x2, g, l, tm=512):
    T, D = x2.shape
    return pl.pallas_call(
        _prenorm_kernel,
        out_shape=jax.ShapeDtypeStruct((T, D), BF16),
        grid=(T // tm,),
        in_specs=[pl.BlockSpec((tm, D), lambda i: (i, 0)), _layer_spec(g, l)],
        out_specs=pl.BlockSpec((tm, D), lambda i: (i, 0)),
        compiler_params=_params("parallel"),
        name="prenorm",
    )(x2, g)


def _in_proj_kernel(h_ref, wt_ref, o_ref):
    tn = o_ref.shape[1]
    rows = pl.ds(pl.multiple_of(pl.program_id(1) * tn, tn), tn)
    o_ref[...] = _dot_t(h_ref[...], wt_ref[rows, :])


def _in_proj(hn, wt, l, tm=1024, tn=2048):
    T, D = hn.shape
    NP = wt.shape[1]
    return pl.pallas_call(
        _in_proj_kernel,
        out_shape=jax.ShapeDtypeStruct((T, NP), F32),
        grid=(T // tm, NP // tn),
        in_specs=[pl.BlockSpec((tm, D), lambda i, j: (i, 0)), _layer_spec(wt, l)],
        out_specs=pl.BlockSpec((tm, tn), lambda i, j: (i, j)),
        compiler_params=pltpu.CompilerParams(dimension_semantics=("parallel", "arbitrary"),
                                             vmem_limit_bytes=VMEM_LIMIT_RESIDENT),
        name="in_proj",
    )(hn, wt)


N_GLA_IN, N_LRU_IN, N_CONF_IN = 8, 10, 10


def _mix_out_kernel(*refs, emit_hn):
    refs = list(refs)
    gla_in = [refs.pop(0) for _ in range(N_GLA_IN)]
    lru_in = [refs.pop(0) for _ in range(N_LRU_IN)]
    conf_in = [refs.pop(0) for _ in range(N_CONF_IN)]
    mc_ref, w_ref, x_ref, pg_ref = [refs.pop(0) for _ in range(4)]
    gn_ref = refs.pop(0) if emit_hn else None
    o_ref = refs.pop(0)
    hn_ref = refs.pop(0) if emit_hn else None
    gla_state, gla_acc, lru_hist, lru_h, lru_acc, conf_hist = refs

    @pl.when(pl.program_id(1) == 0)
    def _():
        for ref in (gla_state, lru_hist, lru_h, conf_hist):
            ref[...] = jnp.zeros(ref.shape, F32)

    gla_stages, gla_finish = _gla_stages(*gla_in, gla_state, gla_acc)

    def next_gla_stage():
        if gla_stages:
            gla_stages.pop(0)()

    m_b = _lru_mix(*lru_in, lru_hist, lru_h, lru_acc)
    m_d = _conf_mix(*conf_in, conf_hist, next_gla_stage)
    while gla_stages:
        next_gla_stage()
    mixed = jnp.concatenate([gla_finish(), m_b, mc_ref[...], m_d], axis=-1)
    x_new = x_ref[...] + _rmsnorm(_dot(mixed, w_ref[...]), pg_ref[...])
    o_ref[...] = x_new
    if emit_hn:
        hn_ref[...] = _rmsnorm(x_new, gn_ref[...]).astype(BF16)


def _mix_out(proj, m_c, x2, p, l, B, S, emit_hn, R=256):
    nt = S // R
    T, D = x2.shape
    bg = p['bg']

    def tile(width):
        return pl.BlockSpec((R, width), lambda b, i: (b * nt + i, 0))

    def seg(*names):
        return [_seg_spec(n, R, nt) for n in names]

    def layer(*names):
        return [_layer_spec(p[n], l) for n in names]

    gla = seg('gla_q', 'gla_k', 'gla_v', 'gla_fg', 'gla_z') + layer('wfg', 'bfg') + [_layer_spec(bg, l, 0)]
    lru = (seg('lru_x', 'lru_z') + layer('lru_cw', 'lru_cb', 'wa', 'lru_ba', 'wx', 'lru_bx', 'lru_lam')
           + [_layer_spec(bg, l, 1)])
    conf = (seg('conv_v', 'conv_glu', 'conv_z') + layer('conf_dw', 'conf_db', 'conf_lg', 'conf_lb', 'pw', 'conf_pb')
            + [_layer_spec(bg, l, 3)])
    assert (len(gla), len(lru), len(conf)) == (N_GLA_IN, N_LRU_IN, N_CONF_IN)
    in_specs = gla + lru + conf + [tile(W_BR), _layer_spec(p['w_out'], l), tile(D), _layer_spec(p['post_g'], l)]
    args = ([proj] * 5 + [p['wfg'], p['bfg'], bg]
            + [proj] * 2 + [p[n] for n in ('lru_cw', 'lru_cb', 'wa', 'lru_ba', 'wx', 'lru_bx', 'lru_lam')] + [bg]
            + [proj] * 3 + [p[n] for n in ('conf_dw', 'conf_db', 'conf_lg', 'conf_lb', 'pw', 'conf_pb')] + [bg]
            + [m_c, p['w_out'], x2, p['post_g']])
    out_shape = jax.ShapeDtypeStruct((T, D), F32)
    out_specs = tile(D)
    if emit_hn:
        in_specs.append(_layer_spec(p['pre_g'], l + 1))
        args.append(p['pre_g'])
        out_shape = (out_shape, jax.ShapeDtypeStruct((T, D), BF16))
        out_specs = (tile(D), tile(D))
    return pl.pallas_call(
        functools.partial(_mix_out_kernel, emit_hn=emit_hn),
        out_shape=out_shape,
        grid=(B, nt),
        in_specs=in_specs,
        out_specs=out_specs,
        scratch_shapes=[pltpu.VMEM((N_HEADS, HEAD_DIM, LANES), F32), pltpu.VMEM((R, W_BR), F32),
                        pltpu.VMEM((SUBLANES, W_BR), F32), pltpu.VMEM((1, W_BR), F32),
                        pltpu.VMEM((R, W_BR), F32), pltpu.VMEM((CONF_HIST, W_BR), F32)],
        compiler_params=_params("parallel", "arbitrary"),
        name="mix_out",
    )(*args)


def _gla_stages(q_ref, k_ref, v_ref, fg_ref, z_ref, wfg_ref, bfg_ref, g_ref, state_ref, acc_ref):
    R = q_ref.shape[0]
    C = GLA_CHUNK

    pre = _dot(fg_ref[...].astype(BF16), wfg_ref[...]) + bfg_ref[...]
    log_f = -_softplus(-pre) * (1.0 / GLA_TAU)
    row = lax.broadcasted_iota(jnp.int32, log_f.shape, 0) % C
    bcum = log_f
    d = 1
    while d < C:
        bcum = bcum + jnp.where(row >= d, pltpu.roll(bcum, d, 0), 0.0)
        d *= 2

    q = q_ref[...] * (GLA_HEAD_K ** -0.5)
    k = k_ref[...]
    v = v_ref[...]
    lane = lax.broadcasted_iota(jnp.int32, (C, LANES), 1)
    causal = (lax.broadcasted_iota(jnp.int32, (C, C), 0) >= lax.broadcasted_iota(jnp.int32, (C, C), 1))
    head_lanes = (lane < GLA_HEAD_K, lane >= GLA_HEAD_K)

    def stage(c, p):
        rs = slice(c * C, (c + 1) * C)
        ls = slice(p * LANES, (p + 1) * LANES)
        b = bcum[rs, ls]
        b_last = b[C - 1:C, :]
        kk = k[rs, ls]
        q_dec = q[rs, ls] * jnp.exp(b)
        k_dec = kk * jnp.exp(-b)
        k_last = kk * jnp.exp(b_last - b)
        decay = jnp.exp(b_last)
        for hh in range(2):
            h = 2 * p + hh
            m = head_lanes[hh]
            qd = jnp.where(m, q_dec, 0.0).astype(BF16)
            kd = jnp.where(m, k_dec, 0.0).astype(BF16)
            kl = jnp.where(m, k_last, 0.0).astype(BF16)
            vh = v[rs, h * HEAD_DIM:(h + 1) * HEAD_DIM]
            attn = jnp.where(causal, _dot_t(qd, kd), 0.0)
            st = state_ref[h]
            o = _dot(attn.astype(BF16), vh.astype(BF16)) + _dot_t(qd, st.astype(BF16))
            acc_ref[rs, h * HEAD_DIM:(h + 1) * HEAD_DIM] = o
            state_ref[h] = decay * st + _dot(vh.T.astype(BF16), kl)

    def finish():
        return _head_norm_gate(acc_ref[...], z_ref[...], g_ref[...])

    stages = [functools.partial(stage, c, p) for c in range(R // C) for p in range(2)]
    return stages, finish


def _lru_mix(x_ref, z_ref, cw_ref, cb_ref, wa_ref, ba_ref, wx_ref, bx_ref, lam_ref, g_ref,
             hist_ref, h_ref, acc_ref):
    R = x_ref.shape[0]
    HIST = hist_ref.shape[0]

    x = x_ref[...]
    xe = jnp.concatenate([hist_ref[...], x], axis=0)
    xc = cb_ref[...] + cw_ref[LRU_CONV - 1:LRU_CONV, :] * x
    for kk in range(LRU_CONV - 1):
        sh = LRU_CONV - 1 - kk
        xc = xc + cw_ref[kk:kk + 1, :] * pltpu.roll(xe, sh, 0)[HIST:HIST + R]
    hist_ref[...] = x[R - HIST:R]

    xb = xc.astype(BF16)
    ra, rx = [], []
    for h in range(N_HEADS):
        xh = xb[:, h * HEAD_DIM:(h + 1) * HEAD_DIM]
        ra.append(_dot(xh, wa_ref[h]))
        rx.append(_dot(xh, wx_ref[h]))
    r = jax.nn.sigmoid(jnp.concatenate(ra, axis=-1) + ba_ref[...])
    ig = jax.nn.sigmoid(jnp.concatenate(rx, axis=-1) + bx_ref[...])
    log_a = -LRU_C * r * _softplus(-lam_ref[...])
    a = jnp.exp(log_a)
    u = jnp.sqrt(-_expm1(2.0 * log_a)) * (ig * xc)

    G = R // SUBLANES
    a3 = a.reshape(G, SUBLANES, W_BR)
    u3 = u.reshape(G, SUBLANES, W_BR)
    row = lax.broadcasted_iota(jnp.int32, a3.shape, 1)
    d = 1
    while d < SUBLANES:
        keep = row >= d
        u3 = jnp.where(keep, a3 * pltpu.roll(u3, d, 1) + u3, u3)
        a3 = jnp.where(keep, a3 * pltpu.roll(a3, d, 1), a3)
        d *= 2
    carry = h_ref[...]
    for gi in range(G):
        hg = a3[gi] * carry + u3[gi]
        acc_ref[gi * SUBLANES:(gi + 1) * SUBLANES, :] = hg
        carry = hg[SUBLANES - 1:SUBLANES, :]
    h_ref[...] = carry
    return _head_norm_gate(acc_ref[...], z_ref[...], g_ref[...])


def _conf_mix(v_ref, glu_ref, z_ref, dw_ref, db_ref, lg_ref, lb_ref, pw_ref, pb_ref, g_ref, hist_ref, beside):
    R = v_ref.shape[0]
    HIST = hist_ref.shape[0]

    y = v_ref[...] * jax.nn.sigmoid(glu_ref[...])
    ye = jnp.concatenate([hist_ref[...], y], axis=0)
    acc = db_ref[...]
    for r in range(SUBLANES):
        yr = ye if r == 0 else pltpu.roll(ye, r, 0)
        for q in range((CONF_KERNEL - 1 - r) // SUBLANES + 1):
            kk = CONF_KERNEL - 1 - (SUBLANES * q + r)
            start = HIST - SUBLANES * q
            acc = acc + dw_ref[kk:kk + 1, :] * yr[start:start + R]
        beside()
    hist_ref[...] = y[R - HIST:R]

    mu = jnp.mean(acc, axis=-1, keepdims=True)
    xc = acc - mu
    var = jnp.mean(xc * xc, axis=-1, keepdims=True)
    yn = xc * lax.rsqrt(var + EPS) * lg_ref[...] + lb_ref[...]
    o = _dot(_silu(yn).astype(BF16), pw_ref[...]) + pb_ref[...]
    return _head_norm_gate(o, z_ref[...], g_ref[...])


def _compress(z_ref, pos_ref, w1_ref, w2_ref):
    NB = z_ref.shape[0] // CMP_STRIDE
    HALF = CMP_LEN // 2
    lo = jnp.zeros((NB, w1_ref.shape[2]), F32)
    hi = jnp.zeros((NB, w1_ref.shape[2]), F32)
    for l in range(HALF):
        grp = z_ref[pl.ds(l, NB, stride=CMP_STRIDE), :]
        lo = lo + _dot((grp + pos_ref[l:l + 1, :]).astype(BF16), w1_ref[l])
        hi = hi + _dot((grp + pos_ref[HALF + l:HALF + l + 1, :]).astype(BF16), w1_ref[HALF + l])
    hid = lo + pltpu.roll(hi, NB - 1, 0)
    blk = lax.broadcasted_iota(jnp.int32, hid.shape, 0)
    hid = jnp.where(blk < NB - 1, hid, 0.0)
    return _dot(_silu(hid).astype(BF16), w2_ref[...])


def _nsa_kernel(q_ref, gt_ref, z_ref, kraw_ref, vraw_ref, posk_ref, w1k_ref, w2k_ref, posv_ref, w1v_ref, w2v_ref,
                ks_ref, vs_ref, kw_ref, vw_ref, g_ref, o_ref, m_ref, acc_ref, ocmp_ref, sel_ref, kc_ref, vc_ref):
    TQ = q_ref.shape[0]
    S = ks_ref.shape[0]
    NB = kc_ref.shape[0]
    H, Dh = N_HEADS, HEAD_DIM
    scale = Dh ** -0.5
    i = pl.program_id(1)
    t0 = i * TQ

    @pl.when(i == 0)
    def _():
        kc_ref[...] = _compress(kraw_ref, posk_ref, w1k_ref, w2k_ref)
        vc_ref[...] = _compress(vraw_ref, posv_ref, w1v_ref, w2v_ref)

    q = q_ref[...]
    q4 = jnp.concatenate([q[:, h * Dh:(h + 1) * Dh] for h in range(H)], axis=0).astype(BF16)

    n_sel = S // SLC_LEN
    RANK_STEP = 8

    def selection_stages():
        st = {}

        def compressed():
            t_c = t0 + lax.broadcasted_iota(jnp.int32, (TQ, NB), 0)
            n_c = lax.broadcasted_iota(jnp.int32, (TQ, NB), 1)
            cmask = t_c >= n_c * CMP_STRIDE + (CMP_LEN - 1)
            cmask4 = jnp.concatenate([cmask] * H, axis=0)
            p4 = _masked_softmax(_dot_t(q4, kc_ref[...].astype(BF16)) * scale, cmask4)
            st['p4b'] = p4.astype(BF16)
            ocmp_ref[...] = _dot(st['p4b'], vc_ref[...].astype(BF16))

        def importance():
            jj = lax.broadcasted_iota(jnp.int32, (n_sel, NB), 0)
            nn = lax.broadcasted_iota(jnp.int32, (n_sel, NB), 1)
            ov = ((nn * CMP_STRIDE < (jj + 1) * SLC_LEN) & (nn * CMP_STRIDE + CMP_LEN > jj * SLC_LEN)
                  & (nn < NB - 1))
            po = _dot_t(jnp.where(ov, 1.0, 0.0).astype(BF16), st['p4b'])
            imp = po[:, 0:TQ]
            for h in range(1, H):
                imp = imp + po[:, h * TQ:(h + 1) * TQ]
            j = lax.broadcasted_iota(jnp.int32, (n_sel, TQ), 0)
            cur = (t0 + lax.broadcasted_iota(jnp.int32, (n_sel, TQ), 1)) // SLC_LEN
            forced = (j == 0) | (j == cur) | (j == cur - 1)
            st['j'] = j
            st['val'] = jnp.where(j > cur, -jnp.inf, jnp.where(forced, FORCED_SCORE, imp))
            st['rank'] = jnp.zeros((n_sel, TQ), F32)

        def rank_part(lo):
            val, j = st['val'], st['j']
            for ii in range(lo, lo + RANK_STEP):
                vi = val[ii:ii + 1, :]
                beats = jnp.where(vi > val, 1.0, jnp.where(vi == val, jnp.where(j > ii, 1.0, 0.0), 0.0))
                st['rank'] = st['rank'] + beats

        def select():
            sel_t = jnp.where(st['rank'] < SLC_TOPN, 1.0, 0.0)
            sel_ref[...] = jnp.concatenate([sel_t, jnp.zeros((LANES - n_sel, TQ), F32)], axis=0).T.astype(BF16)

        return ([compressed, importance] + [functools.partial(rank_part, lo) for lo in range(0, n_sel, RANK_STEP)]
                + [select])

    t_k = t0 + lax.broadcasted_iota(jnp.int32, (TQ, TQ), 0)
    k_i = lax.broadcasted_iota(jnp.int32, (TQ, TQ), 1)
    ones_half = jnp.ones((TQ, Dh), BF16)
    neg_inf = jnp.full((TQ, TQ), -jnp.inf, F32)
    scale_log2e = scale * LOG2_E

    def reset():
        m_ref[...] = jnp.full(m_ref.shape, -jnp.inf, F32)
        acc_ref[...] = jnp.zeros(acc_ref.shape, F32)

    def attend_chunk(k_ref, v_ref, c, bias, beside=lambda: None):
        k0 = pl.multiple_of(c * TQ, TQ)
        kb = k_ref[pl.ds(k0, TQ), :].astype(BF16)
        vb = jnp.concatenate([v_ref[pl.ds(k0, TQ), :].astype(BF16), ones_half], axis=-1)
        s4 = _dot_t(q4, kb)
        for h in range(H):
            rs = slice(h * TQ, (h + 1) * TQ)
            s = s4[rs] if bias is None else s4[rs] + bias
            m_old = m_ref[rs]
            m_new = jnp.maximum(m_old, jnp.max(s, axis=-1, keepdims=True))
            m_safe = jnp.where(m_new == -jnp.inf, 0.0, m_new)
            p = jnp.exp2((s - jnp.concatenate([m_safe] * (TQ // LANES), axis=-1)) * scale_log2e)
            alpha = jnp.exp2((m_old - m_safe) * scale_log2e)
            acc_ref[rs] = jnp.concatenate([alpha, alpha], axis=-1) * acc_ref[rs] + _dot(p.astype(BF16), vb)
            m_ref[rs] = m_new
            beside()

    def attend_window_ends(k_ref, v_ref, beside):
        old = k_i > (t_k - t0)
        k_old, k_new = pl.multiple_of((i - n_back) * TQ, TQ), pl.multiple_of(i * TQ, TQ)
        s4_old = _dot_t(q4, k_ref[pl.ds(k_old, TQ), :].astype(BF16))
        s4_new = _dot_t(q4, k_ref[pl.ds(k_new, TQ), :].astype(BF16))
        vb_old = jnp.concatenate([v_ref[pl.ds(k_old, TQ), :].astype(BF16), ones_half], axis=-1)
        vb_new = jnp.concatenate([v_ref[pl.ds(k_new, TQ), :].astype(BF16), ones_half], axis=-1)
        zero = jnp.zeros((TQ, TQ), BF16)
        for h in range(H):
            rs = slice(h * TQ, (h + 1) * TQ)
            s = jnp.where(old, s4_old[rs], s4_new[rs])
            m_old = m_ref[rs]
            m_new = jnp.maximum(m_old, jnp.max(s, axis=-1, keepdims=True))
            p = jnp.exp2((s - jnp.concatenate([m_new] * (TQ // LANES), axis=-1)) * scale_log2e).astype(BF16)
            alpha = jnp.exp2((m_old - m_new) * scale_log2e)
            acc_ref[rs] = (jnp.concatenate([alpha, alpha], axis=-1) * acc_ref[rs]
                           + _dot(jnp.where(old, p, zero), vb_old) + _dot(jnp.where(old, zero, p), vb_new))
            m_ref[rs] = m_new
            beside()

    def finish():
        acc = acc_ref[...]
        return acc[:, :Dh] / jnp.maximum(acc[:, Dh:], 1e-30)

    reset()
    n_back = WIN // TQ

    def win_chunk(back, beside):
        c = i - back
        if 0 < back < n_back:
            bias = None
        else:
            dist = t_k - (c * TQ + k_i)
            bias = jnp.where((dist >= 0) & (dist < WIN), 0.0, neg_inf)
        attend_chunk(kw_ref, vw_ref, c, bias, beside)

    for first in range(n_back, -1, -1):
        def win_chunks(first=first):
            stages = selection_stages()

            def next_stage():
                if stages:
                    stages.pop(0)()

            if first == n_back:
                for back in range(n_back - 1, 0, -1):
                    win_chunk(back, next_stage)
                attend_window_ends(kw_ref, vw_ref, next_stage)
            else:
                for back in range(first, -1, -1):
                    win_chunk(back, next_stage)
            while stages:
                next_stage()
        pl.when(jnp.minimum(i, n_back) == first)(win_chunks)
    o_win4 = finish()
    sel = sel_ref[...]

    eb = lax.broadcasted_iota(jnp.int32, (LANES, TQ), 0)
    ek = lax.broadcasted_iota(jnp.int32, (LANES, TQ), 1) // SLC_LEN

    def sel_keys(c):
        expand = jnp.where(eb == ek + c * (TQ // SLC_LEN), 1.0, 0.0).astype(BF16)
        return _dot(sel, expand) > 0.5

    reset()

    def slc_chunk(c):
        attend_chunk(ks_ref, vs_ref, c, jnp.where(sel_keys(c), 0.0, neg_inf))

    def slc_diagonal():
        attend_chunk(ks_ref, vs_ref, i, jnp.where(sel_keys(i) & (t0 + k_i <= t_k), 0.0, neg_inf))

    def slc_group(cg, carry):
        for u in range(SLC_GROUP):
            slc_chunk(SLC_GROUP * cg + u)
        return carry

    lax.fori_loop(0, i // SLC_GROUP, slc_group, 0)
    for rem in range(SLC_GROUP):
        def slc_tail(rem=rem):
            for u in range(rem, 0, -1):
                slc_chunk(i - u)
            slc_diagonal()
        pl.when(i % SLC_GROUP == rem)(slc_tail)
    o_slc4 = finish()

    o_cmp4 = ocmp_ref[...]
    gs = jax.nn.sigmoid(gt_ref[...])
    outs = []
    for h in range(H):
        rs = slice(h * TQ, (h + 1) * TQ)
        outs.append(gs[:, 3 * h:3 * h + 1] * o_cmp4[rs] + gs[:, 3 * h + 1:3 * h + 2] * o_slc4[rs]
                    + gs[:, 3 * h + 2:3 * h + 3] * o_win4[rs])
    o_ref[...] = _head_norm_gate(jnp.concatenate(outs, axis=-1), z_ref[...], g_ref[...])


def _nsa(proj, cmp_w, bg, l, B, S, TQ=256):
    assert WIN % TQ == 0 and TQ % SLC_LEN == 0 and TQ % LANES == 0
    nt = S // TQ
    T = B * S
    NB = S // CMP_STRIDE

    def seq(name):
        off, wp = _DST[name]
        return pl.BlockSpec((S, wp), lambda b, i: (b, off // wp))

    return pl.pallas_call(
        _nsa_kernel,
        out_shape=jax.ShapeDtypeStruct((T, W_BR), BF16),
        grid=(B, nt),
        in_specs=[_seg_spec('nsa_q', TQ, nt), _seg_spec('nsa_g', TQ, nt), _seg_spec('nsa_z', TQ, nt),
                  seq('nsa_kc'), seq('nsa_vc')] + [_layer_spec(w, l) for w in cmp_w]
                 + [seq('nsa_ks'), seq('nsa_vs'), seq('nsa_kw'), seq('nsa_vw'), _layer_spec(bg, l, 2)],
        out_specs=pl.BlockSpec((TQ, W_BR), lambda b, i: (b * nt + i, 0)),
        scratch_shapes=[pltpu.VMEM((N_HEADS * TQ, LANES), F32),
                        pltpu.VMEM((N_HEADS * TQ, 2 * HEAD_DIM), F32),
                        pltpu.VMEM((N_HEADS * TQ, HEAD_DIM), F32), pltpu.VMEM((TQ, LANES), BF16),
                        pltpu.VMEM((NB, HEAD_DIM), F32), pltpu.VMEM((NB, HEAD_DIM), F32)],
        compiler_params=_params("parallel", "arbitrary"),
        name="nsa_attention",
    )(proj, proj, proj, proj, proj, *cmp_w, proj, proj, proj, proj, bg)


def kernel(x, pre_norm_g, w_in, gla_w_fg2, gla_b_fg2, lru_conv_w, lru_conv_b, lru_w_a, lru_b_a, lru_w_x,
           lru_b_x, lru_lambda, nsa_cmp_pos_k, nsa_cmp_w1_k, nsa_cmp_w2_k, nsa_cmp_pos_v, nsa_cmp_w1_v,
           nsa_cmp_w2_v, conf_dw_w, conf_dw_b, conf_ln_g, conf_ln_b, conf_pw_w, conf_pw_b, branch_norm_g,
           w_out, post_norm_g):
    B, S, D = x.shape
    L = w_in.shape[0]
    T = B * S
    x2 = x.reshape(T, D)

    def rows(a):
        return a.reshape(L, 1, -1)

    w_in_t = _relayout_w_in(w_in)
    lowrank = gla_w_fg2.shape[1]
    w1k = nsa_cmp_w1_k.reshape(L, CMP_LEN, HEAD_DIM, -1).astype(BF16)
    w1v = nsa_cmp_w1_v.reshape(L, CMP_LEN, HEAD_DIM, -1).astype(BF16)
    w2k = nsa_cmp_w2_k.astype(BF16)
    w2v = nsa_cmp_w2_v.astype(BF16)
    bg = branch_norm_g.reshape(L, N_MIXERS, 1, W_BR)
    p = dict(
        bg=bg, pre_g=rows(pre_norm_g), post_g=rows(post_norm_g), w_out=w_out.astype(BF16),
        wfg=jnp.pad(gla_w_fg2, ((0, 0), (0, LANES - lowrank), (0, 0))).astype(BF16), bfg=rows(gla_b_fg2),
        lru_cw=lru_conv_w, lru_cb=rows(lru_conv_b), wa=lru_w_a.astype(BF16), lru_ba=rows(lru_b_a),
        wx=lru_w_x.astype(BF16), lru_bx=rows(lru_b_x), lru_lam=rows(lru_lambda),
        conf_dw=conf_dw_w, conf_db=rows(conf_dw_b), conf_lg=rows(conf_ln_g), conf_lb=rows(conf_ln_b),
        pw=conf_pw_w.astype(BF16), conf_pb=rows(conf_pw_b))

    hn = _prenorm(x2, p['pre_g'], 0)
    for l in range(L):
        proj = _in_proj(hn, w_in_t, l)
        m_c = _nsa(proj, (nsa_cmp_pos_k, w1k, w2k, nsa_cmp_pos_v, w1v, w2v), bg, l, B, S)
        if l + 1 < L:
            x2, hn = _mix_out(proj, m_c, x2, p, l, B, S, emit_hn=True)
        else:
            x2 = _mix_out(proj, m_c, x2, p, l, B, S, emit_hn=False)
    return x2.reshape(B, S, D)
```

```python
import functools

import jax
import jax.numpy as jnp
from jax import lax
from jax.experimental import pallas as pl
from jax.experimental.pallas import tpu as pltpu

F32 = jnp.float32
BF16 = jnp.bfloat16

D_MODEL = 2048
N_HEADS = 4
HEAD_DIM = 128
W_BR = N_HEADS * HEAD_DIM
N_MIXERS = 4
GLA_HEAD_K = 64
GLA_TAU = 16.0
GLA_CHUNK = 64
LRU_C = 8.0
LRU_CONV = 4
CMP_LEN = 32
CMP_STRIDE = 16
SLC_LEN = 64
SLC_TOPN = 16
SLC_GROUP = 2
WIN = 512
FORCED_SCORE = 1e3
CONF_KERNEL = 31
CONF_HIST = 32
EPS = 1e-6
LOG2_E = 1.4426950408889634
SUBLANES = 8
LANES = 128
VMEM_LIMIT = 48 * 1024 * 1024

_SEGS = (('gla_q', 256), ('gla_k', 256), ('gla_v', 512), ('gla_fg', 16), ('gla_z', 512),
         ('lru_x', 512), ('lru_z', 512),
         ('nsa_q', 512), ('nsa_kc', 128), ('nsa_vc', 128), ('nsa_ks', 128), ('nsa_vs', 128),
         ('nsa_kw', 128), ('nsa_vw', 128), ('nsa_g', 12), ('nsa_z', 512),
         ('conv_v', 512), ('conv_glu', 512), ('conv_z', 512))
_ORDER = ('gla_v', 'gla_z', 'lru_x', 'lru_z', 'nsa_q', 'nsa_z', 'conv_v', 'conv_glu', 'conv_z',
          'gla_q', 'gla_k', 'gla_fg', 'nsa_kc', 'nsa_vc', 'nsa_ks', 'nsa_vs', 'nsa_kw', 'nsa_vw',
          'nsa_g')


def _layout():
    src, off = {}, 0
    for name, w in _SEGS:
        src[name] = (off, w)
        off += w
    dst, off = {}, 0
    for name in _ORDER:
        w = src[name][1]
        wp = -(-w // LANES) * LANES
        assert off % wp == 0
        dst[name] = (off, wp)
        off += wp
    return src, dst, off


_SRC, _DST, D_PROJ_PAD = _layout()


def _relayout_kernel(off_ref, valid_ref, w_ref, o_ref):
    del off_ref
    rows = lax.broadcasted_iota(jnp.int32, o_ref.shape[1:], 0)
    keep = rows < valid_ref[pl.program_id(0)]
    for l in range(o_ref.shape[0]):
        o_ref[l] = jnp.where(keep, w_ref[:, l, :], 0.0).astype(BF16)


def _relayout_w_in(w_in):
    L, D, NP = w_in.shape
    wt = jnp.transpose(w_in, (2, 0, 1))
    offs, valid = [], []
    for name in _ORDER:
        o, w = _SRC[name]
        for b in range(_DST[name][1] // LANES):
            offs.append(o + b * LANES)
            valid.append(min(LANES, w - b * LANES))
    assert max(offs) + LANES <= NP and min(valid) > 0
    grid_spec = pltpu.PrefetchScalarGridSpec(
        num_scalar_prefetch=2,
        grid=(len(offs),),
        in_specs=[pl.BlockSpec((pl.Element(LANES), pl.Element(L), pl.Element(D)),
                               lambda j, off, valid: (off[j], 0, 0))],
        out_specs=pl.BlockSpec((L, LANES, D), lambda j, off, valid: (0, j, 0)))
    return pl.pallas_call(
        _relayout_kernel,
        out_shape=jax.ShapeDtypeStruct((L, D_PROJ_PAD, D), BF16),
        grid_spec=grid_spec,
        compiler_params=_params("parallel"),
        name="relayout_w_in",
    )(jnp.array(offs, jnp.int32), jnp.array(valid, jnp.int32), wt)


def _dot(a, b):
    return jnp.dot(a, b, preferred_element_type=F32)


def _dot_t(a, b):
    return lax.dot_general(a, b, (((1,), (1,)), ((), ())), preferred_element_type=F32)


def _softplus(x):
    return jnp.maximum(x, 0.0) + jnp.log1p(jnp.exp(-jnp.abs(x)))


def _expm1(x):
    return jnp.tanh(0.5 * x) * (jnp.exp(x) + 1.0)


def _silu(x):
    return x * jax.nn.sigmoid(x)


def _rmsnorm(x, g):
    return x * lax.rsqrt(jnp.mean(x * x, axis=-1, keepdims=True) + EPS) * g


def _masked_softmax(s, mask):
    s = jnp.where(mask, s, -jnp.inf)
    m = jnp.max(s, axis=-1, keepdims=True)
    m = jnp.where(jnp.isfinite(m), m, 0.0)
    p = jnp.exp(s - m)
    return p / jnp.maximum(jnp.sum(p, axis=-1, keepdims=True), 1e-30)


def _head_norm_gate(o, z, g):
    outs = []
    for h in range(N_HEADS):
        oh = o[:, h * HEAD_DIM:(h + 1) * HEAD_DIM]
        outs.append(oh * lax.rsqrt(jnp.mean(oh * oh, axis=-1, keepdims=True) + EPS))
    on = jnp.concatenate(outs, axis=-1) * g
    return (on * _silu(z)).astype(BF16)


def _layer_spec(arr, *lead):
    rest = arr.shape[len(lead):]
    idx = tuple(lead) + (0,) * len(rest)
    return pl.BlockSpec((None,) * len(lead) + rest, lambda *_: idx)


def _seg_spec(name, rows, nt):
    off, wp = _DST[name]
    cb = off // wp
    return pl.BlockSpec((rows, wp), lambda b, i: (b * nt + i, cb))


def _params(*sem):
    return pltpu.CompilerParams(dimension_semantics=sem, vmem_limit_bytes=VMEM_LIMIT)


def _prenorm_kernel(x_ref, g_ref, o_ref):
    o_ref[...] = _rmsnorm(x_ref[...], g_ref[...]).astype(BF16)


def _prenorm(x2, g, l, tm=512):
    T, D = x2.shape
    return pl.pallas_call(
        _prenorm_kernel,
        out_shape=jax.ShapeDtypeStruct((T, D), BF16),
        grid=(T // tm,),
        in_specs=[pl.BlockSpec((tm, D), lambda i: (i, 0)), _layer_spec(g, l)],
        out_specs=pl.BlockSpec((tm, D), lambda i: (i, 0)),
        compiler_params=_params("parallel"),
        name="prenorm",
    )(x2, g)


def _in_proj_kernel(h_ref, wt_ref, o_ref):
    o_ref[...] = _dot_t(h_ref[...], wt_ref[...])


def _in_proj(hn, wt, l, tm=1024, tn=2048):
    T, D = hn.shape
    NP = wt.shape[1]
    return pl.pallas_call(
        _in_proj_kernel,
        out_shape=jax.ShapeDtypeStruct((T, NP), F32),
        grid=(T // tm, NP // tn),
        in_specs=[pl.BlockSpec((tm, D), lambda i, j: (i, 0)),
                  pl.BlockSpec((None, tn, D), lambda i, j: (l, j, 0))],
        out_specs=pl.BlockSpec((tm, tn), lambda i, j: (i, j)),
        compiler_params=_params("parallel", "arbitrary"),
        name="in_proj",
    )(hn, wt)


N_GLA_IN, N_LRU_IN, N_CONF_IN = 8, 10, 10


def _mix_out_kernel(*refs, emit_hn):
    refs = list(refs)
    gla_in = [refs.pop(0) for _ in range(N_GLA_IN)]
    lru_in = [refs.pop(0) for _ in range(N_LRU_IN)]
    conf_in = [refs.pop(0) for _ in range(N_CONF_IN)]
    mc_ref, w_ref, x_ref, pg_ref = [refs.pop(0) for _ in range(4)]
    gn_ref = refs.pop(0) if emit_hn else None
    o_ref = refs.pop(0)
    hn_ref = refs.pop(0) if emit_hn else None
    gla_state, gla_acc, lru_hist, lru_h, lru_acc, conf_hist = refs

    @pl.when(pl.program_id(1) == 0)
    def _():
        for ref in (gla_state, lru_hist, lru_h, conf_hist):
            ref[...] = jnp.zeros(ref.shape, F32)

    gla_stages, gla_finish = _gla_stages(*gla_in, gla_state, gla_acc)

    def next_gla_stage():
        if gla_stages:
            gla_stages.pop(0)()

    m_b = _lru_mix(*lru_in, lru_hist, lru_h, lru_acc)
    m_d = _conf_mix(*conf_in, conf_hist, next_gla_stage)
    while gla_stages:
        next_gla_stage()
    mixed = jnp.concatenate([gla_finish(), m_b, mc_ref[...], m_d], axis=-1)
    x_new = x_ref[...] + _rmsnorm(_dot(mixed, w_ref[...]), pg_ref[...])
    o_ref[...] = x_new
    if emit_hn:
        hn_ref[...] = _rmsnorm(x_new, gn_ref[...]).astype(BF16)


def _mix_out(proj, m_c, x2, p, l, B, S, emit_hn, R=256):
    nt = S // R
    T, D = x2.shape
    bg = p['bg']

    def tile(width):
        return pl.BlockSpec((R, width), lambda b, i: (b * nt + i, 0))

    def seg(*names):
        return [_seg_spec(n, R, nt) for n in names]

    def layer(*names):
        return [_layer_spec(p[n], l) for n in names]

    gla = seg('gla_q', 'gla_k', 'gla_v', 'gla_fg', 'gla_z') + layer('wfg', 'bfg') + [_layer_spec(bg, l, 0)]
    lru = (seg('lru_x', 'lru_z') + layer('lru_cw', 'lru_cb', 'wa', 'lru_ba', 'wx', 'lru_bx', 'lru_lam')
           + [_layer_spec(bg, l, 1)])
    conf = (seg('conv_v', 'conv_glu', 'conv_z') + layer('conf_dw', 'conf_db', 'conf_lg', 'conf_lb', 'pw', 'conf_pb')
            + [_layer_spec(bg, l, 3)])
    assert (len(gla), len(lru), len(conf)) == (N_GLA_IN, N_LRU_IN, N_CONF_IN)
    in_specs = gla + lru + conf + [tile(W_BR), _layer_spec(p['w_out'], l), tile(D), _layer_spec(p['post_g'], l)]
    args = ([proj] * 5 + [p['wfg'], p['bfg'], bg]
            + [proj] * 2 + [p[n] for n in ('lru_cw', 'lru_cb', 'wa', 'lru_ba', 'wx', 'lru_bx', 'lru_lam')] + [bg]
            + [proj] * 3 + [p[n] for n in ('conf_dw', 'conf_db', 'conf_lg', 'conf_lb', 'pw', 'conf_pb')] + [bg]
            + [m_c, p['w_out'], x2, p['post_g']])
    out_shape = jax.ShapeDtypeStruct((T, D), F32)
    out_specs = tile(D)
    if emit_hn:
        in_specs.append(_layer_spec(p['pre_g'], l + 1))
        args.append(p['pre_g'])
        out_shape = (out_shape, jax.ShapeDtypeStruct((T, D), BF16))
        out_specs = (tile(D), tile(D))
    return pl.pallas_call(
        functools.partial(_mix_out_kernel, emit_hn=emit_hn),
        out_shape=out_shape,
        grid=(B, nt),
        in_specs=in_specs,
        out_specs=out_specs,
        scratch_shapes=[pltpu.VMEM((N_HEADS, HEAD_DIM, LANES), F32), pltpu.VMEM((R, W_BR), F32),
                        pltpu.VMEM((SUBLANES, W_BR), F32), pltpu.VMEM((1, W_BR), F32),
                        pltpu.VMEM((R, W_BR), F32), pltpu.VMEM((CONF_HIST, W_BR), F32)],
        compiler_params=_params("parallel", "arbitrary"),
        name="mix_out",
    )(*args)


def _gla_stages(q_ref, k_ref, v_ref, fg_ref, z_ref, wfg_ref, bfg_ref, g_ref, state_ref, acc_ref):
    R = q_ref.shape[0]
    C = GLA_CHUNK

    pre = _dot(fg_ref[...].astype(BF16), wfg_ref[...]) + bfg_ref[...]
    log_f = -_softplus(-pre) * (1.0 / GLA_TAU)
    row = lax.broadcasted_iota(jnp.int32, log_f.shape, 0) % C
    bcum = log_f
    d = 1
    while d < C:
        bcum = bcum + jnp.where(row >= d, pltpu.roll(bcum, d, 0), 0.0)
        d *= 2

    q = q_ref[...] * (GLA_HEAD_K ** -0.5)
    k = k_ref[...]
    v = v_ref[...]
    lane = lax.broadcasted_iota(jnp.int32, (C, LANES), 1)
    causal = (lax.broadcasted_iota(jnp.int32, (C, C), 0) >= lax.broadcasted_iota(jnp.int32, (C, C), 1))
    head_lanes = (lane < GLA_HEAD_K, lane >= GLA_HEAD_K)

    def stage(c, p):
        rs = slice(c * C, (c + 1) * C)
        ls = slice(p * LANES, (p + 1) * LANES)
        b = bcum[rs, ls]
        b_last = b[C - 1:C, :]
        kk = k[rs, ls]
        q_dec = q[rs, ls] * jnp.exp(b)
        k_dec = kk * jnp.exp(-b)
        k_last = kk * jnp.exp(b_last - b)
        decay = jnp.exp(b_last)
        for hh in range(2):
            h = 2 * p + hh
            m = head_lanes[hh]
            qd = jnp.where(m, q_dec, 0.0).astype(BF16)
            kd = jnp.where(m, k_dec, 0.0).astype(BF16)
            kl = jnp.where(m, k_last, 0.0).astype(BF16)
            vh = v[rs, h * HEAD_DIM:(h + 1) * HEAD_DIM]
            attn = jnp.where(causal, _dot_t(qd, kd), 0.0)
            st = state_ref[h]
            o = _dot(attn.astype(BF16), vh.astype(BF16)) + _dot_t(qd, st.astype(BF16))
            acc_ref[rs, h * HEAD_DIM:(h + 1) * HEAD_DIM] = o
            state_ref[h] = decay * st + _dot(vh.T.astype(BF16), kl)

    def finish():
        return _head_norm_gate(acc_ref[...], z_ref[...], g_ref[...])

    stages = [functools.partial(stage, c, p) for c in range(R // C) for p in range(2)]
    return stages, finish


def _lru_mix(x_ref, z_ref, cw_ref, cb_ref, wa_ref, ba_ref, wx_ref, bx_ref, lam_ref, g_ref,
             hist_ref, h_ref, acc_ref):
    R = x_ref.shape[0]
    HIST = hist_ref.shape[0]

    x = x_ref[...]
    xe = jnp.concatenate([hist_ref[...], x], axis=0)
    xc = cb_ref[...] + cw_ref[LRU_CONV - 1:LRU_CONV, :] * x
    for kk in range(LRU_CONV - 1):
        sh = LRU_CONV - 1 - kk
        xc = xc + cw_ref[kk:kk + 1, :] * pltpu.roll(xe, sh, 0)[HIST:HIST + R]
    hist_ref[...] = x[R - HIST:R]

    xb = xc.astype(BF16)
    ra, rx = [], []
    for h in range(N_HEADS):
        xh = xb[:, h * HEAD_DIM:(h + 1) * HEAD_DIM]
        ra.append(_dot(xh, wa_ref[h]))
        rx.append(_dot(xh, wx_ref[h]))
    r = jax.nn.sigmoid(jnp.concatenate(ra, axis=-1) + ba_ref[...])
    ig = jax.nn.sigmoid(jnp.concatenate(rx, axis=-1) + bx_ref[...])
    log_a = -LRU_C * r * _softplus(-lam_ref[...])
    a = jnp.exp(log_a)
    u = jnp.sqrt(-_expm1(2.0 * log_a)) * (ig * xc)

    G = R // SUBLANES
    a3 = a.reshape(G, SUBLANES, W_BR)
    u3 = u.reshape(G, SUBLANES, W_BR)
    row = lax.broadcasted_iota(jnp.int32, a3.shape, 1)
    d = 1
    while d < SUBLANES:
        keep = row >= d
        u3 = jnp.where(keep, a3 * pltpu.roll(u3, d, 1) + u3, u3)
        a3 = jnp.where(keep, a3 * pltpu.roll(a3, d, 1), a3)
        d *= 2
    carry = h_ref[...]
    for gi in range(G):
        hg = a3[gi] * carry + u3[gi]
        acc_ref[gi * SUBLANES:(gi + 1) * SUBLANES, :] = hg
        carry = hg[SUBLANES - 1:SUBLANES, :]
    h_ref[...] = carry
    return _head_norm_gate(acc_ref[...], z_ref[...], g_ref[...])


def _conf_mix(v_ref, glu_ref, z_ref, dw_ref, db_ref, lg_ref, lb_ref, pw_ref, pb_ref, g_ref, hist_ref, beside):
    R = v_ref.shape[0]
    HIST = hist_ref.shape[0]

    y = v_ref[...] * jax.nn.sigmoid(glu_ref[...])
    ye = jnp.concatenate([hist_ref[...], y], axis=0)
    acc = db_ref[...]
    for r in range(SUBLANES):
        yr = ye if r == 0 else pltpu.roll(ye, r, 0)
        for q in range((CONF_KERNEL - 1 - r) // SUBLANES + 1):
            kk = CONF_KERNEL - 1 - (SUBLANES * q + r)
            start = HIST - SUBLANES * q
            acc = acc + dw_ref[kk:kk + 1, :] * yr[start:start + R]
        beside()
    hist_ref[...] = y[R - HIST:R]

    mu = jnp.mean(acc, axis=-1, keepdims=True)
    xc = acc - mu
    var = jnp.mean(xc * xc, axis=-1, keepdims=True)
    yn = xc * lax.rsqrt(var + EPS) * lg_ref[...] + lb_ref[...]
    o = _dot(_silu(yn).astype(BF16), pw_ref[...]) + pb_ref[...]
    return _head_norm_gate(o, z_ref[...], g_ref[...])


def _compress(z_ref, pos_ref, w1_ref, w2_ref):
    NB = z_ref.shape[0] // CMP_STRIDE
    HALF = CMP_LEN // 2
    lo = jnp.zeros((NB, w1_ref.shape[2]), F32)
    hi = jnp.zeros((NB, w1_ref.shape[2]), F32)
    for l in range(HALF):
        grp = z_ref[pl.ds(l, NB, stride=CMP_STRIDE), :]
        lo = lo + _dot((grp + pos_ref[l:l + 1, :]).astype(BF16), w1_ref[l])
        hi = hi + _dot((grp + pos_ref[HALF + l:HALF + l + 1, :]).astype(BF16), w1_ref[HALF + l])
    hid = lo + pltpu.roll(hi, NB - 1, 0)
    blk = lax.broadcasted_iota(jnp.int32, hid.shape, 0)
    hid = jnp.where(blk < NB - 1, hid, 0.0)
    return _dot(_silu(hid).astype(BF16), w2_ref[...])


def _nsa_kernel(q_ref, gt_ref, z_ref, kraw_ref, vraw_ref, posk_ref, w1k_ref, w2k_ref, posv_ref, w1v_ref, w2v_ref,
                ks_ref, vs_ref, kw_ref, vw_ref, g_ref, o_ref, m_ref, acc_ref, ocmp_ref, sel_ref, kc_ref, vc_ref):
    TQ = q_ref.shape[0]
    S = ks_ref.shape[0]
    NB = kc_ref.shape[0]
    H, Dh = N_HEADS, HEAD_DIM
    scale = Dh ** -0.5
    i = pl.program_id(1)
    t0 = i * TQ

    @pl.when(i == 0)
    def _():
        kc_ref[...] = _compress(kraw_ref, posk_ref, w1k_ref, w2k_ref)
        vc_ref[...] = _compress(vraw_ref, posv_ref, w1v_ref, w2v_ref)

    q = q_ref[...]
    q4 = jnp.concatenate([q[:, h * Dh:(h + 1) * Dh] for h in range(H)], axis=0).astype(BF16)

    n_sel = S // SLC_LEN
    RANK_STEP = 8

    def selection_stages():
        st = {}

        def compressed():
            t_c = t0 + lax.broadcasted_iota(jnp.int32, (TQ, NB), 0)
            n_c = lax.broadcasted_iota(jnp.int32, (TQ, NB), 1)
            cmask = t_c >= n_c * CMP_STRIDE + (CMP_LEN - 1)
            cmask4 = jnp.concatenate([cmask] * H, axis=0)
            p4 = _masked_softmax(_dot_t(q4, kc_ref[...].astype(BF16)) * scale, cmask4)
            st['p4b'] = p4.astype(BF16)
            ocmp_ref[...] = _dot(st['p4b'], vc_ref[...].astype(BF16))

        def importance():
            jj = lax.broadcasted_iota(jnp.int32, (n_sel, NB), 0)
            nn = lax.broadcasted_iota(jnp.int32, (n_sel, NB), 1)
            ov = ((nn * CMP_STRIDE < (jj + 1) * SLC_LEN) & (nn * CMP_STRIDE + CMP_LEN > jj * SLC_LEN)
                  & (nn < NB - 1))
            po = _dot_t(jnp.where(ov, 1.0, 0.0).astype(BF16), st['p4b'])
            imp = po[:, 0:TQ]
            for h in range(1, H):
                imp = imp + po[:, h * TQ:(h + 1) * TQ]
            j = lax.broadcasted_iota(jnp.int32, (n_sel, TQ), 0)
            cur = (t0 + lax.broadcasted_iota(jnp.int32, (n_sel, TQ), 1)) // SLC_LEN
            forced = (j == 0) | (j == cur) | (j == cur - 1)
            st['j'] = j
            st['val'] = jnp.where(j > cur, -jnp.inf, jnp.where(forced, FORCED_SCORE, imp))
            st['rank'] = jnp.zeros((n_sel, TQ), F32)

        def rank_part(lo):
            val, j = st['val'], st['j']
            for ii in range(lo, lo + RANK_STEP):
                vi = val[ii:ii + 1, :]
                beats = jnp.where(vi > val, 1.0, jnp.where(vi == val, jnp.where(j > ii, 1.0, 0.0), 0.0))
                st['rank'] = st['rank'] + beats

        def select():
            sel_t = jnp.where(st['rank'] < SLC_TOPN, 1.0, 0.0)
            sel_ref[...] = jnp.concatenate([sel_t, jnp.zeros((LANES - n_sel, TQ), F32)], axis=0).T.astype(BF16)

        return ([compressed, importance] + [functools.partial(rank_part, lo) for lo in range(0, n_sel, RANK_STEP)]
                + [select])

    t_k = t0 + lax.broadcasted_iota(jnp.int32, (TQ, TQ), 0)
    k_i = lax.broadcasted_iota(jnp.int32, (TQ, TQ), 1)
    ones_half = jnp.ones((TQ, Dh), BF16)
    neg_inf = jnp.full((TQ, TQ), -jnp.inf, F32)
    scale_log2e = scale * LOG2_E

    def reset():
        m_ref[...] = jnp.full(m_ref.shape, -jnp.inf, F32)
        acc_ref[...] = jnp.zeros(acc_ref.shape, F32)

    def attend_chunk(k_ref, v_ref, c, bias, beside=lambda: None):
        k0 = pl.multiple_of(c * TQ, TQ)
        kb = k_ref[pl.ds(k0, TQ), :].astype(BF16)
        vb = jnp.concatenate([v_ref[pl.ds(k0, TQ), :].astype(BF16), ones_half], axis=-1)
        s4 = _dot_t(q4, kb)
        for h in range(H):
            rs = slice(h * TQ, (h + 1) * TQ)
            s = s4[rs] if bias is None else s4[rs] + bias
            m_old = m_ref[rs]
            m_new = jnp.maximum(m_old, jnp.max(s, axis=-1, keepdims=True))
            m_safe = jnp.where(m_new == -jnp.inf, 0.0, m_new)
            p = jnp.exp2((s - jnp.concatenate([m_safe] * (TQ // LANES), axis=-1)) * scale_log2e)
            alpha = jnp.exp2((m_old - m_safe) * scale_log2e)
            acc_ref[rs] = jnp.concatenate([alpha, alpha], axis=-1) * acc_ref[rs] + _dot(p.astype(BF16), vb)
            m_ref[rs] = m_new
            beside()

    def attend_window_ends(k_ref, v_ref, beside):
        old = k_i > (t_k - t0)
        k_old, k_new = pl.multiple_of((i - n_back) * TQ, TQ), pl.multiple_of(i * TQ, TQ)
        s4_old = _dot_t(q4, k_ref[pl.ds(k_old, TQ), :].astype(BF16))
        s4_new = _dot_t(q4, k_ref[pl.ds(k_new, TQ), :].astype(BF16))
        vb_old = jnp.concatenate([v_ref[pl.ds(k_old, TQ), :].astype(BF16), ones_half], axis=-1)
        vb_new = jnp.concatenate([v_ref[pl.ds(k_new, TQ), :].astype(BF16), ones_half], axis=-1)
        zero = jnp.zeros((TQ, TQ), BF16)
        for h in range(H):
            rs = slice(h * TQ, (h + 1) * TQ)
            s = jnp.where(old, s4_old[rs], s4_new[rs])
            m_old = m_ref[rs]
            m_new = jnp.maximum(m_old, jnp.max(s, axis=-1, keepdims=True))
            p = jnp.exp2((s - jnp.concatenate([m_new] * (TQ // LANES), axis=-1)) * scale_log2e).astype(BF16)
            alpha = jnp.exp2((m_old - m_new) * scale_log2e)
            acc_ref[rs] = (jnp.concatenate([alpha, alpha], axis=-1) * acc_ref[rs]
                           + _dot(jnp.where(old, p, zero), vb_old) + _dot(jnp.where(old, zero, p), vb_new))
            m_ref[rs] = m_new
            beside()

    def finish():
        acc = acc_ref[...]
        return acc[:, :Dh] / jnp.maximum(acc[:, Dh:], 1e-30)

    reset()
    n_back = WIN // TQ

    def win_chunk(back, beside):
        c = i - back
        if 0 < back < n_back:
            bias = None
        else:
            dist = t_k - (c * TQ + k_i)
            bias = jnp.where((dist >= 0) & (dist < WIN), 0.0, neg_inf)
        attend_chunk(kw_ref, vw_ref, c, bias, beside)

    for first in range(n_back, -1, -1):
        def win_chunks(first=first):
            stages = selection_stages()

            def next_stage():
                if stages:
                    stages.pop(0)()

            if first == n_back:
                for back in range(n_back - 1, 0, -1):
                    win_chunk(back, next_stage)
                attend_window_ends(kw_ref, vw_ref, next_stage)
            else:
                for back in range(first, -1, -1):
                    win_chunk(back, next_stage)
            while stages:
                next_stage()
        pl.when(jnp.minimum(i, n_back) == first)(win_chunks)
    o_win4 = finish()
    sel = sel_ref[...]

    eb = lax.broadcasted_iota(jnp.int32, (LANES, TQ), 0)
    ek = lax.broadcasted_iota(jnp.int32, (LANES, TQ), 1) // SLC_LEN

    def sel_keys(c):
        expand = jnp.where(eb == ek + c * (TQ // SLC_LEN), 1.0, 0.0).astype(BF16)
        return _dot(sel, expand) > 0.5

    reset()

    def slc_chunk(c):
        attend_chunk(ks_ref, vs_ref, c, jnp.where(sel_keys(c), 0.0, neg_inf))

    def slc_diagonal():
        attend_chunk(ks_ref, vs_ref, i, jnp.where(sel_keys(i) & (t0 + k_i <= t_k), 0.0, neg_inf))

    def slc_group(cg, carry):
        for u in range(SLC_GROUP):
            slc_chunk(SLC_GROUP * cg + u)
        return carry

    lax.fori_loop(0, i // SLC_GROUP, slc_group, 0)
    for rem in range(SLC_GROUP):
        def slc_tail(rem=rem):
            for u in range(rem, 0, -1):
                slc_chunk(i - u)
            slc_diagonal()
        pl.when(i % SLC_GROUP == rem)(slc_tail)
    o_slc4 = finish()

    o_cmp4 = ocmp_ref[...]
    gs = jax.nn.sigmoid(gt_ref[...])
    outs = []
    for h in range(H):
        rs = slice(h * TQ, (h + 1) * TQ)
        outs.append(gs[:, 3 * h:3 * h + 1] * o_cmp4[rs] + gs[:, 3 * h + 1:3 * h + 2] * o_slc4[rs]
                    + gs[:, 3 * h + 2:3 * h + 3] * o_win4[rs])
    o_ref[...] = _head_norm_gate(jnp.concatenate(outs, axis=-1), z_ref[...], g_ref[...])


def _nsa(proj, cmp_w, bg, l, B, S, TQ=256):
    assert WIN % TQ == 0 and TQ % SLC_LEN == 0 and TQ % LANES == 0
    nt = S // TQ
    T = B * S
    NB = S // CMP_STRIDE

    def seq(name):
        off, wp = _DST[name]
        return pl.BlockSpec((S, wp), lambda b, i: (b, off // wp))

    return pl.pallas_call(
        _nsa_kernel,
        out_shape=jax.ShapeDtypeStruct((T, W_BR), BF16),
        grid=(B, nt),
        in_specs=[_seg_spec('nsa_q', TQ, nt), _seg_spec('nsa_g', TQ, nt), _seg_spec('nsa_z', TQ, nt),
                  seq('nsa_kc'), seq('nsa_vc')] + [_layer_spec(w, l) for w in cmp_w]
                 + [seq('nsa_ks'), seq('nsa_vs'), seq('nsa_kw'), seq('nsa_vw'), _layer_spec(bg, l, 2)],
        out_specs=pl.BlockSpec((TQ, W_BR), lambda b, i: (b * nt + i, 0)),
        scratch_shapes=[pltpu.VMEM((N_HEADS * TQ, LANES), F32),
                        pltpu.VMEM((N_HEADS * TQ, 2 * HEAD_DIM), F32),
                        pltpu.VMEM((N_HEADS * TQ, HEAD_DIM), F32), pltpu.VMEM((TQ, LANES), BF16),
                        pltpu.VMEM((NB, HEAD_DIM), F32), pltpu.VMEM((NB, HEAD_DIM), F32)],
        compiler_params=_params("parallel", "arbitrary"),
        name="nsa_attention",
    )(proj, proj, proj, proj, proj, *cmp_w, proj, proj, proj, proj, bg)


def kernel(x, pre_norm_g, w_in, gla_w_fg2, gla_b_fg2, lru_conv_w, lru_conv_b, lru_w_a, lru_b_a, lru_w_x,
           lru_b_x, lru_lambda, nsa_cmp_pos_k, nsa_cmp_w1_k, nsa_cmp_w2_k, nsa_cmp_pos_v, nsa_cmp_w1_v,
           nsa_cmp_w2_v, conf_dw_w, conf_dw_b, conf_ln_g, conf_ln_b, conf_pw_w, conf_pw_b, branch_norm_g,
           w_out, post_norm_g):
    B, S, D = x.shape
    L = w_in.shape[0]
    T = B * S
    x2 = x.reshape(T, D)

    def rows(a):
        return a.reshape(L, 1, -1)

    w_in_t = _relayout_w_in(w_in)
    lowrank = gla_w_fg2.shape[1]
    w1k = nsa_cmp_w1_k.reshape(L, CMP_LEN, HEAD_DIM, -1).astype(BF16)
    w1v = nsa_cmp_w1_v.reshape(L, CMP_LEN, HEAD_DIM, -1).astype(BF16)
    w2k = nsa_cmp_w2_k.astype(BF16)
    w2v = nsa_cmp_w2_v.astype(BF16)
    bg = branch_norm_g.reshape(L, N_MIXERS, 1, W_BR)
    p = dict(
        bg=bg, pre_g=rows(pre_norm_g), post_g=rows(post_norm_g), w_out=w_out.astype(BF16),
        wfg=jnp.pad(gla_w_fg2, ((0, 0), (0, LANES - lowrank), (0, 0))).astype(BF16), bfg=rows(gla_b_fg2),
        lru_cw=lru_conv_w, lru_cb=rows(lru_conv_b), wa=lru_w_a.astype(BF16), lru_ba=rows(lru_b_a),
        wx=lru_w_x.astype(BF16), lru_bx=rows(lru_b_x), lru_lam=rows(lru_lambda),
        conf_dw=conf_dw_w, conf_db=rows(conf_dw_b), conf_lg=rows(conf_ln_g), conf_lb=rows(conf_ln_b),
        pw=conf_pw_w.astype(BF16), conf_pb=rows(conf_pw_b))

    hn = _prenorm(x2, p['pre_g'], 0)
    for l in range(L):
        proj = _in_proj(hn, w_in_t, l)
        m_c = _nsa(proj, (nsa_cmp_pos_k, w1k, w2k, nsa_cmp_pos_v, w1v, w2v), bg, l, B, S)
        if l + 1 < L:
            x2, hn = _mix_out(proj, m_c, x2, p, l, B, S, emit_hn=True)
        else:
            x2 = _mix_out(proj, m_c, x2, p, l, B, S, emit_hn=False)
    return x2.reshape(B, S, D)
```
